```python
import jax, jax.numpy as jnp
from jax import lax
import numpy as np

D_MODEL = 1024
BATCH = 2
SEQ = 8192
DEPTH = 2

POOL_WIDTH = 256
POOL_GROUPS = 4
POOL_WINDOWS = (2, 4, 8, 16)
CONV_WIDTH = 256
CONV_KERNEL = 31
HEAD_DIM = 64
N_Q_HEADS = 8
N_KV_HEADS = 2
Q_PER_KV = N_Q_HEADS // N_KV_HEADS
ATTN_WIDTH = N_Q_HEADS * HEAD_DIM
KV_WIDTH = N_KV_HEADS * HEAD_DIM
WINDOW = 128
BLOCK = 128
D_MIX = POOL_WIDTH + CONV_WIDTH + ATTN_WIDTH
D_IN = 2 * POOL_WIDTH + 3 * CONV_WIDTH + 2 * ATTN_WIDTH + 2 * KV_WIDTH
EPS = 1e-6

kernel_name = "hybrid_pool_conv_swa_block"


def rms_norm(x, g):
    xf = x.astype(jnp.float32)
    y = xf * lax.rsqrt(jnp.mean(xf * xf, axis=-1, keepdims=True) + EPS)
    return (y * g.astype(jnp.float32)).astype(x.dtype)


def layer_norm(x, g, b):
    xf = x.astype(jnp.float32)
    mu = jnp.mean(xf, axis=-1, keepdims=True)
    var = jnp.mean(jnp.square(xf - mu), axis=-1, keepdims=True)
    y = (xf - mu) * lax.rsqrt(var + EPS)
    return (y * g.astype(jnp.float32) + b.astype(jnp.float32)).astype(x.dtype)


def alibi_slopes():
    return jnp.asarray([2.0 ** (-8.0 * (h + 1) / N_Q_HEADS) for h in range(N_Q_HEADS)], dtype=jnp.float32)


def pool_mixer(u, w, scale):
    B, S, _ = u.shape
    gw = POOL_WIDTH // POOL_GROUPS
    uf = u.astype(jnp.float32)
    c = jnp.cumsum(uf, axis=1)
    t = jnp.arange(S, dtype=jnp.float32)[:, None]
    outs = []
    for g, wnd in enumerate(POOL_WINDOWS):
        cg = c[..., g * gw:(g + 1) * gw]
        prev = jnp.pad(cg, ((0, 0), (wnd, 0), (0, 0)))[:, :S]
        cnt = jnp.minimum(t + 1.0, float(wnd))
        outs.append((cg - prev) / cnt)
    pooled = jnp.stack(outs, axis=2)
    diff = (pooled - uf.reshape(B, S, POOL_GROUPS, gw)).astype(u.dtype)
    y = jnp.einsum('bsgc,gcd->bsgd', diff, w).reshape(B, S, POOL_WIDTH)
    return y * scale


def conv_module(a, b, dw, dw_bias, ln_g, ln_b, pw):
    h = a * jax.nn.sigmoid(b)
    h = jnp.pad(h, ((0, 0), (CONV_KERNEL - 1, 0), (0, 0)))
    h = lax.conv_general_dilated(h, dw[:, None, :], window_strides=(1,), padding='VALID',
                                 dimension_numbers=('NWC', 'WIO', 'NWC'),
                                 feature_group_count=CONV_WIDTH) + dw_bias
    h = layer_norm(h, ln_g, ln_b)
    h = jax.nn.silu(h)
    return jnp.einsum('bsc,cd->bsd', h, pw)


def sliding_window_attention(q, k, v, sinks):
    B, S, _ = q.shape
    nb = S // BLOCK
    qb = q.reshape(B, nb, BLOCK, N_KV_HEADS, Q_PER_KV, HEAD_DIM) * (HEAD_DIM ** -0.5)
    kr = jnp.pad(k.reshape(B, nb, BLOCK, N_KV_HEADS, HEAD_DIM), ((0, 0), (1, 0), (0, 0), (0, 0), (0, 0)))
    vr = jnp.pad(v.reshape(B, nb, BLOCK, N_KV_HEADS, HEAD_DIM), ((0, 0), (1, 0), (0, 0), (0, 0), (0, 0)))
    kb = jnp.concatenate([kr[:, :-1], kr[:, 1:]], axis=2)
    vb = jnp.concatenate([vr[:, :-1], vr[:, 1:]], axis=2)
    scores = jnp.einsum('bnqkgd,bnskd->bnkgqs', qb, kb).astype(jnp.float32)
    i = jnp.arange(BLOCK)[:, None]
    j = jnp.arange(2 * BLOCK)[None, :]
    dist = BLOCK + i - j
    in_band = (dist >= 0) & (dist < WINDOW)
    key_exists = ~(((jnp.arange(nb) == 0)[:, None, None]) & (j < BLOCK)[None])
    valid = in_band[None] & key_exists
    bias = -alibi_slopes().reshape(N_KV_HEADS, Q_PER_KV)[:, :, None, None] * dist.astype(jnp.float32)
    scores = jnp.where(valid[None, :, None, None], scores + bias[None, None], -1e30)
    sink = jnp.broadcast_to(sinks.astype(jnp.float32).reshape(N_KV_HEADS, Q_PER_KV)[None, None, :, :, None, None],
                            scores.shape[:-1] + (1,))
    p = jax.nn.softmax(jnp.concatenate([scores, sink], axis=-1), axis=-1)[..., :-1]
    out = jnp.einsum('bnkgqs,bnskd->bnqkgd', p.astype(v.dtype), vb)
    return out.reshape(B, S, ATTN_WIDTH)


def split_columns(proj):
    sizes = (POOL_WIDTH, POOL_WIDTH, CONV_WIDTH, CONV_WIDTH, CONV_WIDTH,
             ATTN_WIDTH, KV_WIDTH, KV_WIDTH, ATTN_WIDTH)
    idx = [int(s) for s in np.cumsum(sizes)[:-1]]
    return jnp.split(proj, idx, axis=-1)


def setup_inputs(seed: int = 0) -> dict:
    key = jax.random.key(seed)
    ks = jax.random.split(key, 14)
    f32 = jnp.float32
    gw = POOL_WIDTH // POOL_GROUPS
    x = jax.random.normal(ks[0], (BATCH, SEQ, D_MODEL), f32)
    ln_g = 1.0 + 0.05 * jax.random.normal(ks[1], (DEPTH, D_MODEL), f32)
    w_in = jax.random.normal(ks[2], (DEPTH, D_MODEL, D_IN), f32) * D_MODEL ** -0.5
    pool_w = jax.random.normal(ks[3], (DEPTH, POOL_GROUPS, gw, gw), f32) * gw ** -0.5
    pool_scale = 0.5 + 0.05 * jax.random.normal(ks[4], (DEPTH, POOL_WIDTH), f32)
    conv_dw = jax.random.normal(ks[5], (DEPTH, CONV_KERNEL, CONV_WIDTH), f32) * CONV_KERNEL ** -0.5
    conv_b = 0.02 * jax.random.normal(ks[6], (DEPTH, CONV_WIDTH), f32)
    conv_ln_g = 1.0 + 0.05 * jax.random.normal(ks[7], (DEPTH, CONV_WIDTH), f32)
    conv_ln_b = 0.02 * jax.random.normal(ks[8], (DEPTH, CONV_WIDTH), f32)
    conv_pw = jax.random.normal(ks[9], (DEPTH, CONV_WIDTH, CONV_WIDTH), f32) * CONV_WIDTH ** -0.5
    attn_sinks = 0.5 * jax.random.normal(ks[10], (DEPTH, N_Q_HEADS), f32)
    w_out = jax.random.normal(ks[11], (DEPTH, D_MIX, D_MODEL), f32) * D_MIX ** -0.5
    final_g = 1.0 + 0.05 * jax.random.normal(ks[12], (D_MODEL,), f32)
    return {"x": x, "ln_g": ln_g, "w_in": w_in, "pool_w": pool_w, "pool_scale": pool_scale,
            "conv_dw": conv_dw, "conv_b": conv_b, "conv_ln_g": conv_ln_g, "conv_ln_b": conv_ln_b,
            "conv_pw": conv_pw, "attn_sinks": attn_sinks, "w_out": w_out, "final_g": final_g}


def reference(x, ln_g, w_in, pool_w, pool_scale, conv_dw, conv_b, conv_ln_g, conv_ln_b,
              conv_pw, attn_sinks, w_out, final_g):
    for l in range(DEPTH):
        h = rms_norm(x, ln_g[l])
        proj = jnp.einsum('bsd,de->bse', h, w_in[l])
        (u_pool, g_pool, c_a, c_b, g_conv, q, k, v, g_attn) = split_columns(proj)
        y_pool = pool_mixer(u_pool, pool_w[l], pool_scale[l])
        y_conv = conv_module(c_a, c_b, conv_dw[l], conv_b[l], conv_ln_g[l], conv_ln_b[l], conv_pw[l])
        y_attn = sliding_window_attention(q, k, v, attn_sinks[l])
        y = jnp.concatenate([y_pool * jax.nn.silu(g_pool),
                             y_conv * jax.nn.silu(g_conv),
                             y_attn * jax.nn.silu(g_attn)], axis=-1)
        x = x + jnp.einsum('bse,ed->bsd', y, w_out[l])
    return rms_norm(x, final_g)
```

```python
import functools

import numpy as np
import jax
import jax.numpy as jnp
from jax import lax
from jax.experimental import pallas as pl
from jax.experimental.pallas import tpu as pltpu

D_MODEL = 1024
DEPTH = 2
POOL_WIDTH = 256
POOL_GROUPS = 4
POOL_WINDOWS = (2, 4, 8, 16)
CONV_WIDTH = 256
CONV_KERNEL = 31
HEAD_DIM = 64
N_Q_HEADS = 8
N_KV_HEADS = 2
Q_PER_KV = N_Q_HEADS // N_KV_HEADS
ATTN_WIDTH = N_Q_HEADS * HEAD_DIM
KV_WIDTH = N_KV_HEADS * HEAD_DIM
WINDOW = 128
BLOCK = 128
D_MIX = POOL_WIDTH + CONV_WIDTH + ATTN_WIDTH
D_IN = 2 * POOL_WIDTH + 3 * CONV_WIDTH + 2 * ATTN_WIDTH + 2 * KV_WIDTH
EPS = 1e-6

C_UPOOL = 0
C_GPOOL = C_UPOOL + POOL_WIDTH
C_CA = C_GPOOL + POOL_WIDTH
C_CB = C_CA + CONV_WIDTH
C_GCONV = C_CB + CONV_WIDTH
C_Q = C_GCONV + CONV_WIDTH
C_K = C_Q + ATTN_WIDTH
C_V = C_K + KV_WIDTH
C_GATTN = C_V + KV_WIDTH

SUBLANES = 8
LANES = 128
TS = 512
NB = TS // BLOCK
CHUNK = 64
GROUPS = CHUNK // SUBLANES
HALO_U = 16
HALO_H = 32
VMEM_LIMIT_BYTES = 56 * 1024 * 1024

_F32 = jnp.float32
_BF16 = jnp.bfloat16


def _silu(v):
    return v * jax.nn.sigmoid(v)


def _shift_down(groups, d, row):
    rolled = [pltpu.roll(g, d, axis=0) for g in groups]
    take_prev = row < d
    out = [rolled[0]]
    for i in range(1, len(groups)):
        out.append(jnp.where(take_prev, rolled[i - 1], rolled[i]))
    return out


def _layer_body(x_ref, lng_ref, win_ref, poolw_ref, pscale_ref, invcnt_ref, dw_ref, cb_ref, clg_ref, clb_ref,
                pw_ref, sinks_ref, bias_ref, wout_ref, fg_ref, o_ref,
                hn_ref, ubuf, hbuf, tbuf, zbuf, kbuf, vbuf, qbuf, abuf, mix_ref, *, final):
    j = pl.program_id(1)
    first_tile = j == 0

    @pl.when(first_tile)
    def _():
        ubuf[0:HALO_U, :] = jnp.zeros((HALO_U, POOL_WIDTH), _F32)
        hbuf[0:HALO_H, :] = jnp.zeros((HALO_H, CONV_WIDTH), _F32)
        kbuf[:, 0:BLOCK, :] = jnp.zeros((N_KV_HEADS, BLOCK, HEAD_DIM), _BF16)
        vbuf[:, 0:BLOCK, :] = jnp.zeros((N_KV_HEADS, BLOCK, HEAD_DIM), _BF16)

    @pl.when(jnp.logical_not(first_tile))
    def _():
        ubuf[0:HALO_U, :] = ubuf[TS:TS + HALO_U, :]
        hbuf[0:HALO_H, :] = hbuf[TS:TS + HALO_H, :]
        kbuf[:, 0:BLOCK, :] = kbuf[:, TS:TS + BLOCK, :]
        vbuf[:, 0:BLOCK, :] = vbuf[:, TS:TS + BLOCK, :]

    def norm_chunk(c, carry):
        r = pl.multiple_of(c * CHUNK, CHUNK)
        xc = x_ref[0, pl.ds(r, CHUNK), :]
        ms = jnp.mean(xc * xc, axis=-1, keepdims=True)
        y = xc * lax.rsqrt(ms + EPS) * lng_ref[...]
        hn_ref[pl.ds(r, CHUNK), :] = y.astype(_BF16)
        return carry

    lax.fori_loop(0, TS // CHUNK, norm_chunk, 0)

    def proj(c0, width):
        return jnp.dot(hn_ref[...], win_ref[:, c0:c0 + width], preferred_element_type=_F32)

    row = lax.broadcasted_iota(jnp.int32, (SUBLANES, LANES), 0)
    lane = lax.broadcasted_iota(jnp.int32, (SUBLANES, LANES), 1)
    low_half = lane < (LANES // 2)

    ubuf[HALO_U:HALO_U + TS, :] = proj(C_UPOOL, POOL_WIDTH)
    tbuf[...] = _silu(proj(C_GPOOL, POOL_WIDTH))

    inv_w = [jnp.where(low_half, 1.0 / POOL_WINDOWS[0], 1.0 / POOL_WINDOWS[1]).astype(_F32),
             jnp.where(low_half, 1.0 / POOL_WINDOWS[2], 1.0 / POOL_WINDOWS[3]).astype(_F32)]

    def pool_chunk(c, carry):
        r = pl.multiple_of(c * CHUNK, CHUNK)
        seq_start = jnp.logical_and(first_tile, c == 0)
        n_in = GROUPS + HALO_U // SUBLANES
        for col in range(2):
            lanes = slice(col * LANES, (col + 1) * LANES)
            u = [ubuf[pl.ds(r + SUBLANES * i, SUBLANES), lanes] for i in range(n_in)]
            s2 = [a + b for a, b in zip(u, _shift_down(u, 1, row))]
            s4 = [a + b for a, b in zip(s2, _shift_down(s2, 2, row))]
            if col == 0:
                wide, narrow = s4, s2
            else:
                s8 = [a + b for a, b in zip(s4, _shift_down(s4, 4, row))]
                s16 = [s8[0]] + [s8[i] + s8[i - 1] for i in range(1, n_in)]
                wide, narrow = s16, s8
            pieces = []
            for g in range(GROUPS):
                i = g + HALO_U // SUBLANES
                inv = inv_w[col]
                if g < HALO_U // SUBLANES:
                    tab = invcnt_ref[SUBLANES * g:SUBLANES * (g + 1), lanes]
                    inv = jnp.where(seq_start, tab, inv)
                pooled = jnp.where(low_half, narrow[i], wide[i]) * inv
                pieces.append(pooled - u[i])
            zbuf[pl.ds(r, CHUNK), lanes] = jnp.concatenate(pieces, axis=0).astype(_BF16)
        return carry

    lax.fori_loop(0, TS // CHUNK, pool_chunk, 0)
    y_pool = jnp.dot(zbuf[...], poolw_ref[...], preferred_element_type=_F32) * pscale_ref[...]
    mix_ref[:, 0:POOL_WIDTH] = (y_pool * tbuf[...]).astype(_BF16)

    hbuf[HALO_H:HALO_H + TS, :] = proj(C_CA, CONV_WIDTH)
    hbuf[HALO_H:HALO_H + TS, :] = hbuf[HALO_H:HALO_H + TS, :] * jax.nn.sigmoid(proj(C_CB, CONV_WIDTH))
    tbuf[...] = _silu(proj(C_GCONV, CONV_WIDTH))

    taps_by_b = [[] for _ in range(SUBLANES)]
    for k in range(CONV_KERNEL):
        a, b = divmod(k + HALO_H - (CONV_KERNEL - 1), SUBLANES)
        taps_by_b[b].append((a, k))
    n_h = GROUPS + HALO_H // SUBLANES

    def conv_chunk(c, carry):
        r = pl.multiple_of(c * CHUNK, CHUNK)
        outs = []
        for col in range(2):
            lanes = slice(col * LANES, (col + 1) * LANES)
            h = [hbuf[pl.ds(r + SUBLANES * m, SUBLANES), lanes] for m in range(n_h)]
            acc = None
            for b in range(SUBLANES):
                n_p = GROUPS if b == 0 else GROUPS + 1
                part = []
                for m in range(n_p):
                    p = None
                    for a, k in taps_by_b[b]:
                        term = h[m + a] * dw_ref[k:k + 1, lanes]
                        p = term if p is None else p + term
                    part.append(p)
                if b == 0:
                    acc = part
                else:
                    rolled = [pltpu.roll(p, SUBLANES - b, axis=0) for p in part]
                    keep = row < (SUBLANES - b)
                    acc = [acc[g] + jnp.where(keep, rolled[g], rolled[g + 1]) for g in range(GROUPS)]
            outs.append(jnp.concatenate(acc, axis=0))
        conv = jnp.concatenate(outs, axis=1) + cb_ref[...]
        mu = jnp.mean(conv, axis=-1, keepdims=True)
        cen = conv - mu
        var = jnp.mean(cen * cen, axis=-1, keepdims=True)
        z = cen * lax.rsqrt(var + EPS) * clg_ref[...] + clb_ref[...]
        zbuf[pl.ds(r, CHUNK), :] = _silu(z).astype(_BF16)
        return carry

    lax.fori_loop(0, TS // CHUNK, conv_chunk, 0)
    y_conv = jnp.dot(zbuf[...], pw_ref[...], preferred_element_type=_F32)
    mix_ref[:, POOL_WIDTH:POOL_WIDTH + CONV_WIDTH] = (y_conv * tbuf[...]).astype(_BF16)

    kv = proj(C_K, 2 * KV_WIDTH)
    for kvh in range(N_KV_HEADS):
        kbuf[kvh, BLOCK:BLOCK + TS, :] = kv[:, kvh * HEAD_DIM:(kvh + 1) * HEAD_DIM].astype(_BF16)
        vbuf[kvh, BLOCK:BLOCK + TS, :] = kv[:, KV_WIDTH + kvh * HEAD_DIM:KV_WIDTH + (kvh + 1) * HEAD_DIM].astype(_BF16)
    for half in range(2):
        qh = proj(C_Q + half * 256, 256) * (HEAD_DIM ** -0.5)
        for i in range(4):
            qbuf[half * 4 + i, :, :] = qh[:, i * HEAD_DIM:(i + 1) * HEAD_DIM].astype(_BF16)

    def attn_block(n, carry):
        r = pl.multiple_of(n * BLOCK, BLOCK)
        seq_start = jnp.logical_and(first_tile, n == 0).astype(jnp.int32)
        for kvh in range(N_KV_HEADS):
            k2 = kbuf[kvh, pl.ds(r, 2 * BLOCK), :]
            v2 = vbuf[kvh, pl.ds(r, 2 * BLOCK), :]
            q4 = jnp.concatenate([qbuf[kvh * Q_PER_KV + g, pl.ds(r, BLOCK), :] for g in range(Q_PER_KV)], axis=0)
            s = lax.dot_general(q4, k2, (((1,), (1,)), ((), ())), preferred_element_type=_F32)
            s = s + bias_ref[seq_start, kvh]
            probs, inv_den = [], []
            for g in range(Q_PER_KV):
                sg = s[g * BLOCK:(g + 1) * BLOCK]
                sink = sinks_ref[kvh * Q_PER_KV + g]
                m = jnp.maximum(jnp.max(sg, axis=-1, keepdims=True), sink)
                p = jnp.exp(sg - m)
                den = jnp.sum(p, axis=-1, keepdims=True) + jnp.exp(sink - m)
                probs.append(p.astype(_BF16))
                inv_den.append(1.0 / den)
            o = jnp.dot(jnp.concatenate(probs, axis=0), v2, preferred_element_type=_F32)
            for g in range(Q_PER_KV):
                hq = kvh * Q_PER_KV + g
                abuf[pl.ds(r, BLOCK), hq * HEAD_DIM:(hq + 1) * HEAD_DIM] = o[g * BLOCK:(g + 1) * BLOCK] * inv_den[g]
        return carry

    lax.fori_loop(0, NB, attn_block, 0)
    for half in range(2):
        cols = slice(half * 256, (half + 1) * 256)
        gate = _silu(proj(C_GATTN + half * 256, 256))
        mix_ref[:, POOL_WIDTH + CONV_WIDTH + half * 256:POOL_WIDTH + CONV_WIDTH + (half + 1) * 256] = (
            abuf[:, cols] * gate).astype(_BF16)

    for q in range(D_MODEL // 256):
        cols = slice(q * 256, (q + 1) * 256)
        o_ref[0, :, cols] = x_ref[0, :, cols] + jnp.dot(mix_ref[...], wout_ref[:, cols], preferred_element_type=_F32)

    if final:
        def final_chunk(c, carry):
            r = pl.multiple_of(c * CHUNK, CHUNK)
            xc = o_ref[0, pl.ds(r, CHUNK), :]
            ms = jnp.mean(xc * xc, axis=-1, keepdims=True)
            o_ref[0, pl.ds(r, CHUNK), :] = xc * lax.rsqrt(ms + EPS) * fg_ref[...]
            return carry

        lax.fori_loop(0, TS // CHUNK, final_chunk, 0)


def _bias_table():
    i = np.arange(BLOCK)[:, None]
    jj = np.arange(2 * BLOCK)[None, :]
    dist = BLOCK + i - jj
    in_band = (dist >= 0) & (dist < WINDOW)
    slopes = np.asarray([2.0 ** (-8.0 * (h + 1) / N_Q_HEADS) for h in range(N_Q_HEADS)], dtype=np.float32)
    tab = np.empty((2, N_KV_HEADS, Q_PER_KV * BLOCK, 2 * BLOCK), np.float32)
    for start in range(2):
        valid = in_band & ~((jj < BLOCK) & (start == 1))
        for h in range(N_Q_HEADS):
            bias = -slopes[h] * dist.astype(np.float32)
            kvh, g = divmod(h, Q_PER_KV)
            tab[start, kvh, g * BLOCK:(g + 1) * BLOCK] = np.where(valid, bias, np.float32(-1e30))
    return tab


def _inv_count_table():
    t = np.arange(HALO_U, dtype=np.float32)[:, None]
    gw = POOL_WIDTH // POOL_GROUPS
    wnd = np.repeat(np.asarray(POOL_WINDOWS, np.float32), gw)[None, :]
    return (1.0 / np.minimum(t + 1.0, wnd)).astype(np.float32)


def _layer_call(batch, seq, final):
    full = lambda shape: pl.BlockSpec(shape, lambda b, j: (0,) * len(shape))
    in_specs = [
        pl.BlockSpec((1, TS, D_MODEL), lambda b, j: (b, j, 0)),
        full((1, D_MODEL)),
        full((D_MODEL, D_IN)),
        full((POOL_WIDTH, POOL_WIDTH)),
        full((1, POOL_WIDTH)),
        full((HALO_U, POOL_WIDTH)),
        full((CONV_KERNEL, CONV_WIDTH)),
        full((1, CONV_WIDTH)),
        full((1, CONV_WIDTH)),
        full((1, CONV_WIDTH)),
        full((CONV_WIDTH, CONV_WIDTH)),
        pl.BlockSpec(memory_space=pltpu.SMEM),
        full((2, N_KV_HEADS, Q_PER_KV * BLOCK, 2 * BLOCK)),
        full((D_MIX, D_MODEL)),
        full((1, D_MODEL)),
    ]
    scratch = [
        pltpu.VMEM((TS, D_MODEL), _BF16),
        pltpu.VMEM((TS + HALO_U, POOL_WIDTH), _F32),
        pltpu.VMEM((TS + HALO_H, CONV_WIDTH), _F32),
        pltpu.VMEM((TS, 256), _F32),
        pltpu.VMEM((TS, 256), _BF16),
        pltpu.VMEM((N_KV_HEADS, TS + BLOCK, HEAD_DIM), _BF16),
        pltpu.VMEM((N_KV_HEADS, TS + BLOCK, HEAD_DIM), _BF16),
        pltpu.VMEM((N_Q_HEADS, TS, HEAD_DIM), _BF16),
        pltpu.VMEM((TS, ATTN_WIDTH), _F32),
        pltpu.VMEM((TS, D_MIX), _BF16),
    ]
    return pl.pallas_call(
        functools.partial(_layer_body, final=final),
        out_shape=jax.ShapeDtypeStruct((batch, seq, D_MODEL), _F32),
        grid=(batch, seq // TS),
        in_specs=in_specs,
        out_specs=pl.BlockSpec((1, TS, D_MODEL), lambda b, j: (b, j, 0)),
        scratch_shapes=scratch,
        compiler_params=pltpu.CompilerParams(
            dimension_semantics=("arbitrary", "arbitrary"),
            vmem_limit_bytes=VMEM_LIMIT_BYTES,
        ),
        name="hybrid_layer_final" if final else "hybrid_layer",
    )


def kernel(x, ln_g, w_in, pool_w, pool_scale, conv_dw, conv_b, conv_ln_g, conv_ln_b, conv_pw, attn_sinks, w_out, final_g):
    batch, seq, d_model = x.shape
    assert d_model == D_MODEL and seq % TS == 0 and TS % BLOCK == 0
    bias = jnp.asarray(_bias_table())
    inv_cnt = jnp.asarray(_inv_count_table())
    row = lambda v: v.reshape(1, -1)
    for l in range(DEPTH):
        poolw_bd = jax.scipy.linalg.block_diag(*[pool_w[l, g] for g in range(POOL_GROUPS)])
        x = _layer_call(batch, seq, final=(l == DEPTH - 1))(
            x, row(ln_g[l]), w_in[l].astype(_BF16), poolw_bd.astype(_BF16), row(pool_scale[l]), inv_cnt,
            conv_dw[l], row(conv_b[l]), row(conv_ln_g[l]), row(conv_ln_b[l]), conv_pw[l].astype(_BF16),
            attn_sinks[l], bias, w_out[l].astype(_BF16), row(final_g))
    return x
```

```python
import functools

import numpy as np
import jax
import jax.numpy as jnp
from jax import lax
from jax.experimental import pallas as pl
from jax.experimental.pallas import tpu as pltpu

D_MODEL = 1024
DEPTH = 2
POOL_WIDTH = 256
POOL_GROUPS = 4
POOL_WINDOWS = (2, 4, 8, 16)
CONV_WIDTH = 256
CONV_KERNEL = 31
HEAD_DIM = 64
N_Q_HEADS = 8
N_KV_HEADS = 2
Q_PER_KV = N_Q_HEADS // N_KV_HEADS
ATTN_WIDTH = N_Q_HEADS * HEAD_DIM
KV_WIDTH = N_KV_HEADS * HEAD_DIM
WINDOW = 128
BLOCK = 128
D_MIX = POOL_WIDTH + CONV_WIDTH + ATTN_WIDTH
D_IN = 2 * POOL_WIDTH + 3 * CONV_WIDTH + 2 * ATTN_WIDTH + 2 * KV_WIDTH
EPS = 1e-6

C_UPOOL = 0
C_GPOOL = C_UPOOL + POOL_WIDTH
C_CA = C_GPOOL + POOL_WIDTH
C_CB = C_CA + CONV_WIDTH
C_GCONV = C_CB + CONV_WIDTH
C_Q = C_GCONV + CONV_WIDTH
C_K = C_Q + ATTN_WIDTH
C_V = C_K + KV_WIDTH
C_GATTN = C_V + KV_WIDTH

SUBLANES = 8
LANES = 128
TS = 512
NB = TS // BLOCK
CHUNK = 64
GROUPS = CHUNK // SUBLANES
HALO_U = 16
HALO_H = 32
VMEM_LIMIT_BYTES = 56 * 1024 * 1024

_F32 = jnp.float32
_BF16 = jnp.bfloat16


def _silu(v):
    return v * jax.nn.sigmoid(v)


def _unrolled(n, body):
    for c in range(n):
        body(c)


def _shift_down(groups, d, row):
    rolled = [pltpu.roll(g, d, axis=0) for g in groups]
    take_prev = row < d
    out = [rolled[0]]
    for i in range(1, len(groups)):
        out.append(jnp.where(take_prev, rolled[i - 1], rolled[i]))
    return out


def _layer_body(x_ref, lng_ref, win_ref, poolw_ref, pscale_ref, invcnt_ref, dw_ref, cb_ref, clg_ref, clb_ref,
                pw_ref, sinks_ref, bias_ref, wout_ref, fg_ref, o_ref,
                hn_ref, ubuf, hbuf, tbuf, zbuf, kbuf, vbuf, qbuf, abuf, mix_ref, *, final):
    j = pl.program_id(1)
    first_tile = j == 0

    @pl.when(first_tile)
    def _():
        ubuf[0:HALO_U, :] = jnp.zeros((HALO_U, POOL_WIDTH), _F32)
        hbuf[0:HALO_H, :] = jnp.zeros((HALO_H, CONV_WIDTH), _F32)
        kbuf[:, 0:BLOCK, :] = jnp.zeros((N_KV_HEADS, BLOCK, HEAD_DIM), _BF16)
        vbuf[:, 0:BLOCK, :] = jnp.zeros((N_KV_HEADS, BLOCK, HEAD_DIM), _BF16)

    @pl.when(jnp.logical_not(first_tile))
    def _():
        ubuf[0:HALO_U, :] = ubuf[TS:TS + HALO_U, :]
        hbuf[0:HALO_H, :] = hbuf[TS:TS + HALO_H, :]
        kbuf[:, 0:BLOCK, :] = kbuf[:, TS:TS + BLOCK, :]
        vbuf[:, 0:BLOCK, :] = vbuf[:, TS:TS + BLOCK, :]

    def norm_chunk(c):
        r = c * CHUNK
        xc = x_ref[0, pl.ds(r, CHUNK), :]
        ms = jnp.mean(xc * xc, axis=-1, keepdims=True)
        y = xc * lax.rsqrt(ms + EPS) * lng_ref[...]
        hn_ref[pl.ds(r, CHUNK), :] = y.astype(_BF16)

    _unrolled(TS // CHUNK, norm_chunk)

    def proj(c0, width):
        return jnp.dot(hn_ref[...], win_ref[:, c0:c0 + width], preferred_element_type=_F32)

    row = lax.broadcasted_iota(jnp.int32, (SUBLANES, LANES), 0)
    lane = lax.broadcasted_iota(jnp.int32, (SUBLANES, LANES), 1)
    low_half = lane < (LANES // 2)

    ubuf[HALO_U:HALO_U + TS, :] = proj(C_UPOOL, POOL_WIDTH)
    tbuf[...] = _silu(proj(C_GPOOL, POOL_WIDTH))

    inv_w = [jnp.where(low_half, 1.0 / POOL_WINDOWS[0], 1.0 / POOL_WINDOWS[1]).astype(_F32),
             jnp.where(low_half, 1.0 / POOL_WINDOWS[2], 1.0 / POOL_WINDOWS[3]).astype(_F32)]

    def pool_chunk(c):
        r = c * CHUNK
        n_in = GROUPS + HALO_U // SUBLANES
        for col in range(2):
            lanes = slice(col * LANES, (col + 1) * LANES)
            u = [ubuf[pl.ds(r + SUBLANES * i, SUBLANES), lanes] for i in range(n_in)]
            s2 = [a + b for a, b in zip(u, _shift_down(u, 1, row))]
            s4 = [a + b for a, b in zip(s2, _shift_down(s2, 2, row))]
            if col == 0:
                wide, narrow = s4, s2
            else:
                s8 = [a + b for a, b in zip(s4, _shift_down(s4, 4, row))]
                s16 = [s8[0]] + [s8[i] + s8[i - 1] for i in range(1, n_in)]
                wide, narrow = s16, s8
            pieces = []
            for g in range(GROUPS):
                i = g + HALO_U // SUBLANES
                inv = inv_w[col]
                if c == 0 and g < HALO_U // SUBLANES:
                    tab = invcnt_ref[SUBLANES * g:SUBLANES * (g + 1), lanes]
                    inv = jnp.where(first_tile, tab, inv)
                pooled = jnp.where(low_half, narrow[i], wide[i]) * inv
                pieces.append(pooled - u[i])
            zbuf[pl.ds(r, CHUNK), lanes] = jnp.concatenate(pieces, axis=0).astype(_BF16)

    _unrolled(TS // CHUNK, pool_chunk)
    y_pool = jnp.dot(zbuf[...], poolw_ref[...], preferred_element_type=_F32) * pscale_ref[...]
    mix_ref[:, 0:POOL_WIDTH] = (y_pool * tbuf[...]).astype(_BF16)

    hbuf[HALO_H:HALO_H + TS, :] = proj(C_CA, CONV_WIDTH)
    hbuf[HALO_H:HALO_H + TS, :] = hbuf[HALO_H:HALO_H + TS, :] * jax.nn.sigmoid(proj(C_CB, CONV_WIDTH))
    tbuf[...] = _silu(proj(C_GCONV, CONV_WIDTH))

    taps_by_b = [[] for _ in range(SUBLANES)]
    for k in range(CONV_KERNEL):
        a, b = divmod(k + HALO_H - (CONV_KERNEL - 1), SUBLANES)
        taps_by_b[b].append((a, k))
    n_h = GROUPS + HALO_H // SUBLANES

    def conv_chunk(c):
        r = c * CHUNK
        outs = []
        for col in range(2):
            lanes = slice(col * LANES, (col + 1) * LANES)
            h = [hbuf[pl.ds(r + SUBLANES * m, SUBLANES), lanes] for m in range(n_h)]
            acc = None
            for b in range(SUBLANES):
                n_p = GROUPS if b == 0 else GROUPS + 1
                part = []
                for m in range(n_p):
                    p = None
                    for a, k in taps_by_b[b]:
                        term = h[m + a] * dw_ref[k:k + 1, lanes]
                        p = term if p is None else p + term
                    part.append(p)
                if b == 0:
                    acc = part
                else:
                    rolled = [pltpu.roll(p, SUBLANES - b, axis=0) for p in part]
                    keep = row < (SUBLANES - b)
                    acc = [acc[g] + jnp.where(keep, rolled[g], rolled[g + 1]) for g in range(GROUPS)]
            outs.append(jnp.concatenate(acc, axis=0))
        conv = jnp.concatenate(outs, axis=1) + cb_ref[...]
        mu = jnp.mean(conv, axis=-1, keepdims=True)
        cen = conv - mu
        var = jnp.mean(cen * cen, axis=-1, keepdims=True)
        z = cen * lax.rsqrt(var + EPS) * clg_ref[...] + clb_ref[...]
        zbuf[pl.ds(r, CHUNK), :] = _silu(z).astype(_BF16)

    _unrolled(TS // CHUNK, conv_chunk)
    y_conv = jnp.dot(zbuf[...], pw_ref[...], preferred_element_type=_F32)
    mix_ref[:, POOL_WIDTH:POOL_WIDTH + CONV_WIDTH] = (y_conv * tbuf[...]).astype(_BF16)

    kv = proj(C_K, 2 * KV_WIDTH)
    for kvh in range(N_KV_HEADS):
        kbuf[kvh, BLOCK:BLOCK + TS, :] = kv[:, kvh * HEAD_DIM:(kvh + 1) * HEAD_DIM].astype(_BF16)
        vbuf[kvh, BLOCK:BLOCK + TS, :] = kv[:, KV_WIDTH + kvh * HEAD_DIM:KV_WIDTH + (kvh + 1) * HEAD_DIM].astype(_BF16)
    for half in range(2):
        qh = proj(C_Q + half * 256, 256) * (HEAD_DIM ** -0.5)
        for i in range(4):
            qbuf[half * 4 + i, :, :] = qh[:, i * HEAD_DIM:(i + 1) * HEAD_DIM].astype(_BF16)

    def attn_block(n):
        r = n * BLOCK
        seq_start = first_tile.astype(jnp.int32) if n == 0 else 0
        for kvh in range(N_KV_HEADS):
            k2 = kbuf[kvh, pl.ds(r, 2 * BLOCK), :]
            v2 = vbuf[kvh, pl.ds(r, 2 * BLOCK), :]
            q4 = jnp.concatenate([qbuf[kvh * Q_PER_KV + g, pl.ds(r, BLOCK), :] for g in range(Q_PER_KV)], axis=0)
            s = lax.dot_general(q4, k2, (((1,), (1,)), ((), ())), preferred_element_type=_F32)
            s = s + bias_ref[seq_start, kvh]
            probs, inv_den = [], []
            for g in range(Q_PER_KV):
                sg = s[g * BLOCK:(g + 1) * BLOCK]
                sink = sinks_ref[kvh * Q_PER_KV + g]
                m = jnp.maximum(jnp.max(sg, axis=-1, keepdims=True), sink)
                p = jnp.exp(sg - m)
                den = jnp.sum(p, axis=-1, keepdims=True) + jnp.exp(sink - m)
                probs.append(p.astype(_BF16))
                inv_den.append(1.0 / den)
            o = jnp.dot(jnp.concatenate(probs, axis=0), v2, preferred_element_type=_F32)
            for g in range(Q_PER_KV):
                hq = kvh * Q_PER_KV + g
                abuf[pl.ds(r, BLOCK), hq * HEAD_DIM:(hq + 1) * HEAD_DIM] = o[g * BLOCK:(g + 1) * BLOCK] * inv_den[g]

    _unrolled(NB, attn_block)
    for half in range(2):
        cols = slice(half * 256, (half + 1) * 256)
        gate = _silu(proj(C_GATTN + half * 256, 256))
        mix_ref[:, POOL_WIDTH + CONV_WIDTH + half * 256:POOL_WIDTH + CONV_WIDTH + (half + 1) * 256] = (
            abuf[:, cols] * gate).astype(_BF16)

    for q in range(D_MODEL // 256):
        cols = slice(q * 256, (q + 1) * 256)
        o_ref[0, :, cols] = x_ref[0, :, cols] + jnp.dot(mix_ref[...], wout_ref[:, cols], preferred_element_type=_F32)

    if final:
        def final_chunk(c):
            r = c * CHUNK
            xc = o_ref[0, pl.ds(r, CHUNK), :]
            ms = jnp.mean(xc * xc, axis=-1, keepdims=True)
            o_ref[0, pl.ds(r, CHUNK), :] = xc * lax.rsqrt(ms + EPS) * fg_ref[...]

        _unrolled(TS // CHUNK, final_chunk)


def _bias_table():
    i = np.arange(BLOCK)[:, None]
    jj = np.arange(2 * BLOCK)[None, :]
    dist = BLOCK + i - jj
    in_band = (dist >= 0) & (dist < WINDOW)
    slopes = np.asarray([2.0 ** (-8.0 * (h + 1) / N_Q_HEADS) for h in range(N_Q_HEADS)], dtype=np.float32)
    tab = np.empty((2, N_KV_HEADS, Q_PER_KV * BLOCK, 2 * BLOCK), np.float32)
    for start in range(2):
        valid = in_band & ~((jj < BLOCK) & (start == 1))
        for h in range(N_Q_HEADS):
            bias = -slopes[h] * dist.astype(np.float32)
            kvh, g = divmod(h, Q_PER_KV)
            tab[start, kvh, g * BLOCK:(g + 1) * BLOCK] = np.where(valid, bias, np.float32(-1e30))
    return tab


def _inv_count_table():
    t = np.arange(HALO_U, dtype=np.float32)[:, None]
    gw = POOL_WIDTH // POOL_GROUPS
    wnd = np.repeat(np.asarray(POOL_WINDOWS, np.float32), gw)[None, :]
    return (1.0 / np.minimum(t + 1.0, wnd)).astype(np.float32)


def _layer_call(batch, seq, final):
    full = lambda shape: pl.BlockSpec(shape, lambda b, j: (0,) * len(shape))
    in_specs = [
        pl.BlockSpec((1, TS, D_MODEL), lambda b, j: (b, j, 0)),
        full((1, D_MODEL)),
        full((D_MODEL, D_IN)),
        full((POOL_WIDTH, POOL_WIDTH)),
        full((1, POOL_WIDTH)),
        full((HALO_U, POOL_WIDTH)),
        full((CONV_KERNEL, CONV_WIDTH)),
        full((1, CONV_WIDTH)),
        full((1, CONV_WIDTH)),
        full((1, CONV_WIDTH)),
        full((CONV_WIDTH, CONV_WIDTH)),
        pl.BlockSpec(memory_space=pltpu.SMEM),
        full((2, N_KV_HEADS, Q_PER_KV * BLOCK, 2 * BLOCK)),
        full((D_MIX, D_MODEL)),
        full((1, D_MODEL)),
    ]
    scratch = [
        pltpu.VMEM((TS, D_MODEL), _BF16),
        pltpu.VMEM((TS + HALO_U, POOL_WIDTH), _F32),
        pltpu.VMEM((TS + HALO_H, CONV_WIDTH), _F32),
        pltpu.VMEM((TS, 256), _F32),
        pltpu.VMEM((TS, 256), _BF16),
        pltpu.VMEM((N_KV_HEADS, TS + BLOCK, HEAD_DIM), _BF16),
        pltpu.VMEM((N_KV_HEADS, TS + BLOCK, HEAD_DIM), _BF16),
        pltpu.VMEM((N_Q_HEADS, TS, HEAD_DIM), _BF16),
        pltpu.VMEM((TS, ATTN_WIDTH), _F32),
        pltpu.VMEM((TS, D_MIX), _BF16),
    ]
    return pl.pallas_call(
        functools.partial(_layer_body, final=final),
        out_shape=jax.ShapeDtypeStruct((batch, seq, D_MODEL), _F32),
        grid=(batch, seq // TS),
        in_specs=in_specs,
        out_specs=pl.BlockSpec((1, TS, D_MODEL), lambda b, j: (b, j, 0)),
        scratch_shapes=scratch,
        compiler_params=pltpu.CompilerParams(
            dimension_semantics=("arbitrary", "arbitrary"),
            vmem_limit_bytes=VMEM_LIMIT_BYTES,
        ),
        name="hybrid_layer_final" if final else "hybrid_layer",
    )


def kernel(x, ln_g, w_in, pool_w, pool_scale, conv_dw, conv_b, conv_ln_g, conv_ln_b, conv_pw, attn_sinks, w_out, final_g):
    batch, seq, d_model = x.shape
    assert d_model == D_MODEL and seq % TS == 0 and TS % BLOCK == 0
    bias = jnp.asarray(_bias_table())
    inv_cnt = jnp.asarray(_inv_count_table())
    row = lambda v: v.reshape(1, -1)
    for l in range(DEPTH):
        poolw_bd = jax.scipy.linalg.block_diag(*[pool_w[l, g] for g in range(POOL_GROUPS)])
        x = _layer_call(batch, seq, final=(l == DEPTH - 1))(
            x, row(ln_g[l]), w_in[l].astype(_BF16), poolw_bd.astype(_BF16), row(pool_scale[l]), inv_cnt,
            conv_dw[l], row(conv_b[l]), row(conv_ln_g[l]), row(conv_ln_b[l]), conv_pw[l].astype(_BF16),
            attn_sinks[l], bias, w_out[l].astype(_BF16), row(final_g))
    return x
```

```python
import functools

import numpy as np
import jax
import jax.numpy as jnp
from jax import lax
from jax.experimental import pallas as pl
from jax.experimental.pallas import tpu as pltpu

D_MODEL = 1024
DEPTH = 2
POOL_WIDTH = 256
POOL_GROUPS = 4
POOL_WINDOWS = (2, 4, 8, 16)
CONV_WIDTH = 256
CONV_KERNEL = 31
HEAD_DIM = 64
N_Q_HEADS = 8
N_KV_HEADS = 2
Q_PER_KV = N_Q_HEADS // N_KV_HEADS
ATTN_WIDTH = N_Q_HEADS * HEAD_DIM
KV_WIDTH = N_KV_HEADS * HEAD_DIM
WINDOW = 128
BLOCK = 128
D_MIX = POOL_WIDTH + CONV_WIDTH + ATTN_WIDTH
D_IN = 2 * POOL_WIDTH + 3 * CONV_WIDTH + 2 * ATTN_WIDTH + 2 * KV_WIDTH
EPS = 1e-6

C_UPOOL = 0
C_GPOOL = C_UPOOL + POOL_WIDTH
C_CA = C_GPOOL + POOL_WIDTH
C_CB = C_CA + CONV_WIDTH
C_GCONV = C_CB + CONV_WIDTH
C_Q = C_GCONV + CONV_WIDTH
C_K = C_Q + ATTN_WIDTH
C_V = C_K + KV_WIDTH
C_GATTN = C_V + KV_WIDTH

SUBLANES = 8
LANES = 128
TS = 512
NB = TS // BLOCK
CHUNK = 64
GROUPS = CHUNK // SUBLANES
HALO_U = 16
HALO_H = 32
VMEM_LIMIT_BYTES = 56 * 1024 * 1024

_F32 = jnp.float32
_BF16 = jnp.bfloat16


def _silu(v):
    return v * jax.nn.sigmoid(v)


def _unrolled(n, body):
    for c in range(n):
        body(c)


def _shift_down(groups, d, row):
    rolled = [pltpu.roll(g, d, axis=0) for g in groups]
    take_prev = row < d
    out = [rolled[0]]
    for i in range(1, len(groups)):
        out.append(jnp.where(take_prev, rolled[i - 1], rolled[i]))
    return out


def _layer_body(x_ref, lng_ref, win_ref, poolw_ref, pscale_ref, invcnt_ref, dw_ref, cb_ref, clg_ref, clb_ref,
                pw_ref, sinks_ref, bias_ref, wout_ref, fg_ref, o_ref,
                hn_ref, ubuf, hbuf, gp_buf, gc_buf, ga_buf, zp_buf, zc_buf, kbuf, ve_buf, vo_buf, qbuf, mix_ref,
                *, layer, final):
    j = pl.program_id(1)
    first_tile = j == 0

    row = lax.broadcasted_iota(jnp.int32, (SUBLANES, LANES), 0)
    lane = lax.broadcasted_iota(jnp.int32, (SUBLANES, LANES), 1)
    low_half = lane < HEAD_DIM

    @pl.when(first_tile)
    def _():
        ubuf[0:HALO_U, :] = jnp.zeros((HALO_U, POOL_WIDTH), _F32)
        hbuf[0:HALO_H, :] = jnp.zeros((HALO_H, CONV_WIDTH), _F32)
        kbuf[:, 0:BLOCK, :] = jnp.zeros((N_KV_HEADS, BLOCK, HEAD_DIM), _BF16)
        zeros = jnp.zeros((TS + BLOCK, LANES), _BF16)
        lane_t = lax.broadcasted_iota(jnp.int32, (TS + BLOCK, LANES), 1)
        ones_lo = jnp.where(lane_t < HEAD_DIM, 1.0, 0.0).astype(_BF16)
        ones_hi = jnp.where(lane_t < HEAD_DIM, 0.0, 1.0).astype(_BF16)
        for kvh in range(N_KV_HEADS):
            ve_buf[kvh, :, 0:LANES] = zeros
            vo_buf[kvh, :, 0:LANES] = zeros
            ve_buf[kvh, :, LANES:2 * LANES] = ones_lo
            vo_buf[kvh, :, LANES:2 * LANES] = ones_hi

    @pl.when(jnp.logical_not(first_tile))
    def _():
        ubuf[0:HALO_U, :] = ubuf[TS:TS + HALO_U, :]
        hbuf[0:HALO_H, :] = hbuf[TS:TS + HALO_H, :]
        kbuf[:, 0:BLOCK, :] = kbuf[:, TS:TS + BLOCK, :]
        ve_buf[:, 0:BLOCK, 0:LANES] = ve_buf[:, TS:TS + BLOCK, 0:LANES]
        vo_buf[:, 0:BLOCK, 0:LANES] = vo_buf[:, TS:TS + BLOCK, 0:LANES]

    def norm_chunk(c):
        r = c * CHUNK
        xc = x_ref[0, pl.ds(r, CHUNK), :]
        ms = jnp.mean(xc * xc, axis=-1, keepdims=True)
        y = xc * lax.rsqrt(ms + EPS) * lng_ref[0]
        hn_ref[pl.ds(r, CHUNK), :] = y.astype(_BF16)

    _unrolled(TS // CHUNK, norm_chunk)

    def proj(c0, width, r0=0, rows=TS):
        return jnp.dot(hn_ref[pl.ds(r0, rows), :], win_ref[0, :, c0:c0 + width], preferred_element_type=_F32)

    for r0 in range(0, TS, TS // 2):
        rows = pl.ds(HALO_H + r0, TS // 2)
        hbuf[rows, :] = proj(C_CA, CONV_WIDTH, r0, TS // 2)
        hbuf[rows, :] = hbuf[rows, :] * jax.nn.sigmoid(proj(C_CB, CONV_WIDTH, r0, TS // 2))

    def proj_upool():
        ubuf[HALO_U:HALO_U + TS, :] = proj(C_UPOOL, POOL_WIDTH)

    def proj_gpool():
        gp_buf[...] = _silu(proj(C_GPOOL, POOL_WIDTH))

    def proj_gconv():
        gc_buf[...] = _silu(proj(C_GCONV, CONV_WIDTH))

    def proj_kv():
        kv = proj(C_K, 2 * KV_WIDTH)
        keys = kv[:, 0:KV_WIDTH]
        vals = kv[:, KV_WIDTH:2 * KV_WIDTH]
        vals_swapped = pltpu.roll(vals, HEAD_DIM, axis=1)
        lo = lax.broadcasted_iota(jnp.int32, (TS, LANES), 1) < HEAD_DIM
        for kvh in range(N_KV_HEADS):
            kbuf[kvh, BLOCK:BLOCK + TS, :] = keys[:, kvh * HEAD_DIM:(kvh + 1) * HEAD_DIM].astype(_BF16)
            own_lo, own_hi = (vals, vals_swapped) if kvh == 0 else (vals_swapped, vals)
            ve_buf[kvh, BLOCK:BLOCK + TS, 0:LANES] = jnp.where(lo, own_lo, 0.0).astype(_BF16)
            vo_buf[kvh, BLOCK:BLOCK + TS, 0:LANES] = jnp.where(lo, 0.0, own_hi).astype(_BF16)

    def proj_q(half):
        qh = proj(C_Q + half * 256, 256) * (HEAD_DIM ** -0.5)
        for i in range(4):
            qbuf[half * 4 + i, :, :] = qh[:, i * HEAD_DIM:(i + 1) * HEAD_DIM].astype(_BF16)

    def proj_gattn(half):
        ga_buf[:, half * 256:(half + 1) * 256] = _silu(proj(C_GATTN + half * 256, 256))

    inv_w = [jnp.where(low_half, 1.0 / POOL_WINDOWS[0], 1.0 / POOL_WINDOWS[1]).astype(_F32),
             jnp.where(low_half, 1.0 / POOL_WINDOWS[2], 1.0 / POOL_WINDOWS[3]).astype(_F32)]

    def pool_chunk(c):
        r = c * CHUNK
        n_in = GROUPS + HALO_U // SUBLANES
        for col in range(2):
            lanes = slice(col * LANES, (col + 1) * LANES)
            u = [ubuf[pl.ds(r + SUBLANES * i, SUBLANES), lanes] for i in range(n_in)]
            s2 = [a + b for a, b in zip(u, _shift_down(u, 1, row))]
            s4 = [a + b for a, b in zip(s2, _shift_down(s2, 2, row))]
            if col == 0:
                wide, narrow = s4, s2
            else:
                s8 = [a + b for a, b in zip(s4, _shift_down(s4, 4, row))]
                s16 = [s8[0]] + [s8[i] + s8[i - 1] for i in range(1, n_in)]
                wide, narrow = s16, s8
            pieces = []
            for g in range(GROUPS):
                i = g + HALO_U // SUBLANES
                inv = inv_w[col]
                if c == 0 and g < HALO_U // SUBLANES:
                    tab = invcnt_ref[SUBLANES * g:SUBLANES * (g + 1), lanes]
                    inv = jnp.where(first_tile, tab, inv)
                pooled = jnp.where(low_half, narrow[i], wide[i]) * inv
                pieces.append(pooled - u[i])
            zp_buf[pl.ds(r, CHUNK), lanes] = jnp.concatenate(pieces, axis=0).astype(_BF16)

    taps_by_b = [[] for _ in range(SUBLANES)]
    for k in range(CONV_KERNEL):
        a, b = divmod(k + HALO_H - (CONV_KERNEL - 1), SUBLANES)
        taps_by_b[b].append((a, k))
    n_h = GROUPS + HALO_H // SUBLANES

    def conv_chunk(c):
        r = c * CHUNK
        outs = []
        for col in range(2):
            lanes = slice(col * LANES, (col + 1) * LANES)
            h = [hbuf[pl.ds(r + SUBLANES * m, SUBLANES), lanes] for m in range(n_h)]
            acc = None
            for b in range(SUBLANES):
                n_p = GROUPS if b == 0 else GROUPS + 1
                part = []
                for m in range(n_p):
                    p = None
                    for a, k in taps_by_b[b]:
                        term = h[m + a] * dw_ref[0, k:k + 1, lanes]
                        p = term if p is None else p + term
                    part.append(p)
                if b == 0:
                    acc = part
                else:
                    rolled = [pltpu.roll(p, SUBLANES - b, axis=0) for p in part]
                    keep = row < (SUBLANES - b)
                    acc = [acc[g] + jnp.where(keep, rolled[g], rolled[g + 1]) for g in range(GROUPS)]
            outs.append(jnp.concatenate(acc, axis=0))
        conv = jnp.concatenate(outs, axis=1) + cb_ref[0]
        mu = jnp.mean(conv, axis=-1, keepdims=True)
        cen = conv - mu
        var = jnp.mean(cen * cen, axis=-1, keepdims=True)
        z = cen * lax.rsqrt(var + EPS) * clg_ref[0] + clb_ref[0]
        zc_buf[pl.ds(r, CHUNK), :] = _silu(z).astype(_BF16)

    side_work = [proj_upool, proj_gpool, proj_kv, lambda: proj_q(0), lambda: proj_q(1), proj_gconv,
                 lambda: proj_gattn(0), lambda: proj_gattn(1)]
    assert len(side_work) == TS // CHUNK
    for c in range(TS // CHUNK):
        conv_chunk(c)
        side_work[c]()
        if c >= 1:
            pool_chunk(c - 1)
    pool_chunk(TS // CHUNK - 1)

    n_local = POOL_WIDTH + CONV_WIDTH

    sink_col = lax.broadcasted_iota(jnp.int32, (BLOCK, LANES), 1) == 0
    top_rows = lax.broadcasted_iota(jnp.int32, (16, 2 * LANES), 0)
    top_lanes = lax.broadcasted_iota(jnp.int32, (16, 2 * LANES), 1)
    sink_value_slot = jnp.logical_and(top_rows == 0, top_lanes < LANES)

    def scores(n):
        r = n * BLOCK
        seq_start = first_tile.astype(jnp.int32) if n == 0 else 0
        out = []
        for kvh in range(N_KV_HEADS):
            k2 = kbuf[kvh, pl.ds(r, 2 * BLOCK), :]
            q4 = jnp.concatenate([qbuf[kvh * Q_PER_KV + g, pl.ds(r, BLOCK), :] for g in range(Q_PER_KV)], axis=0)
            s = lax.dot_general(q4, k2, (((1,), (1,)), ((), ())), preferred_element_type=_F32)
            out.append(s + bias_ref[seq_start, kvh])
        return out

    def window_values(buf, kvh, r):
        top = buf[kvh, pl.ds(r, 16), :].astype(_F32)
        top = jnp.where(sink_value_slot, 0.0, top).astype(_BF16)
        return jnp.concatenate([top, buf[kvh, pl.ds(r + 16, 2 * BLOCK - 16), :]], axis=0)

    def softmax_pv(n, s_list):
        r = n * BLOCK
        for kvh in range(N_KV_HEADS):
            probs = []
            for g in range(Q_PER_KV):
                sg = s_list[kvh][g * BLOCK:(g + 1) * BLOCK]
                left = jnp.where(sink_col, sinks_ref[layer, kvh * Q_PER_KV + g], sg[:, 0:LANES])
                sg = jnp.concatenate([left, sg[:, LANES:]], axis=1)
                m = jnp.max(sg, axis=-1, keepdims=True)
                probs.append(jnp.exp(sg - m).astype(_BF16))
            p_even = jnp.concatenate([probs[0], probs[2]], axis=0)
            p_odd = jnp.concatenate([probs[1], probs[3]], axis=0)
            o = (jnp.dot(p_even, window_values(ve_buf, kvh, r), preferred_element_type=_F32)
                 + jnp.dot(p_odd, window_values(vo_buf, kvh, r), preferred_element_type=_F32))
            res = o[:, 0:LANES] / o[:, LANES:2 * LANES]
            for pair in range(2):
                c0 = (kvh * 2 + pair) * LANES
                gate = ga_buf[pl.ds(r, BLOCK), c0:c0 + LANES]
                mix_ref[pl.ds(r, BLOCK), n_local + c0:n_local + c0 + LANES] = (
                    res[pair * BLOCK:(pair + 1) * BLOCK] * gate).astype(_BF16)

    def out_attn(r0, rows):
        for q in range(D_MODEL // 256):
            cols = slice(q * 256, (q + 1) * 256)
            o_ref[0, pl.ds(r0, rows), cols] = x_ref[0, pl.ds(r0, rows), cols] + jnp.dot(
                mix_ref[pl.ds(r0, rows), n_local:D_MIX], wout_ref[0, n_local:D_MIX, cols],
                preferred_element_type=_F32)

    s_cur = scores(0)
    for n in range(NB):
        s_next = scores(n + 1) if n + 1 < NB else None
        if n >= 2 and n % 2 == 0:
            out_attn((n - 2) * BLOCK, 2 * BLOCK)
        softmax_pv(n, s_cur)
        s_cur = s_next
    out_attn((NB - 2) * BLOCK, 2 * BLOCK)

    y_pool = jnp.dot(zp_buf[...], poolw_ref[0], preferred_element_type=_F32) * pscale_ref[0]
    mix_ref[:, 0:POOL_WIDTH] = (y_pool * gp_buf[...]).astype(_BF16)
    y_conv = jnp.dot(zc_buf[...], pw_ref[0], preferred_element_type=_F32)
    mix_ref[:, POOL_WIDTH:POOL_WIDTH + CONV_WIDTH] = (y_conv * gc_buf[...]).astype(_BF16)

    for q in range(D_MODEL // 256):
        cols = slice(q * 256, (q + 1) * 256)
        o_ref[0, :, cols] = o_ref[0, :, cols] + jnp.dot(
            mix_ref[:, 0:n_local], wout_ref[0, 0:n_local, cols], preferred_element_type=_F32)

    if final:
        def final_chunk(c):
            r = c * CHUNK
            xc = o_ref[0, pl.ds(r, CHUNK), :]
            ms = jnp.mean(xc * xc, axis=-1, keepdims=True)
            o_ref[0, pl.ds(r, CHUNK), :] = xc * lax.rsqrt(ms + EPS) * fg_ref[...]

        _unrolled(TS // CHUNK, final_chunk)


def _bias_table():
    i = np.arange(BLOCK)[:, None]
    jj = np.arange(2 * BLOCK)[None, :]
    dist = BLOCK + i - jj
    in_band = (dist >= 0) & (dist < WINDOW)
    slopes = np.asarray([2.0 ** (-8.0 * (h + 1) / N_Q_HEADS) for h in range(N_Q_HEADS)], dtype=np.float32)
    tab = np.empty((2, N_KV_HEADS, Q_PER_KV * BLOCK, 2 * BLOCK), np.float32)
    for start in range(2):
        valid = in_band & ~((jj < BLOCK) & (start == 1))
        for h in range(N_Q_HEADS):
            bias = -slopes[h] * dist.astype(np.float32)
            kvh, g = divmod(h, Q_PER_KV)
            tab[start, kvh, g * BLOCK:(g + 1) * BLOCK] = np.where(valid, bias, np.float32(-1e30))
    return tab


def _inv_count_table():
    t = np.arange(HALO_U, dtype=np.float32)[:, None]
    gw = POOL_WIDTH // POOL_GROUPS
    wnd = np.repeat(np.asarray(POOL_WINDOWS, np.float32), gw)[None, :]
    return (1.0 / np.minimum(t + 1.0, wnd)).astype(np.float32)


def _layer_call(batch, seq, layer, final):
    n_tiles = seq // TS
    const = lambda shape: pl.BlockSpec(shape, lambda b, j: (0,) * len(shape))
    per_layer = lambda shape: pl.BlockSpec((1,) + shape, lambda b, j: (layer,) + (0,) * len(shape))
    in_specs = [
        pl.BlockSpec((1, TS, D_MODEL), lambda b, j: (b, j, 0)),
        per_layer((1, D_MODEL)),
        per_layer((D_MODEL, D_IN)),
        per_layer((POOL_WIDTH, POOL_WIDTH)),
        per_layer((1, POOL_WIDTH)),
        const((HALO_U, POOL_WIDTH)),
        per_layer((CONV_KERNEL, CONV_WIDTH)),
        per_layer((1, CONV_WIDTH)),
        per_layer((1, CONV_WIDTH)),
        per_layer((1, CONV_WIDTH)),
        per_layer((CONV_WIDTH, CONV_WIDTH)),
        pl.BlockSpec(memory_space=pltpu.SMEM),
        const((2, N_KV_HEADS, Q_PER_KV * BLOCK, 2 * BLOCK)),
        per_layer((D_MIX, D_MODEL)),
        const((1, D_MODEL)),
    ]
    scratch = [
        pltpu.VMEM((TS, D_MODEL), _BF16),
        pltpu.VMEM((TS + HALO_U, POOL_WIDTH), _F32),
        pltpu.VMEM((TS + HALO_H, CONV_WIDTH), _F32),
        pltpu.VMEM((TS, POOL_WIDTH), _F32),
        pltpu.VMEM((TS, CONV_WIDTH), _F32),
        pltpu.VMEM((TS, ATTN_WIDTH), _F32),
        pltpu.VMEM((TS, POOL_WIDTH), _BF16),
        pltpu.VMEM((TS, CONV_WIDTH), _BF16),
        pltpu.VMEM((N_KV_HEADS, TS + BLOCK, HEAD_DIM), _BF16),
        pltpu.VMEM((N_KV_HEADS, TS + BLOCK, 2 * LANES), _BF16),
        pltpu.VMEM((N_KV_HEADS, TS + BLOCK, 2 * LANES), _BF16),
        pltpu.VMEM((N_Q_HEADS, TS, HEAD_DIM), _BF16),
        pltpu.VMEM((TS, D_MIX), _BF16),
    ]
    return pl.pallas_call(
        functools.partial(_layer_body, layer=layer, final=final),
        out_shape=jax.ShapeDtypeStruct((batch, seq, D_MODEL), _F32),
        grid=(batch, n_tiles),
        in_specs=in_specs,
        out_specs=pl.BlockSpec((1, TS, D_MODEL), lambda b, j: (b, j, 0)),
        scratch_shapes=scratch,
        compiler_params=pltpu.CompilerParams(
            dimension_semantics=("arbitrary", "arbitrary"),
            vmem_limit_bytes=VMEM_LIMIT_BYTES,
        ),
        name="hybrid_layer_final" if final else "hybrid_layer",
    )


def kernel(x, ln_g, w_in, pool_w, pool_scale, conv_dw, conv_b, conv_ln_g, conv_ln_b, conv_pw, attn_sinks, w_out, final_g):
    batch, seq, d_model = x.shape
    assert d_model == D_MODEL and seq % TS == 0 and TS % BLOCK == 0
    bias = jnp.asarray(_bias_table())
    inv_cnt = jnp.asarray(_inv_count_table())
    rows = lambda v: v.reshape(DEPTH, 1, -1)
    poolw_bd = jnp.stack([jax.scipy.linalg.block_diag(*[pool_w[l, g] for g in range(POOL_GROUPS)])
                          for l in range(DEPTH)]).astype(_BF16)
    params = (rows(ln_g), w_in.astype(_BF16), poolw_bd, rows(pool_scale), inv_cnt, conv_dw, rows(conv_b),
              rows(conv_ln_g), rows(conv_ln_b), conv_pw.astype(_BF16), attn_sinks, bias, w_out.astype(_BF16),
              final_g.reshape(1, -1))
    for l in range(DEPTH):
        x = _layer_call(batch, seq, l, final=(l == DEPTH - 1))(x, *params)
    return x
```

```python
import functools

import numpy as np
import jax
import jax.numpy as jnp
from jax import lax
from jax.experimental import pallas as pl
from jax.experimental.pallas import tpu as pltpu

D_MODEL = 1024
DEPTH = 2
POOL_WIDTH = 256
POOL_GROUPS = 4
POOL_WINDOWS = (2, 4, 8, 16)
CONV_WIDTH = 256
CONV_KERNEL = 31
HEAD_DIM = 64
N_Q_HEADS = 8
N_KV_HEADS = 2
Q_PER_KV = N_Q_HEADS // N_KV_HEADS
ATTN_WIDTH = N_Q_HEADS * HEAD_DIM
KV_WIDTH = N_KV_HEADS * HEAD_DIM
WINDOW = 128
BLOCK = 128
D_MIX = POOL_WIDTH + CONV_WIDTH + ATTN_WIDTH
D_IN = 2 * POOL_WIDTH + 3 * CONV_WIDTH + 2 * ATTN_WIDTH + 2 * KV_WIDTH
EPS = 1e-6

C_UPOOL = 0
C_GPOOL = C_UPOOL + POOL_WIDTH
C_CA = C_GPOOL + POOL_WIDTH
C_CB = C_CA + CONV_WIDTH
C_GCONV = C_CB + CONV_WIDTH
C_Q = C_GCONV + CONV_WIDTH
C_K = C_Q + ATTN_WIDTH
C_V = C_K + KV_WIDTH
C_GATTN = C_V + KV_WIDTH

SUBLANES = 8
LANES = 128
TS = 512
NB = TS // BLOCK
CHUNK = 64
GROUPS = CHUNK // SUBLANES
HALO_U = 16
HALO_H = 32
VMEM_LIMIT_BYTES = 56 * 1024 * 1024
LOG2E = 1.4426950408889634

_F32 = jnp.float32
_BF16 = jnp.bfloat16


def _silu(v):
    return v * jax.nn.sigmoid(v)


def _unrolled(n, body):
    for c in range(n):
        body(c)


def _shift_down(groups, d, row):
    rolled = [pltpu.roll(g, d, axis=0) for g in groups]
    take_prev = row < d
    out = [rolled[0]]
    for i in range(1, len(groups)):
        out.append(jnp.where(take_prev, rolled[i - 1], rolled[i]))
    return out


def _layer_body(x_ref, lng_ref, win_ref, poolw_ref, pscale_ref, invcnt_ref, dw_ref, cb_ref, clg_ref, clb_ref,
                pw_ref, sinks_ref, bias_ref, wout_ref, fg_ref, o_ref,
                hn_ref, ubuf, hbuf, gp_buf, gc_buf, ga_buf, zp_buf, zc_buf, kbuf, ve_buf, vo_buf, qbuf, mix_ref,
                bias_buf, *, layer, final):
    j = pl.program_id(1)
    first_tile = j == 0

    row = lax.broadcasted_iota(jnp.int32, (SUBLANES, LANES), 0)
    lane = lax.broadcasted_iota(jnp.int32, (SUBLANES, LANES), 1)
    low_half = lane < HEAD_DIM

    @pl.when(first_tile)
    def _():
        sink_slot = lax.broadcasted_iota(jnp.int32, (BLOCK, 2 * BLOCK), 1) == 0
        for start in range(2):
            for h in range(N_Q_HEADS):
                kvh, g = divmod(h, Q_PER_KV)
                rows = pl.ds(g * BLOCK, BLOCK)
                bias_buf[start, kvh, rows, :] = jnp.where(
                    sink_slot, sinks_ref[layer, h] * LOG2E, bias_ref[start, kvh, rows, :])
        ubuf[0:HALO_U, :] = jnp.zeros((HALO_U, POOL_WIDTH), _F32)
        hbuf[0:HALO_H, :] = jnp.zeros((HALO_H, CONV_WIDTH), _F32)
        kbuf[:, 0:BLOCK, :] = jnp.zeros((N_KV_HEADS, BLOCK, HEAD_DIM), _BF16)
        zeros = jnp.zeros((TS + BLOCK, LANES), _BF16)
        lane_t = lax.broadcasted_iota(jnp.int32, (TS + BLOCK, LANES), 1)
        ones_lo = jnp.where(lane_t < HEAD_DIM, 1.0, 0.0).astype(_BF16)
        ones_hi = jnp.where(lane_t < HEAD_DIM, 0.0, 1.0).astype(_BF16)
        for kvh in range(N_KV_HEADS):
            ve_buf[kvh, :, 0:LANES] = zeros
            vo_buf[kvh, :, 0:LANES] = zeros
            ve_buf[kvh, :, LANES:2 * LANES] = ones_lo
            vo_buf[kvh, :, LANES:2 * LANES] = ones_hi

    @pl.when(jnp.logical_not(first_tile))
    def _():
        ubuf[0:HALO_U, :] = ubuf[TS:TS + HALO_U, :]
        hbuf[0:HALO_H, :] = hbuf[TS:TS + HALO_H, :]
        kbuf[:, 0:BLOCK, :] = kbuf[:, TS:TS + BLOCK, :]
        ve_buf[:, 0:BLOCK, 0:LANES] = ve_buf[:, TS:TS + BLOCK, 0:LANES]
        vo_buf[:, 0:BLOCK, 0:LANES] = vo_buf[:, TS:TS + BLOCK, 0:LANES]

    def norm_chunk(c):
        r = c * CHUNK
        xc = x_ref[0, pl.ds(r, CHUNK), :]
        ms = jnp.mean(xc * xc, axis=-1, keepdims=True)
        y = xc * lax.rsqrt(ms + EPS) * lng_ref[0]
        hn_ref[pl.ds(r, CHUNK), :] = y.astype(_BF16)

    _unrolled(TS // CHUNK, norm_chunk)

    def proj(c0, width, r0=0, rows=TS):
        return jnp.dot(hn_ref[pl.ds(r0, rows), :], win_ref[0, :, c0:c0 + width], preferred_element_type=_F32)

    for r0 in range(0, TS, TS // 2):
        rows = pl.ds(HALO_H + r0, TS // 2)
        hbuf[rows, :] = proj(C_CA, CONV_WIDTH, r0, TS // 2)
        hbuf[rows, :] = hbuf[rows, :] * jax.nn.sigmoid(proj(C_CB, CONV_WIDTH, r0, TS // 2))

    def proj_upool():
        ubuf[HALO_U:HALO_U + TS, :] = proj(C_UPOOL, POOL_WIDTH)

    def proj_gpool():
        gp_buf[...] = _silu(proj(C_GPOOL, POOL_WIDTH))

    def proj_gconv():
        gc_buf[...] = _silu(proj(C_GCONV, CONV_WIDTH))

    def proj_kv():
        kv = proj(C_K, 2 * KV_WIDTH)
        keys = kv[:, 0:KV_WIDTH]
        vals = kv[:, KV_WIDTH:2 * KV_WIDTH]
        vals_swapped = pltpu.roll(vals, HEAD_DIM, axis=1)
        lo = lax.broadcasted_iota(jnp.int32, (TS, LANES), 1) < HEAD_DIM
        for kvh in range(N_KV_HEADS):
            kbuf[kvh, BLOCK:BLOCK + TS, :] = keys[:, kvh * HEAD_DIM:(kvh + 1) * HEAD_DIM].astype(_BF16)
            own_lo, own_hi = (vals, vals_swapped) if kvh == 0 else (vals_swapped, vals)
            ve_buf[kvh, BLOCK:BLOCK + TS, 0:LANES] = jnp.where(lo, own_lo, 0.0).astype(_BF16)
            vo_buf[kvh, BLOCK:BLOCK + TS, 0:LANES] = jnp.where(lo, 0.0, own_hi).astype(_BF16)

    def proj_q(half):
        qh = proj(C_Q + half * 256, 256) * (HEAD_DIM ** -0.5 * LOG2E)
        for i in range(4):
            qbuf[half * 4 + i, :, :] = qh[:, i * HEAD_DIM:(i + 1) * HEAD_DIM].astype(_BF16)

    def proj_gattn(half):
        ga_buf[:, half * 256:(half + 1) * 256] = _silu(proj(C_GATTN + half * 256, 256))

    inv_w = [jnp.where(low_half, 1.0 / POOL_WINDOWS[0], 1.0 / POOL_WINDOWS[1]).astype(_F32),
             jnp.where(low_half, 1.0 / POOL_WINDOWS[2], 1.0 / POOL_WINDOWS[3]).astype(_F32)]

    def pool_chunk(c):
        r = c * CHUNK
        n_in = GROUPS + HALO_U // SUBLANES
        for col in range(2):
            lanes = slice(col * LANES, (col + 1) * LANES)
            u = [ubuf[pl.ds(r + SUBLANES * i, SUBLANES), lanes] for i in range(n_in)]
            s2 = [a + b for a, b in zip(u, _shift_down(u, 1, row))]
            s4 = [a + b for a, b in zip(s2, _shift_down(s2, 2, row))]
            if col == 0:
                wide, narrow = s4, s2
            else:
                s8 = [a + b for a, b in zip(s4, _shift_down(s4, 4, row))]
                s16 = [s8[0]] + [s8[i] + s8[i - 1] for i in range(1, n_in)]
                wide, narrow = s16, s8
            pieces = []
            for g in range(GROUPS):
                i = g + HALO_U // SUBLANES
                inv = inv_w[col]
                if c == 0 and g < HALO_U // SUBLANES:
                    tab = invcnt_ref[SUBLANES * g:SUBLANES * (g + 1), lanes]
                    inv = jnp.where(first_tile, tab, inv)
                pooled = jnp.where(low_half, narrow[i], wide[i]) * inv
                pieces.append(pooled - u[i])
            zp_buf[pl.ds(r, CHUNK), lanes] = jnp.concatenate(pieces, axis=0).astype(_BF16)

    taps_by_b = [[] for _ in range(SUBLANES)]
    for k in range(CONV_KERNEL):
        a, b = divmod(k + HALO_H - (CONV_KERNEL - 1), SUBLANES)
        taps_by_b[b].append((a, k))
    n_h = GROUPS + HALO_H // SUBLANES

    def conv_chunk(c):
        r = c * CHUNK
        outs = []
        for col in range(2):
            lanes = slice(col * LANES, (col + 1) * LANES)
            h = [hbuf[pl.ds(r + SUBLANES * m, SUBLANES), lanes] for m in range(n_h)]
            acc = None
            for b in range(SUBLANES):
                n_p = GROUPS if b == 0 else GROUPS + 1
                part = []
                for m in range(n_p):
                    p = None
                    for a, k in taps_by_b[b]:
                        term = h[m + a] * dw_ref[0, k:k + 1, lanes]
                        p = term if p is None else p + term
                    part.append(p)
                if b == 0:
                    acc = part
                else:
                    rolled = [pltpu.roll(p, SUBLANES - b, axis=0) for p in part]
                    keep = row < (SUBLANES - b)
                    acc = [acc[g] + jnp.where(keep, rolled[g], rolled[g + 1]) for g in range(GROUPS)]
            outs.append(jnp.concatenate(acc, axis=0))
        conv = jnp.concatenate(outs, axis=1) + cb_ref[0]
        mu = jnp.mean(conv, axis=-1, keepdims=True)
        cen = conv - mu
        var = jnp.mean(cen * cen, axis=-1, keepdims=True)
        z = cen * lax.rsqrt(var + EPS) * clg_ref[0] + clb_ref[0]
        zc_buf[pl.ds(r, CHUNK), :] = _silu(z).astype(_BF16)

    side_work = [proj_kv, functools.partial(proj_q, 0), functools.partial(proj_q, 1),
                 functools.partial(proj_gattn, 0), functools.partial(proj_gattn, 1),
                 proj_upool, proj_gpool, proj_gconv]
    assert len(side_work) == TS // CHUNK
    for c in range(TS // CHUNK):
        side_work[c]()
        conv_chunk(c)
    _unrolled(TS // CHUNK, pool_chunk)

    n_local = POOL_WIDTH + CONV_WIDTH

    top_rows = lax.broadcasted_iota(jnp.int32, (16, 2 * LANES), 0)
    top_lanes = lax.broadcasted_iota(jnp.int32, (16, 2 * LANES), 1)
    sink_value_slot = jnp.logical_and(top_rows == 0, top_lanes < LANES)
    sink_key_slot = lax.broadcasted_iota(jnp.int32, (16, HEAD_DIM), 0) == 0

    def window_keys(kvh, r):
        top = kbuf[kvh, pl.ds(r, 16), :].astype(_F32)
        top = jnp.where(sink_key_slot, 0.0, top).astype(_BF16)
        return jnp.concatenate([top, kbuf[kvh, pl.ds(r + 16, 2 * BLOCK - 16), :]], axis=0)

    def scores(n):
        r = n * BLOCK
        seq_start = first_tile.astype(jnp.int32) if n == 0 else 0
        out = []
        for kvh in range(N_KV_HEADS):
            k2 = window_keys(kvh, r)
            q4 = jnp.concatenate([qbuf[kvh * Q_PER_KV + g, pl.ds(r, BLOCK), :] for g in range(Q_PER_KV)], axis=0)
            s = lax.dot_general(q4, k2, (((1,), (1,)), ((), ())), preferred_element_type=_F32)
            out.append(s + bias_buf[seq_start, kvh])
        return out

    def window_values(buf, kvh, r):
        top = buf[kvh, pl.ds(r, 16), :].astype(_F32)
        top = jnp.where(sink_value_slot, 0.0, top).astype(_BF16)
        return jnp.concatenate([top, buf[kvh, pl.ds(r + 16, 2 * BLOCK - 16), :]], axis=0)

    def softmax_pv(n, s_list):
        r = n * BLOCK
        for kvh in range(N_KV_HEADS):
            probs = []
            for g in range(Q_PER_KV):
                sg = s_list[kvh][g * BLOCK:(g + 1) * BLOCK]
                m = jnp.max(sg, axis=-1, keepdims=True)
                probs.append(jnp.exp2(sg - m).astype(_BF16))
            p_even = jnp.concatenate([probs[0], probs[2]], axis=0)
            p_odd = jnp.concatenate([probs[1], probs[3]], axis=0)
            o = (jnp.dot(p_even, window_values(ve_buf, kvh, r), preferred_element_type=_F32)
                 + jnp.dot(p_odd, window_values(vo_buf, kvh, r), preferred_element_type=_F32))
            res = o[:, 0:LANES] / o[:, LANES:2 * LANES]
            for pair in range(2):
                c0 = (kvh * 2 + pair) * LANES
                gate = ga_buf[pl.ds(r, BLOCK), c0:c0 + LANES]
                mix_ref[pl.ds(r, BLOCK), n_local + c0:n_local + c0 + LANES] = (
                    res[pair * BLOCK:(pair + 1) * BLOCK] * gate).astype(_BF16)

    def out_attn(r0, rows):
        for q in range(D_MODEL // 256):
            cols = slice(q * 256, (q + 1) * 256)
            o_ref[0, pl.ds(r0, rows), cols] = x_ref[0, pl.ds(r0, rows), cols] + jnp.dot(
                mix_ref[pl.ds(r0, rows), n_local:D_MIX], wout_ref[0, n_local:D_MIX, cols],
                preferred_element_type=_F32)

    s_cur = scores(0)
    for n in range(NB):
        s_next = scores(n + 1) if n + 1 < NB else None
        if n >= 2 and n % 2 == 0:
            out_attn((n - 2) * BLOCK, 2 * BLOCK)
        softmax_pv(n, s_cur)
        s_cur = s_next
    out_attn((NB - 2) * BLOCK, 2 * BLOCK)

    y_pool = jnp.dot(zp_buf[...], poolw_ref[0], preferred_element_type=_F32) * pscale_ref[0]
    mix_ref[:, 0:POOL_WIDTH] = (y_pool * gp_buf[...]).astype(_BF16)
    y_conv = jnp.dot(zc_buf[...], pw_ref[0], preferred_element_type=_F32)
    mix_ref[:, POOL_WIDTH:POOL_WIDTH + CONV_WIDTH] = (y_conv * gc_buf[...]).astype(_BF16)

    for q in range(D_MODEL // 256):
        cols = slice(q * 256, (q + 1) * 256)
        o_ref[0, :, cols] = o_ref[0, :, cols] + jnp.dot(
            mix_ref[:, 0:n_local], wout_ref[0, 0:n_local, cols], preferred_element_type=_F32)

    if final:
        def final_chunk(c):
            r = c * CHUNK
            xc = o_ref[0, pl.ds(r, CHUNK), :]
            ms = jnp.mean(xc * xc, axis=-1, keepdims=True)
            o_ref[0, pl.ds(r, CHUNK), :] = xc * lax.rsqrt(ms + EPS) * fg_ref[...]

        _unrolled(TS // CHUNK, final_chunk)


def _bias_table():
    i = np.arange(BLOCK)[:, None]
    jj = np.arange(2 * BLOCK)[None, :]
    dist = BLOCK + i - jj
    in_band = (dist >= 0) & (dist < WINDOW)
    slopes = np.asarray([2.0 ** (-8.0 * (h + 1) / N_Q_HEADS) for h in range(N_Q_HEADS)], dtype=np.float32)
    tab = np.empty((2, N_KV_HEADS, Q_PER_KV * BLOCK, 2 * BLOCK), np.float32)
    for start in range(2):
        valid = in_band & ~((jj < BLOCK) & (start == 1))
        for h in range(N_Q_HEADS):
            bias = (-slopes[h] * LOG2E) * dist.astype(np.float64)
            kvh, g = divmod(h, Q_PER_KV)
            tab[start, kvh, g * BLOCK:(g + 1) * BLOCK] = np.where(valid, bias, -1e30).astype(np.float32)
    return tab


def _inv_count_table():
    t = np.arange(HALO_U, dtype=np.float32)[:, None]
    gw = POOL_WIDTH // POOL_GROUPS
    wnd = np.repeat(np.asarray(POOL_WINDOWS, np.float32), gw)[None, :]
    return (1.0 / np.minimum(t + 1.0, wnd)).astype(np.float32)


def _layer_call(batch, seq, layer, final):
    n_tiles = seq // TS
    const = lambda shape: pl.BlockSpec(shape, lambda b, j: (0,) * len(shape))
    per_layer = lambda shape: pl.BlockSpec((1,) + shape, lambda b, j: (layer,) + (0,) * len(shape))
    in_specs = [
        pl.BlockSpec((1, TS, D_MODEL), lambda b, j: (b, j, 0)),
        per_layer((1, D_MODEL)),
        per_layer((D_MODEL, D_IN)),
        per_layer((POOL_WIDTH, POOL_WIDTH)),
        per_layer((1, POOL_WIDTH)),
        const((HALO_U, POOL_WIDTH)),
        per_layer((CONV_KERNEL, CONV_WIDTH)),
        per_layer((1, CONV_WIDTH)),
        per_layer((1, CONV_WIDTH)),
        per_layer((1, CONV_WIDTH)),
        per_layer((CONV_WIDTH, CONV_WIDTH)),
        pl.BlockSpec(memory_space=pltpu.SMEM),
        const((2, N_KV_HEADS, Q_PER_KV * BLOCK, 2 * BLOCK)),
        per_layer((D_MIX, D_MODEL)),
        const((1, D_MODEL)),
    ]
    scratch = [
        pltpu.VMEM((TS, D_MODEL), _BF16),
        pltpu.VMEM((TS + HALO_U, POOL_WIDTH), _F32),
        pltpu.VMEM((TS + HALO_H, CONV_WIDTH), _F32),
        pltpu.VMEM((TS, POOL_WIDTH), _F32),
        pltpu.VMEM((TS, CONV_WIDTH), _F32),
        pltpu.VMEM((TS, ATTN_WIDTH), _F32),
        pltpu.VMEM((TS, POOL_WIDTH), _BF16),
        pltpu.VMEM((TS, CONV_WIDTH), _BF16),
        pltpu.VMEM((N_KV_HEADS, TS + BLOCK, HEAD_DIM), _BF16),
        pltpu.VMEM((N_KV_HEADS, TS + BLOCK, 2 * LANES), _BF16),
        pltpu.VMEM((N_KV_HEADS, TS + BLOCK, 2 * LANES), _BF16),
        pltpu.VMEM((N_Q_HEADS, TS, HEAD_DIM), _BF16),
        pltpu.VMEM((TS, D_MIX), _BF16),
        pltpu.VMEM((2, N_KV_HEADS, Q_PER_KV * BLOCK, 2 * BLOCK), _F32),
    ]
    return pl.pallas_call(
        functools.partial(_layer_body, layer=layer, final=final),
        out_shape=jax.ShapeDtypeStruct((batch, seq, D_MODEL), _F32),
        grid=(batch, n_tiles),
        in_specs=in_specs,
        out_specs=pl.BlockSpec((1, TS, D_MODEL), lambda b, j: (b, j, 0)),
        scratch_shapes=scratch,
        compiler_params=pltpu.CompilerParams(
            dimension_semantics=("arbitrary", "arbitrary"),
            vmem_limit_bytes=VMEM_LIMIT_BYTES,
        ),
        name="hybrid_layer_final" if final else "hybrid_layer",
    )


def kernel(x, ln_g, w_in, pool_w, pool_scale, conv_dw, conv_b, conv_ln_g, conv_ln_b, conv_pw, attn_sinks, w_out, final_g):
    batch, seq, d_model = x.shape
    assert d_model == D_MODEL and seq % TS == 0 and TS % BLOCK == 0
    bias = jnp.asarray(_bias_table())
    inv_cnt = jnp.asarray(_inv_count_table())
    rows = lambda v: v.reshape(DEPTH, 1, -1)
    poolw_bd = jnp.stack([jax.scipy.linalg.block_diag(*[pool_w[l, g] for g in range(POOL_GROUPS)])
                          for l in range(DEPTH)]).astype(_BF16)
    params = (rows(ln_g), w_in.astype(_BF16), poolw_bd, rows(pool_scale), inv_cnt, conv_dw, rows(conv_b),
              rows(conv_ln_g), rows(conv_ln_b), conv_pw.astype(_BF16), attn_sinks, bias, w_out.astype(_BF16),
              final_g.reshape(1, -1))
    for l in range(DEPTH):
        x = _layer_call(batch, seq, l, final=(l == DEPTH - 1))(x, *params)
    return x
```

```python
import functools

import numpy as np
import jax
import jax.numpy as jnp
from jax import lax
from jax.experimental import pallas as pl
from jax.experimental.pallas import tpu as pltpu

D_MODEL = 1024
DEPTH = 2
POOL_WIDTH = 256
POOL_GROUPS = 4
POOL_WINDOWS = (2, 4, 8, 16)
CONV_WIDTH = 256
CONV_KERNEL = 31
HEAD_DIM = 64
N_Q_HEADS = 8
N_KV_HEADS = 2
Q_PER_KV = N_Q_HEADS // N_KV_HEADS
ATTN_WIDTH = N_Q_HEADS * HEAD_DIM
KV_WIDTH = N_KV_HEADS * HEAD_DIM
WINDOW = 128
BLOCK = 128
D_MIX = POOL_WIDTH + CONV_WIDTH + ATTN_WIDTH
D_IN = 2 * POOL_WIDTH + 3 * CONV_WIDTH + 2 * ATTN_WIDTH + 2 * KV_WIDTH
EPS = 1e-6

C_UPOOL = 0
C_GPOOL = C_UPOOL + POOL_WIDTH
C_CA = C_GPOOL + POOL_WIDTH
C_CB = C_CA + CONV_WIDTH
C_GCONV = C_CB + CONV_WIDTH
C_Q = C_GCONV + CONV_WIDTH
C_K = C_Q + ATTN_WIDTH
C_V = C_K + KV_WIDTH
C_GATTN = C_V + KV_WIDTH

SUBLANES = 8
LANES = 128
TS = 1024
NB = TS // BLOCK
CHUNK = 64
GROUPS = CHUNK // SUBLANES
HALO_U = 16
HALO_H = 32
VMEM_LIMIT_BYTES = 60 * 1024 * 1024
LOG2E = 1.4426950408889634

_F32 = jnp.float32
_BF16 = jnp.bfloat16


def _silu(v):
    return v * jax.nn.sigmoid(v)


def _unrolled(n, body):
    for c in range(n):
        body(c)


def _shift_down(groups, d, row):
    rolled = [pltpu.roll(g, d, axis=0) for g in groups]
    take_prev = row < d
    out = [rolled[0]]
    for i in range(1, len(groups)):
        out.append(jnp.where(take_prev, rolled[i - 1], rolled[i]))
    return out


def _layer_body(x_ref, lng_ref, win_ref, poolw_ref, pscale_ref, invcnt_ref, dw_ref, cb_ref, clg_ref, clb_ref,
                pw_ref, sinks_ref, bias_ref, wout_ref, fg_ref, o_ref,
                hn_ref, ubuf, hbuf, gp_buf, gc_buf, ga_buf, zp_buf, zc_buf, kbuf, ve_buf, vo_buf, qbuf, mix_ref,
                bias_buf, *, layer, final):
    j = pl.program_id(1)
    first_tile = j == 0

    row = lax.broadcasted_iota(jnp.int32, (SUBLANES, LANES), 0)
    lane = lax.broadcasted_iota(jnp.int32, (SUBLANES, LANES), 1)
    low_half = lane < HEAD_DIM

    @pl.when(first_tile)
    def _():
        sink_slot = lax.broadcasted_iota(jnp.int32, (BLOCK, 2 * BLOCK), 1) == 0
        for start in range(2):
            for h in range(N_Q_HEADS):
                kvh, g = divmod(h, Q_PER_KV)
                rows = pl.ds(g * BLOCK, BLOCK)
                bias_buf[start, kvh, rows, :] = jnp.where(
                    sink_slot, sinks_ref[layer, h] * LOG2E, bias_ref[start, kvh, rows, :])
        ubuf[0:HALO_U, :] = jnp.zeros((HALO_U, POOL_WIDTH), _F32)
        hbuf[0:HALO_H, :] = jnp.zeros((HALO_H, CONV_WIDTH), _F32)
        kbuf[:, 0:BLOCK, :] = jnp.zeros((N_KV_HEADS, BLOCK, HEAD_DIM), _BF16)
        zeros = jnp.zeros((TS + BLOCK, LANES), _BF16)
        lane_t = lax.broadcasted_iota(jnp.int32, (TS + BLOCK, LANES), 1)
        ones_lo = jnp.where(lane_t < HEAD_DIM, 1.0, 0.0).astype(_BF16)
        ones_hi = jnp.where(lane_t < HEAD_DIM, 0.0, 1.0).astype(_BF16)
        for kvh in range(N_KV_HEADS):
            ve_buf[kvh, :, 0:LANES] = zeros
            vo_buf[kvh, :, 0:LANES] = zeros
            ve_buf[kvh, :, LANES:2 * LANES] = ones_lo
            vo_buf[kvh, :, LANES:2 * LANES] = ones_hi

    @pl.when(jnp.logical_not(first_tile))
    def _():
        ubuf[0:HALO_U, :] = ubuf[TS:TS + HALO_U, :]
        hbuf[0:HALO_H, :] = hbuf[TS:TS + HALO_H, :]
        kbuf[:, 0:BLOCK, :] = kbuf[:, TS:TS + BLOCK, :]
        ve_buf[:, 0:BLOCK, 0:LANES] = ve_buf[:, TS:TS + BLOCK, 0:LANES]
        vo_buf[:, 0:BLOCK, 0:LANES] = vo_buf[:, TS:TS + BLOCK, 0:LANES]

    def norm_chunk(c):
        r = c * CHUNK
        xc = x_ref[0, pl.ds(r, CHUNK), :]
        ms = jnp.mean(xc * xc, axis=-1, keepdims=True)
        y = xc * lax.rsqrt(ms + EPS) * lng_ref[0]
        hn_ref[pl.ds(r, CHUNK), :] = y.astype(_BF16)

    _unrolled(TS // CHUNK, norm_chunk)

    def proj(c0, width, r0=0, rows=TS):
        return jnp.dot(hn_ref[pl.ds(r0, rows), :], win_ref[0, :, c0:c0 + width], preferred_element_type=_F32)

    for r0 in range(0, TS, TS // 2):
        rows = pl.ds(HALO_H + r0, TS // 2)
        hbuf[rows, :] = proj(C_CA, CONV_WIDTH, r0, TS // 2)
        hbuf[rows, :] = hbuf[rows, :] * jax.nn.sigmoid(proj(C_CB, CONV_WIDTH, r0, TS // 2))

    def proj_upool():
        ubuf[HALO_U:HALO_U + TS, :] = proj(C_UPOOL, POOL_WIDTH)

    def proj_gpool():
        gp_buf[...] = _silu(proj(C_GPOOL, POOL_WIDTH))

    def proj_gconv():
        gc_buf[...] = _silu(proj(C_GCONV, CONV_WIDTH))

    def proj_kv():
        kv = proj(C_K, 2 * KV_WIDTH)
        keys = kv[:, 0:KV_WIDTH]
        vals = kv[:, KV_WIDTH:2 * KV_WIDTH]
        vals_swapped = pltpu.roll(vals, HEAD_DIM, axis=1)
        lo = lax.broadcasted_iota(jnp.int32, (TS, LANES), 1) < HEAD_DIM
        for kvh in range(N_KV_HEADS):
            kbuf[kvh, BLOCK:BLOCK + TS, :] = keys[:, kvh * HEAD_DIM:(kvh + 1) * HEAD_DIM].astype(_BF16)
            own_lo, own_hi = (vals, vals_swapped) if kvh == 0 else (vals_swapped, vals)
            ve_buf[kvh, BLOCK:BLOCK + TS, 0:LANES] = jnp.where(lo, own_lo, 0.0).astype(_BF16)
            vo_buf[kvh, BLOCK:BLOCK + TS, 0:LANES] = jnp.where(lo, 0.0, own_hi).astype(_BF16)

    def proj_q(half):
        qh = proj(C_Q + half * 256, 256) * (HEAD_DIM ** -0.5 * LOG2E)
        for i in range(4):
            qbuf[half * 4 + i, :, :] = qh[:, i * HEAD_DIM:(i + 1) * HEAD_DIM].astype(_BF16)

    def proj_gattn(half):
        ga_buf[:, half * 256:(half + 1) * 256] = _silu(proj(C_GATTN + half * 256, 256))

    inv_w = [jnp.where(low_half, 1.0 / POOL_WINDOWS[0], 1.0 / POOL_WINDOWS[1]).astype(_F32),
             jnp.where(low_half, 1.0 / POOL_WINDOWS[2], 1.0 / POOL_WINDOWS[3]).astype(_F32)]

    def pool_chunk(c):
        r = c * CHUNK
        n_in = GROUPS + HALO_U // SUBLANES
        for col in range(2):
            lanes = slice(col * LANES, (col + 1) * LANES)
            u = [ubuf[pl.ds(r + SUBLANES * i, SUBLANES), lanes] for i in range(n_in)]
            s2 = [a + b for a, b in zip(u, _shift_down(u, 1, row))]
            s4 = [a + b for a, b in zip(s2, _shift_down(s2, 2, row))]
            if col == 0:
                wide, narrow = s4, s2
            else:
                s8 = [a + b for a, b in zip(s4, _shift_down(s4, 4, row))]
                s16 = [s8[0]] + [s8[i] + s8[i - 1] for i in range(1, n_in)]
                wide, narrow = s16, s8
            pieces = []
            for g in range(GROUPS):
                i = g + HALO_U // SUBLANES
                inv = inv_w[col]
                if c == 0 and g < HALO_U // SUBLANES:
                    tab = invcnt_ref[SUBLANES * g:SUBLANES * (g + 1), lanes]
                    inv = jnp.where(first_tile, tab, inv)
                pooled = jnp.where(low_half, narrow[i], wide[i]) * inv
                pieces.append(pooled - u[i])
            zp_buf[pl.ds(r, CHUNK), lanes] = jnp.concatenate(pieces, axis=0).astype(_BF16)

    taps_by_b = [[] for _ in range(SUBLANES)]
    for k in range(CONV_KERNEL):
        a, b = divmod(k + HALO_H - (CONV_KERNEL - 1), SUBLANES)
        taps_by_b[b].append((a, k))
    n_h = GROUPS + HALO_H // SUBLANES

    def conv_chunk(c):
        r = c * CHUNK
        outs = []
        for col in range(2):
            lanes = slice(col * LANES, (col + 1) * LANES)
            h = [hbuf[pl.ds(r + SUBLANES * m, SUBLANES), lanes] for m in range(n_h)]
            acc = None
            for b in range(SUBLANES):
                n_p = GROUPS if b == 0 else GROUPS + 1
                part = []
                for m in range(n_p):
                    p = None
                    for a, k in taps_by_b[b]:
                        term = h[m + a] * dw_ref[0, k:k + 1, lanes]
                        p = term if p is None else p + term
                    part.append(p)
                if b == 0:
                    acc = part
                else:
                    rolled = [pltpu.roll(p, SUBLANES - b, axis=0) for p in part]
                    keep = row < (SUBLANES - b)
                    acc = [acc[g] + jnp.where(keep, rolled[g], rolled[g + 1]) for g in range(GROUPS)]
            outs.append(jnp.concatenate(acc, axis=0))
        conv = jnp.concatenate(outs, axis=1) + cb_ref[0]
        mu = jnp.mean(conv, axis=-1, keepdims=True)
        cen = conv - mu
        var = jnp.mean(cen * cen, axis=-1, keepdims=True)
        z = cen * lax.rsqrt(var + EPS) * clg_ref[0] + clb_ref[0]
        zc_buf[pl.ds(r, CHUNK), :] = _silu(z).astype(_BF16)

    side_work = [proj_kv, functools.partial(proj_q, 0), functools.partial(proj_q, 1),
                 functools.partial(proj_gattn, 0), functools.partial(proj_gattn, 1),
                 proj_upool, proj_gpool, proj_gconv]
    chunks_per_projection = (TS // CHUNK) // len(side_work)
    for c in range(TS // CHUNK):
        if c % chunks_per_projection == 0:
            side_work[c // chunks_per_projection]()
        conv_chunk(c)
    _unrolled(TS // CHUNK, pool_chunk)

    n_local = POOL_WIDTH + CONV_WIDTH

    top_rows = lax.broadcasted_iota(jnp.int32, (16, 2 * LANES), 0)
    top_lanes = lax.broadcasted_iota(jnp.int32, (16, 2 * LANES), 1)
    sink_value_slot = jnp.logical_and(top_rows == 0, top_lanes < LANES)
    sink_key_slot = lax.broadcasted_iota(jnp.int32, (16, HEAD_DIM), 0) == 0

    def window_keys(kvh, r):
        top = kbuf[kvh, pl.ds(r, 16), :].astype(_F32)
        top = jnp.where(sink_key_slot, 0.0, top).astype(_BF16)
        return jnp.concatenate([top, kbuf[kvh, pl.ds(r + 16, 2 * BLOCK - 16), :]], axis=0)

    def scores(n):
        r = n * BLOCK
        seq_start = first_tile.astype(jnp.int32) if n == 0 else 0
        out = []
        for kvh in range(N_KV_HEADS):
            k2 = window_keys(kvh, r)
            q4 = jnp.concatenate([qbuf[kvh * Q_PER_KV + g, pl.ds(r, BLOCK), :] for g in range(Q_PER_KV)], axis=0)
            s = lax.dot_general(q4, k2, (((1,), (1,)), ((), ())), preferred_element_type=_F32)
            out.append(s + bias_buf[seq_start, kvh])
        return out

    def window_values(buf, kvh, r):
        top = buf[kvh, pl.ds(r, 16), :].astype(_F32)
        top = jnp.where(sink_value_slot, 0.0, top).astype(_BF16)
        return jnp.concatenate([top, buf[kvh, pl.ds(r + 16, 2 * BLOCK - 16), :]], axis=0)

    def softmax_pv(n, s_list):
        r = n * BLOCK
        for kvh in range(N_KV_HEADS):
            probs = []
            for g in range(Q_PER_KV):
                sg = s_list[kvh][g * BLOCK:(g + 1) * BLOCK]
                m = jnp.max(sg, axis=-1, keepdims=True)
                probs.append(jnp.exp2(sg - m).astype(_BF16))
            p_even = jnp.concatenate([probs[0], probs[2]], axis=0)
            p_odd = jnp.concatenate([probs[1], probs[3]], axis=0)
            o = (jnp.dot(p_even, window_values(ve_buf, kvh, r), preferred_element_type=_F32)
                 + jnp.dot(p_odd, window_values(vo_buf, kvh, r), preferred_element_type=_F32))
            res = o[:, 0:LANES] / o[:, LANES:2 * LANES]
            for pair in range(2):
                c0 = (kvh * 2 + pair) * LANES
                gate = ga_buf[pl.ds(r, BLOCK), c0:c0 + LANES]
                mix_ref[pl.ds(r, BLOCK), n_local + c0:n_local + c0 + LANES] = (
                    res[pair * BLOCK:(pair + 1) * BLOCK] * gate).astype(_BF16)

    def out_attn(r0, rows):
        for q in range(D_MODEL // 256):
            cols = slice(q * 256, (q + 1) * 256)
            o_ref[0, pl.ds(r0, rows), cols] = x_ref[0, pl.ds(r0, rows), cols] + jnp.dot(
                mix_ref[pl.ds(r0, rows), n_local:D_MIX], wout_ref[0, n_local:D_MIX, cols],
                preferred_element_type=_F32)

    s_cur = scores(0)
    for n in range(NB):
        s_next = scores(n + 1) if n + 1 < NB else None
        if n >= 2 and n % 2 == 0:
            out_attn((n - 2) * BLOCK, 2 * BLOCK)
        softmax_pv(n, s_cur)
        s_cur = s_next
    out_attn((NB - 2) * BLOCK, 2 * BLOCK)

    y_pool = jnp.dot(zp_buf[...], poolw_ref[0], preferred_element_type=_F32) * pscale_ref[0]
    mix_ref[:, 0:POOL_WIDTH] = (y_pool * gp_buf[...]).astype(_BF16)
    y_conv = jnp.dot(zc_buf[...], pw_ref[0], preferred_element_type=_F32)
    mix_ref[:, POOL_WIDTH:POOL_WIDTH + CONV_WIDTH] = (y_conv * gc_buf[...]).astype(_BF16)

    for q in range(D_MODEL // 256):
        cols = slice(q * 256, (q + 1) * 256)
        o_ref[0, :, cols] = o_ref[0, :, cols] + jnp.dot(
            mix_ref[:, 0:n_local], wout_ref[0, 0:n_local, cols], preferred_element_type=_F32)

    if final:
        def final_chunk(c):
            r = c * CHUNK
            xc = o_ref[0, pl.ds(r, CHUNK), :]
            ms = jnp.mean(xc * xc, axis=-1, keepdims=True)
            o_ref[0, pl.ds(r, CHUNK), :] = xc * lax.rsqrt(ms + EPS) * fg_ref[...]

        _unrolled(TS // CHUNK, final_chunk)


def _bias_table():
    i = np.arange(BLOCK)[:, None]
    jj = np.arange(2 * BLOCK)[None, :]
    dist = BLOCK + i - jj
    in_band = (dist >= 0) & (dist < WINDOW)
    slopes = np.asarray([2.0 ** (-8.0 * (h + 1) / N_Q_HEADS) for h in range(N_Q_HEADS)], dtype=np.float32)
    tab = np.empty((2, N_KV_HEADS, Q_PER_KV * BLOCK, 2 * BLOCK), np.float32)
    for start in range(2):
        valid = in_band & ~((jj < BLOCK) & (start == 1))
        for h in range(N_Q_HEADS):
            bias = (-slopes[h] * LOG2E) * dist.astype(np.float64)
            kvh, g = divmod(h, Q_PER_KV)
            tab[start, kvh, g * BLOCK:(g + 1) * BLOCK] = np.where(valid, bias, -1e30).astype(np.float32)
    return tab


def _inv_count_table():
    t = np.arange(HALO_U, dtype=np.float32)[:, None]
    gw = POOL_WIDTH // POOL_GROUPS
    wnd = np.repeat(np.asarray(POOL_WINDOWS, np.float32), gw)[None, :]
    return (1.0 / np.minimum(t + 1.0, wnd)).astype(np.float32)


def _layer_call(batch, seq, layer, final):
    n_tiles = seq // TS
    const = lambda shape: pl.BlockSpec(shape, lambda b, j: (0,) * len(shape), pipeline_mode=pl.Buffered(1))
    per_layer = lambda shape: pl.BlockSpec((1,) + shape, lambda b, j: (layer,) + (0,) * len(shape),
                                           pipeline_mode=pl.Buffered(1))
    in_specs = [
        pl.BlockSpec((1, TS, D_MODEL), lambda b, j: (b, j, 0)),
        per_layer((1, D_MODEL)),
        per_layer((D_MODEL, D_IN)),
        per_layer((POOL_WIDTH, POOL_WIDTH)),
        per_layer((1, POOL_WIDTH)),
        const((HALO_U, POOL_WIDTH)),
        per_layer((CONV_KERNEL, CONV_WIDTH)),
        per_layer((1, CONV_WIDTH)),
        per_layer((1, CONV_WIDTH)),
        per_layer((1, CONV_WIDTH)),
        per_layer((CONV_WIDTH, CONV_WIDTH)),
        pl.BlockSpec(memory_space=pltpu.SMEM),
        const((2, N_KV_HEADS, Q_PER_KV * BLOCK, 2 * BLOCK)),
        per_layer((D_MIX, D_MODEL)),
        const((1, D_MODEL)),
    ]
    scratch = [
        pltpu.VMEM((TS, D_MODEL), _BF16),
        pltpu.VMEM((TS + HALO_U, POOL_WIDTH), _F32),
        pltpu.VMEM((TS + HALO_H, CONV_WIDTH), _F32),
        pltpu.VMEM((TS, POOL_WIDTH), _F32),
        pltpu.VMEM((TS, CONV_WIDTH), _F32),
        pltpu.VMEM((TS, ATTN_WIDTH), _F32),
        pltpu.VMEM((TS, POOL_WIDTH), _BF16),
        pltpu.VMEM((TS, CONV_WIDTH), _BF16),
        pltpu.VMEM((N_KV_HEADS, TS + BLOCK, HEAD_DIM), _BF16),
        pltpu.VMEM((N_KV_HEADS, TS + BLOCK, 2 * LANES), _BF16),
        pltpu.VMEM((N_KV_HEADS, TS + BLOCK, 2 * LANES), _BF16),
        pltpu.VMEM((N_Q_HEADS, TS, HEAD_DIM), _BF16),
        pltpu.VMEM((TS, D_MIX), _BF16),
        pltpu.VMEM((2, N_KV_HEADS, Q_PER_KV * BLOCK, 2 * BLOCK), _F32),
    ]
    return pl.pallas_call(
        functools.partial(_layer_body, layer=layer, final=final),
        out_shape=jax.ShapeDtypeStruct((batch, seq, D_MODEL), _F32),
        grid=(batch, n_tiles),
        in_specs=in_specs,
        out_specs=pl.BlockSpec((1, TS, D_MODEL), lambda b, j: (b, j, 0)),
        scratch_shapes=scratch,
        compiler_params=pltpu.CompilerParams(
            dimension_semantics=("arbitrary", "arbitrary"),
            vmem_limit_bytes=VMEM_LIMIT_BYTES,
        ),
        name="hybrid_layer_final" if final else "hybrid_layer",
    )


def kernel(x, ln_g, w_in, pool_w, pool_scale, conv_dw, conv_b, conv_ln_g, conv_ln_b, conv_pw, attn_sinks, w_out, final_g):
    batch, seq, d_model = x.shape
    assert d_model == D_MODEL and seq % TS == 0 and TS % BLOCK == 0
    bias = jnp.asarray(_bias_table())
    inv_cnt = jnp.asarray(_inv_count_table())
    rows = lambda v: v.reshape(DEPTH, 1, -1)
    poolw_bd = jnp.stack([jax.scipy.linalg.block_diag(*[pool_w[l, g] for g in range(POOL_GROUPS)])
                          for l in range(DEPTH)]).astype(_BF16)
    params = (rows(ln_g), w_in.astype(_BF16), poolw_bd, rows(pool_scale), inv_cnt, conv_dw, rows(conv_b),
              rows(conv_ln_g), rows(conv_ln_b), conv_pw.astype(_BF16), attn_sinks, bias, w_out.astype(_BF16),
              final_g.reshape(1, -1))
    for l in range(DEPTH):
        x = _layer_call(batch, seq, l, final=(l == DEPTH - 1))(x, *params)
    return x
```

```python
import functools

import numpy as np
import jax
import jax.numpy as jnp
from jax import lax
from jax.experimental import pallas as pl
from jax.experimental.pallas import tpu as pltpu

D_MODEL = 1024
DEPTH = 2
POOL_WIDTH = 256
POOL_GROUPS = 4
POOL_WINDOWS = (2, 4, 8, 16)
CONV_WIDTH = 256
CONV_KERNEL = 31
HEAD_DIM = 64
N_Q_HEADS = 8
N_KV_HEADS = 2
Q_PER_KV = N_Q_HEADS // N_KV_HEADS
ATTN_WIDTH = N_Q_HEADS * HEAD_DIM
KV_WIDTH = N_KV_HEADS * HEAD_DIM
WINDOW = 128
BLOCK = 128
D_MIX = POOL_WIDTH + CONV_WIDTH + ATTN_WIDTH
D_IN = 2 * POOL_WIDTH + 3 * CONV_WIDTH + 2 * ATTN_WIDTH + 2 * KV_WIDTH
EPS = 1e-6

C_UPOOL = 0
C_GPOOL = C_UPOOL + POOL_WIDTH
C_CA = C_GPOOL + POOL_WIDTH
C_CB = C_CA + CONV_WIDTH
C_GCONV = C_CB + CONV_WIDTH
C_Q = C_GCONV + CONV_WIDTH
C_K = C_Q + ATTN_WIDTH
C_V = C_K + KV_WIDTH
C_GATTN = C_V + KV_WIDTH

SUBLANES = 8
LANES = 128
TS = 1024
NB = TS // BLOCK
CHUNK = 64
GROUPS = CHUNK // SUBLANES
HALO_U = 16
HALO_H = 32
VMEM_LIMIT_BYTES = 60 * 1024 * 1024
LOG2E = 1.4426950408889634

_F32 = jnp.float32
_BF16 = jnp.bfloat16


def _silu(v):
    return v * jax.nn.sigmoid(v)


def _unrolled(n, body):
    for c in range(n):
        body(c)


def _shift_down(groups, d, row):
    rolled = [pltpu.roll(g, d, axis=0) for g in groups]
    take_prev = row < d
    out = [rolled[0]]
    for i in range(1, len(groups)):
        out.append(jnp.where(take_prev, rolled[i - 1], rolled[i]))
    return out


def _layer_body(x_ref, lng_ref, win_ref, poolw_ref, pscale_ref, invcnt_ref, dw_ref, cb_ref, clg_ref, clb_ref,
                pw_ref, sinks_ref, bias_ref, wout_ref, fg_ref, o_ref,
                hn_ref, ubuf, hbuf, gp_buf, gc_buf, ga_buf, zp_buf, zc_buf, kbuf, ve_buf, vo_buf, qbuf, mix_ref,
                bias_buf, *, layer, final):
    j = pl.program_id(1)
    first_tile = j == 0

    row = lax.broadcasted_iota(jnp.int32, (SUBLANES, LANES), 0)
    lane = lax.broadcasted_iota(jnp.int32, (SUBLANES, LANES), 1)
    low_half = lane < HEAD_DIM

    @pl.when(first_tile)
    def _():
        sink_slot = lax.broadcasted_iota(jnp.int32, (BLOCK, 2 * BLOCK), 1) == 0
        for start in range(2):
            for h in range(N_Q_HEADS):
                kvh, g = divmod(h, Q_PER_KV)
                rows = pl.ds(g * BLOCK, BLOCK)
                bias_buf[start, kvh, rows, :] = jnp.where(
                    sink_slot, sinks_ref[layer, h] * LOG2E, bias_ref[start, kvh, rows, :])
        ubuf[0:HALO_U, :] = jnp.zeros((HALO_U, POOL_WIDTH), _F32)
        hbuf[0:HALO_H, :] = jnp.zeros((HALO_H, CONV_WIDTH), _F32)
        kbuf[:, 0:BLOCK, :] = jnp.zeros((N_KV_HEADS, BLOCK, HEAD_DIM), _BF16)
        zeros = jnp.zeros((TS + BLOCK, LANES), _BF16)
        lane_t = lax.broadcasted_iota(jnp.int32, (TS + BLOCK, LANES), 1)
        ones_lo = jnp.where(lane_t < HEAD_DIM, 1.0, 0.0).astype(_BF16)
        ones_hi = jnp.where(lane_t < HEAD_DIM, 0.0, 1.0).astype(_BF16)
        for kvh in range(N_KV_HEADS):
            ve_buf[kvh, :, 0:LANES] = zeros
            vo_buf[kvh, :, 0:LANES] = zeros
            ve_buf[kvh, :, LANES:2 * LANES] = ones_lo
            vo_buf[kvh, :, LANES:2 * LANES] = ones_hi

    @pl.when(jnp.logical_not(first_tile))
    def _():
        ubuf[0:HALO_U, :] = ubuf[TS:TS + HALO_U, :]
        hbuf[0:HALO_H, :] = hbuf[TS:TS + HALO_H, :]
        kbuf[:, 0:BLOCK, :] = kbuf[:, TS:TS + BLOCK, :]
        ve_buf[:, 0:BLOCK, 0:LANES] = ve_buf[:, TS:TS + BLOCK, 0:LANES]
        vo_buf[:, 0:BLOCK, 0:LANES] = vo_buf[:, TS:TS + BLOCK, 0:LANES]

    def norm_chunk(c):
        r = c * CHUNK
        xc = x_ref[0, pl.ds(r, CHUNK), :]
        ms = jnp.mean(xc * xc, axis=-1, keepdims=True)
        y = xc * lax.rsqrt(ms + EPS) * lng_ref[layer:layer + 1, :]
        hn_ref[pl.ds(r, CHUNK), :] = y.astype(_BF16)

    _unrolled(TS // CHUNK, norm_chunk)

    def proj(c0, width, r0=0, rows=TS):
        w = win_ref[0, :, c0:c0 + width].astype(_BF16)
        return jnp.dot(hn_ref[pl.ds(r0, rows), :], w, preferred_element_type=_F32)

    for r0 in range(0, TS, TS // 2):
        rows = pl.ds(HALO_H + r0, TS // 2)
        hbuf[rows, :] = proj(C_CA, CONV_WIDTH, r0, TS // 2)
        hbuf[rows, :] = hbuf[rows, :] * jax.nn.sigmoid(proj(C_CB, CONV_WIDTH, r0, TS // 2))

    def proj_upool():
        ubuf[HALO_U:HALO_U + TS, :] = proj(C_UPOOL, POOL_WIDTH)

    def proj_gpool():
        gp_buf[...] = _silu(proj(C_GPOOL, POOL_WIDTH))

    def proj_gconv():
        gc_buf[...] = _silu(proj(C_GCONV, CONV_WIDTH))

    def proj_kv():
        kv = proj(C_K, 2 * KV_WIDTH)
        keys = kv[:, 0:KV_WIDTH]
        vals = kv[:, KV_WIDTH:2 * KV_WIDTH]
        vals_swapped = pltpu.roll(vals, HEAD_DIM, axis=1)
        lo = lax.broadcasted_iota(jnp.int32, (TS, LANES), 1) < HEAD_DIM
        for kvh in range(N_KV_HEADS):
            kbuf[kvh, BLOCK:BLOCK + TS, :] = keys[:, kvh * HEAD_DIM:(kvh + 1) * HEAD_DIM].astype(_BF16)
            own_lo, own_hi = (vals, vals_swapped) if kvh == 0 else (vals_swapped, vals)
            ve_buf[kvh, BLOCK:BLOCK + TS, 0:LANES] = jnp.where(lo, own_lo, 0.0).astype(_BF16)
            vo_buf[kvh, BLOCK:BLOCK + TS, 0:LANES] = jnp.where(lo, 0.0, own_hi).astype(_BF16)

    def proj_q(half):
        qh = proj(C_Q + half * 256, 256) * (HEAD_DIM ** -0.5 * LOG2E)
        for i in range(4):
            qbuf[half * 4 + i, :, :] = qh[:, i * HEAD_DIM:(i + 1) * HEAD_DIM].astype(_BF16)

    def proj_gattn(half):
        ga_buf[:, half * 256:(half + 1) * 256] = _silu(proj(C_GATTN + half * 256, 256))

    inv_w = [jnp.where(low_half, 1.0 / POOL_WINDOWS[0], 1.0 / POOL_WINDOWS[1]).astype(_F32),
             jnp.where(low_half, 1.0 / POOL_WINDOWS[2], 1.0 / POOL_WINDOWS[3]).astype(_F32)]

    def pool_chunk(c):
        r = c * CHUNK
        n_in = GROUPS + HALO_U // SUBLANES
        for col in range(2):
            lanes = slice(col * LANES, (col + 1) * LANES)
            u = [ubuf[pl.ds(r + SUBLANES * i, SUBLANES), lanes] for i in range(n_in)]
            s2 = [a + b for a, b in zip(u, _shift_down(u, 1, row))]
            s4 = [a + b for a, b in zip(s2, _shift_down(s2, 2, row))]
            if col == 0:
                wide, narrow = s4, s2
            else:
                s8 = [a + b for a, b in zip(s4, _shift_down(s4, 4, row))]
                s16 = [s8[0]] + [s8[i] + s8[i - 1] for i in range(1, n_in)]
                wide, narrow = s16, s8
            pieces = []
            for g in range(GROUPS):
                i = g + HALO_U // SUBLANES
                inv = inv_w[col]
                if c == 0 and g < HALO_U // SUBLANES:
                    tab = invcnt_ref[SUBLANES * g:SUBLANES * (g + 1), lanes]
                    inv = jnp.where(first_tile, tab, inv)
                pooled = jnp.where(low_half, narrow[i], wide[i]) * inv
                pieces.append(pooled - u[i])
            zp_buf[pl.ds(r, CHUNK), lanes] = jnp.concatenate(pieces, axis=0).astype(_BF16)

    taps_by_b = [[] for _ in range(SUBLANES)]
    for k in range(CONV_KERNEL):
        a, b = divmod(k + HALO_H - (CONV_KERNEL - 1), SUBLANES)
        taps_by_b[b].append((a, k))
    n_h = GROUPS + HALO_H // SUBLANES

    def conv_chunk(c):
        r = c * CHUNK
        outs = []
        for col in range(2):
            lanes = slice(col * LANES, (col + 1) * LANES)
            h = [hbuf[pl.ds(r + SUBLANES * m, SUBLANES), lanes] for m in range(n_h)]
            acc = None
            for b in range(SUBLANES):
                n_p = GROUPS if b == 0 else GROUPS + 1
                part = []
                for m in range(n_p):
                    p = None
                    for a, k in taps_by_b[b]:
                        term = h[m + a] * dw_ref[0, k:k + 1, lanes]
                        p = term if p is None else p + term
                    part.append(p)
                if b == 0:
                    acc = part
                else:
                    rolled = [pltpu.roll(p, SUBLANES - b, axis=0) for p in part]
                    keep = row < (SUBLANES - b)
                    acc = [acc[g] + jnp.where(keep, rolled[g], rolled[g + 1]) for g in range(GROUPS)]
            outs.append(jnp.concatenate(acc, axis=0))
        conv = jnp.concatenate(outs, axis=1) + cb_ref[layer:layer + 1, :]
        mu = jnp.mean(conv, axis=-1, keepdims=True)
        cen = conv - mu
        var = jnp.mean(cen * cen, axis=-1, keepdims=True)
        z = cen * lax.rsqrt(var + EPS) * clg_ref[layer:layer + 1, :] + clb_ref[layer:layer + 1, :]
        zc_buf[pl.ds(r, CHUNK), :] = _silu(z).astype(_BF16)

    side_work = [proj_kv, functools.partial(proj_q, 0), functools.partial(proj_q, 1),
                 functools.partial(proj_gattn, 0), functools.partial(proj_gattn, 1),
                 proj_upool, proj_gpool, proj_gconv]
    chunks_per_projection = (TS // CHUNK) // len(side_work)
    for c in range(TS // CHUNK):
        if c % chunks_per_projection == 0:
            side_work[c // chunks_per_projection]()
        conv_chunk(c)
    _unrolled(TS // CHUNK, pool_chunk)

    n_local = POOL_WIDTH + CONV_WIDTH

    top_rows = lax.broadcasted_iota(jnp.int32, (16, 2 * LANES), 0)
    top_lanes = lax.broadcasted_iota(jnp.int32, (16, 2 * LANES), 1)
    sink_value_slot = jnp.logical_and(top_rows == 0, top_lanes < LANES)
    sink_key_slot = lax.broadcasted_iota(jnp.int32, (16, HEAD_DIM), 0) == 0

    def window_keys(kvh, r):
        top = kbuf[kvh, pl.ds(r, 16), :].astype(_F32)
        top = jnp.where(sink_key_slot, 0.0, top).astype(_BF16)
        return jnp.concatenate([top, kbuf[kvh, pl.ds(r + 16, 2 * BLOCK - 16), :]], axis=0)

    def scores(n):
        r = n * BLOCK
        seq_start = first_tile.astype(jnp.int32) if n == 0 else 0
        out = []
        for kvh in range(N_KV_HEADS):
            k2 = window_keys(kvh, r)
            q4 = jnp.concatenate([qbuf[kvh * Q_PER_KV + g, pl.ds(r, BLOCK), :] for g in range(Q_PER_KV)], axis=0)
            s = lax.dot_general(q4, k2, (((1,), (1,)), ((), ())), preferred_element_type=_F32)
            out.append(s + bias_buf[seq_start, kvh])
        return out

    def window_values(buf, kvh, r):
        top = buf[kvh, pl.ds(r, 16), :].astype(_F32)
        top = jnp.where(sink_value_slot, 0.0, top).astype(_BF16)
        return jnp.concatenate([top, buf[kvh, pl.ds(r + 16, 2 * BLOCK - 16), :]], axis=0)

    def softmax_pv(n, s_list):
        r = n * BLOCK
        for kvh in range(N_KV_HEADS):
            probs = []
            for g in range(Q_PER_KV):
                sg = s_list[kvh][g * BLOCK:(g + 1) * BLOCK]
                m = jnp.max(sg, axis=-1, keepdims=True)
                probs.append(jnp.exp2(sg - m).astype(_BF16))
            p_even = jnp.concatenate([probs[0], probs[2]], axis=0)
            p_odd = jnp.concatenate([probs[1], probs[3]], axis=0)
            o = (jnp.dot(p_even, window_values(ve_buf, kvh, r), preferred_element_type=_F32)
                 + jnp.dot(p_odd, window_values(vo_buf, kvh, r), preferred_element_type=_F32))
            res = o[:, 0:LANES] / o[:, LANES:2 * LANES]
            for pair in range(2):
                c0 = (kvh * 2 + pair) * LANES
                gate = ga_buf[pl.ds(r, BLOCK), c0:c0 + LANES]
                mix_ref[pl.ds(r, BLOCK), n_local + c0:n_local + c0 + LANES] = (
                    res[pair * BLOCK:(pair + 1) * BLOCK] * gate).astype(_BF16)

    def out_attn(r0, rows):
        for q in range(D_MODEL // 256):
            cols = slice(q * 256, (q + 1) * 256)
            o_ref[0, pl.ds(r0, rows), cols] = x_ref[0, pl.ds(r0, rows), cols] + jnp.dot(
                mix_ref[pl.ds(r0, rows), n_local:D_MIX], wout_ref[0, n_local:D_MIX, cols].astype(_BF16),
                preferred_element_type=_F32)

    s_cur = scores(0)
    for n in range(NB):
        s_next = scores(n + 1) if n + 1 < NB else None
        if n >= 2 and n % 2 == 0:
            out_attn((n - 2) * BLOCK, 2 * BLOCK)
        softmax_pv(n, s_cur)
        s_cur = s_next
    out_attn((NB - 2) * BLOCK, 2 * BLOCK)

    y_pool = jnp.dot(zp_buf[...], poolw_ref[0].astype(_BF16), preferred_element_type=_F32)
    y_pool = y_pool * pscale_ref[layer:layer + 1, :]
    mix_ref[:, 0:POOL_WIDTH] = (y_pool * gp_buf[...]).astype(_BF16)
    y_conv = jnp.dot(zc_buf[...], pw_ref[0].astype(_BF16), preferred_element_type=_F32)
    mix_ref[:, POOL_WIDTH:POOL_WIDTH + CONV_WIDTH] = (y_conv * gc_buf[...]).astype(_BF16)

    for q in range(D_MODEL // 256):
        cols = slice(q * 256, (q + 1) * 256)
        o_ref[0, :, cols] = o_ref[0, :, cols] + jnp.dot(
            mix_ref[:, 0:n_local], wout_ref[0, 0:n_local, cols].astype(_BF16), preferred_element_type=_F32)

    if final:
        def final_chunk(c):
            r = c * CHUNK
            xc = o_ref[0, pl.ds(r, CHUNK), :]
            ms = jnp.mean(xc * xc, axis=-1, keepdims=True)
            o_ref[0, pl.ds(r, CHUNK), :] = xc * lax.rsqrt(ms + EPS) * fg_ref[...]

        _unrolled(TS // CHUNK, final_chunk)


def _bias_table():
    i = np.arange(BLOCK)[:, None]
    jj = np.arange(2 * BLOCK)[None, :]
    dist = BLOCK + i - jj
    in_band = (dist >= 0) & (dist < WINDOW)
    slopes = np.asarray([2.0 ** (-8.0 * (h + 1) / N_Q_HEADS) for h in range(N_Q_HEADS)], dtype=np.float32)
    tab = np.empty((2, N_KV_HEADS, Q_PER_KV * BLOCK, 2 * BLOCK), np.float32)
    for start in range(2):
        valid = in_band & ~((jj < BLOCK) & (start == 1))
        for h in range(N_Q_HEADS):
            bias = (-slopes[h] * LOG2E) * dist.astype(np.float64)
            kvh, g = divmod(h, Q_PER_KV)
            tab[start, kvh, g * BLOCK:(g + 1) * BLOCK] = np.where(valid, bias, -1e30).astype(np.float32)
    return tab


def _inv_count_table():
    t = np.arange(HALO_U, dtype=np.float32)[:, None]
    gw = POOL_WIDTH // POOL_GROUPS
    wnd = np.repeat(np.asarray(POOL_WINDOWS, np.float32), gw)[None, :]
    return (1.0 / np.minimum(t + 1.0, wnd)).astype(np.float32)


def _layer_call(batch, seq, layer, final):
    n_tiles = seq // TS
    const = lambda shape: pl.BlockSpec(shape, lambda b, j: (0,) * len(shape), pipeline_mode=pl.Buffered(1))
    per_layer = lambda shape: pl.BlockSpec((1,) + shape, lambda b, j: (layer,) + (0,) * len(shape),
                                           pipeline_mode=pl.Buffered(1))
    in_specs = [
        pl.BlockSpec((1, TS, D_MODEL), lambda b, j: (b, j, 0)),
        const((DEPTH, D_MODEL)),
        per_layer((D_MODEL, D_IN)),
        per_layer((POOL_WIDTH, POOL_WIDTH)),
        const((DEPTH, POOL_WIDTH)),
        const((HALO_U, POOL_WIDTH)),
        per_layer((CONV_KERNEL, CONV_WIDTH)),
        const((DEPTH, CONV_WIDTH)),
        const((DEPTH, CONV_WIDTH)),
        const((DEPTH, CONV_WIDTH)),
        per_layer((CONV_WIDTH, CONV_WIDTH)),
        pl.BlockSpec(memory_space=pltpu.SMEM),
        const((2, N_KV_HEADS, Q_PER_KV * BLOCK, 2 * BLOCK)),
        per_layer((D_MIX, D_MODEL)),
        const((1, D_MODEL)),
    ]
    scratch = [
        pltpu.VMEM((TS, D_MODEL), _BF16),
        pltpu.VMEM((TS + HALO_U, POOL_WIDTH), _F32),
        pltpu.VMEM((TS + HALO_H, CONV_WIDTH), _F32),
        pltpu.VMEM((TS, POOL_WIDTH), _F32),
        pltpu.VMEM((TS, CONV_WIDTH), _F32),
        pltpu.VMEM((TS, ATTN_WIDTH), _F32),
        pltpu.VMEM((TS, POOL_WIDTH), _BF16),
        pltpu.VMEM((TS, CONV_WIDTH), _BF16),
        pltpu.VMEM((N_KV_HEADS, TS + BLOCK, HEAD_DIM), _BF16),
        pltpu.VMEM((N_KV_HEADS, TS + BLOCK, 2 * LANES), _BF16),
        pltpu.VMEM((N_KV_HEADS, TS + BLOCK, 2 * LANES), _BF16),
        pltpu.VMEM((N_Q_HEADS, TS, HEAD_DIM), _BF16),
        pltpu.VMEM((TS, D_MIX), _BF16),
        pltpu.VMEM((2, N_KV_HEADS, Q_PER_KV * BLOCK, 2 * BLOCK), _F32),
    ]
    return pl.pallas_call(
        functools.partial(_layer_body, layer=layer, final=final),
        out_shape=jax.ShapeDtypeStruct((batch, seq, D_MODEL), _F32),
        grid=(batch, n_tiles),
        in_specs=in_specs,
        out_specs=pl.BlockSpec((1, TS, D_MODEL), lambda b, j: (b, j, 0)),
        scratch_shapes=scratch,
        compiler_params=pltpu.CompilerParams(
            dimension_semantics=("arbitrary", "arbitrary"),
            vmem_limit_bytes=VMEM_LIMIT_BYTES,
        ),
        name="hybrid_layer_final" if final else "hybrid_layer",
    )


def kernel(x, ln_g, w_in, pool_w, pool_scale, conv_dw, conv_b, conv_ln_g, conv_ln_b, conv_pw, attn_sinks, w_out, final_g):
    batch, seq, d_model = x.shape
    assert d_model == D_MODEL and seq % TS == 0 and TS % BLOCK == 0
    bias = jnp.asarray(_bias_table())
    inv_cnt = jnp.asarray(_inv_count_table())
    poolw_bd = jnp.stack([jax.scipy.linalg.block_diag(*[pool_w[l, g] for g in range(POOL_GROUPS)])
                          for l in range(DEPTH)])
    params = (ln_g, w_in, poolw_bd, pool_scale, inv_cnt, conv_dw, conv_b, conv_ln_g, conv_ln_b, conv_pw,
              attn_sinks, bias, w_out, final_g.reshape(1, -1))
    for l in range(DEPTH):
        x = _layer_call(batch, seq, l, final=(l == DEPTH - 1))(x, *params)
    return x
```

```python
import functools

import numpy as np
import jax
import jax.numpy as jnp
from jax import lax
from jax.experimental import pallas as pl
from jax.experimental.pallas import tpu as pltpu

D_MODEL = 1024
DEPTH = 2
POOL_WIDTH = 256
POOL_GROUPS = 4
POOL_WINDOWS = (2, 4, 8, 16)
CONV_WIDTH = 256
CONV_KERNEL = 31
HEAD_DIM = 64
N_Q_HEADS = 8
N_KV_HEADS = 2
Q_PER_KV = N_Q_HEADS // N_KV_HEADS
ATTN_WIDTH = N_Q_HEADS * HEAD_DIM
KV_WIDTH = N_KV_HEADS * HEAD_DIM
WINDOW = 128
BLOCK = 128
D_MIX = POOL_WIDTH + CONV_WIDTH + ATTN_WIDTH
D_IN = 2 * POOL_WIDTH + 3 * CONV_WIDTH + 2 * ATTN_WIDTH + 2 * KV_WIDTH
EPS = 1e-6

C_UPOOL = 0
C_GPOOL = C_UPOOL + POOL_WIDTH
C_CA = C_GPOOL + POOL_WIDTH
C_CB = C_CA + CONV_WIDTH
C_GCONV = C_CB + CONV_WIDTH
C_Q = C_GCONV + CONV_WIDTH
C_K = C_Q + ATTN_WIDTH
C_V = C_K + KV_WIDTH
C_GATTN = C_V + KV_WIDTH

SUBLANES = 8
LANES = 128
BF16_ROWS = 16
MXU_COLS = 256
TS = 1024
NB = TS // BLOCK
CHUNK = 64
GROUPS = CHUNK // SUBLANES
HALO_U = 16
HALO_H = 32
VMEM_LIMIT_BYTES = 60 * 1024 * 1024
LOG2E = 1.4426950408889634

_F32 = jnp.float32
_BF16 = jnp.bfloat16


def _silu(v):
    return v * jax.nn.sigmoid(v)


def _unrolled(n, body):
    for c in range(n):
        body(c)


def _shift_down(groups, d, row):
    rolled = [pltpu.roll(g, d, axis=0) for g in groups]
    take_prev = row < d
    out = [rolled[0]]
    for i in range(1, len(groups)):
        out.append(jnp.where(take_prev, rolled[i - 1], rolled[i]))
    return out


def _layer_body(x_ref, lng_ref, win_ref, poolw_ref, pscale_ref, invcnt_ref, dw_ref, cb_ref, clg_ref, clb_ref,
                pw_ref, sinks_ref, bias_ref, wout_ref, fg_ref, o_ref,
                hn_ref, ubuf, hbuf, gp_buf, gc_buf, ga_buf, zp_buf, zc_buf, kbuf, ve_buf, vo_buf, qbuf, mix_ref,
                bias_buf, *, layer, final):
    j = pl.program_id(1)
    first_tile = j == 0

    row = lax.broadcasted_iota(jnp.int32, (SUBLANES, LANES), 0)
    lane = lax.broadcasted_iota(jnp.int32, (SUBLANES, LANES), 1)
    low_half = lane < HEAD_DIM

    @pl.when(first_tile)
    def _():
        sink_slot = lax.broadcasted_iota(jnp.int32, (BLOCK, 2 * BLOCK), 1) == 0
        for start in range(2):
            for h in range(N_Q_HEADS):
                kvh, g = divmod(h, Q_PER_KV)
                rows = pl.ds(g * BLOCK, BLOCK)
                bias_buf[start, kvh, rows, :] = jnp.where(
                    sink_slot, sinks_ref[layer, h] * LOG2E, bias_ref[start, kvh, rows, :])
        ubuf[0:HALO_U, :] = jnp.zeros((HALO_U, POOL_WIDTH), _F32)
        hbuf[0:HALO_H, :] = jnp.zeros((HALO_H, CONV_WIDTH), _F32)
        kbuf[:, 0:BLOCK, :] = jnp.zeros((N_KV_HEADS, BLOCK, HEAD_DIM), _BF16)
        zeros = jnp.zeros((TS + BLOCK, LANES), _BF16)
        lane_t = lax.broadcasted_iota(jnp.int32, (TS + BLOCK, LANES), 1)
        ones_lo = jnp.where(lane_t < HEAD_DIM, 1.0, 0.0).astype(_BF16)
        ones_hi = jnp.where(lane_t < HEAD_DIM, 0.0, 1.0).astype(_BF16)
        for kvh in range(N_KV_HEADS):
            ve_buf[kvh, :, 0:LANES] = zeros
            vo_buf[kvh, :, 0:LANES] = zeros
            ve_buf[kvh, :, LANES:2 * LANES] = ones_lo
            vo_buf[kvh, :, LANES:2 * LANES] = ones_hi

    @pl.when(jnp.logical_not(first_tile))
    def _():
        ubuf[0:HALO_U, :] = ubuf[TS:TS + HALO_U, :]
        hbuf[0:HALO_H, :] = hbuf[TS:TS + HALO_H, :]
        kbuf[:, 0:BLOCK, :] = kbuf[:, TS:TS + BLOCK, :]
        ve_buf[:, 0:BLOCK, 0:LANES] = ve_buf[:, TS:TS + BLOCK, 0:LANES]
        vo_buf[:, 0:BLOCK, 0:LANES] = vo_buf[:, TS:TS + BLOCK, 0:LANES]

    def norm_chunk(c):
        r = c * CHUNK
        xc = x_ref[0, pl.ds(r, CHUNK), :]
        ms = jnp.mean(xc * xc, axis=-1, keepdims=True)
        y = xc * lax.rsqrt(ms + EPS) * lng_ref[layer:layer + 1, :]
        hn_ref[pl.ds(r, CHUNK), :] = y.astype(_BF16)

    _unrolled(TS // CHUNK, norm_chunk)

    def proj(c0, width, r0=0, rows=TS):
        w = win_ref[0, :, c0:c0 + width].astype(_BF16)
        return jnp.dot(hn_ref[pl.ds(r0, rows), :], w, preferred_element_type=_F32)

    for r0 in range(0, TS, TS // 2):
        rows = pl.ds(HALO_H + r0, TS // 2)
        hbuf[rows, :] = proj(C_CA, CONV_WIDTH, r0, TS // 2)
        hbuf[rows, :] = hbuf[rows, :] * jax.nn.sigmoid(proj(C_CB, CONV_WIDTH, r0, TS // 2))

    def proj_upool():
        ubuf[HALO_U:HALO_U + TS, :] = proj(C_UPOOL, POOL_WIDTH)

    def proj_gpool():
        gp_buf[...] = _silu(proj(C_GPOOL, POOL_WIDTH))

    def proj_gconv():
        gc_buf[...] = _silu(proj(C_GCONV, CONV_WIDTH))

    def proj_kv():
        kv = proj(C_K, 2 * KV_WIDTH)
        keys = kv[:, 0:KV_WIDTH]
        vals = kv[:, KV_WIDTH:2 * KV_WIDTH]
        vals_swapped = pltpu.roll(vals, HEAD_DIM, axis=1)
        lo = lax.broadcasted_iota(jnp.int32, (TS, LANES), 1) < HEAD_DIM
        for kvh in range(N_KV_HEADS):
            kbuf[kvh, BLOCK:BLOCK + TS, :] = keys[:, kvh * HEAD_DIM:(kvh + 1) * HEAD_DIM].astype(_BF16)
            own_lo, own_hi = (vals, vals_swapped) if kvh == 0 else (vals_swapped, vals)
            ve_buf[kvh, BLOCK:BLOCK + TS, 0:LANES] = jnp.where(lo, own_lo, 0.0).astype(_BF16)
            vo_buf[kvh, BLOCK:BLOCK + TS, 0:LANES] = jnp.where(lo, 0.0, own_hi).astype(_BF16)

    def proj_q(half):
        heads = MXU_COLS // HEAD_DIM
        qh = proj(C_Q + half * MXU_COLS, MXU_COLS) * (HEAD_DIM ** -0.5 * LOG2E)
        for i in range(heads):
            qbuf[half * heads + i, :, :] = qh[:, i * HEAD_DIM:(i + 1) * HEAD_DIM].astype(_BF16)

    def proj_gattn(half):
        ga_buf[:, half * MXU_COLS:(half + 1) * MXU_COLS] = _silu(proj(C_GATTN + half * MXU_COLS, MXU_COLS))

    inv_w = [jnp.where(low_half, 1.0 / POOL_WINDOWS[0], 1.0 / POOL_WINDOWS[1]).astype(_F32),
             jnp.where(low_half, 1.0 / POOL_WINDOWS[2], 1.0 / POOL_WINDOWS[3]).astype(_F32)]

    def pool_chunk(c):
        r = c * CHUNK
        n_in = GROUPS + HALO_U // SUBLANES
        for col in range(2):
            lanes = slice(col * LANES, (col + 1) * LANES)
            u = [ubuf[pl.ds(r + SUBLANES * i, SUBLANES), lanes] for i in range(n_in)]
            s2 = [a + b for a, b in zip(u, _shift_down(u, 1, row))]
            s4 = [a + b for a, b in zip(s2, _shift_down(s2, 2, row))]
            if col == 0:
                wide, narrow = s4, s2
            else:
                s8 = [a + b for a, b in zip(s4, _shift_down(s4, 4, row))]
                s16 = [s8[0]] + [s8[i] + s8[i - 1] for i in range(1, n_in)]
                wide, narrow = s16, s8
            pieces = []
            for g in range(GROUPS):
                i = g + HALO_U // SUBLANES
                inv = inv_w[col]
                if c == 0 and g < HALO_U // SUBLANES:
                    tab = invcnt_ref[SUBLANES * g:SUBLANES * (g + 1), lanes]
                    inv = jnp.where(first_tile, tab, inv)
                pooled = jnp.where(low_half, narrow[i], wide[i]) * inv
                pieces.append(pooled - u[i])
            zp_buf[pl.ds(r, CHUNK), lanes] = jnp.concatenate(pieces, axis=0).astype(_BF16)

    taps_by_b = [[] for _ in range(SUBLANES)]
    for k in range(CONV_KERNEL):
        a, b = divmod(k + HALO_H - (CONV_KERNEL - 1), SUBLANES)
        taps_by_b[b].append((a, k))
    n_h = GROUPS + HALO_H // SUBLANES

    def conv_chunk(c):
        r = c * CHUNK
        outs = []
        for col in range(2):
            lanes = slice(col * LANES, (col + 1) * LANES)
            h = [hbuf[pl.ds(r + SUBLANES * m, SUBLANES), lanes] for m in range(n_h)]
            acc = None
            for b in range(SUBLANES):
                n_p = GROUPS if b == 0 else GROUPS + 1
                part = []
                for m in range(n_p):
                    p = None
                    for a, k in taps_by_b[b]:
                        term = h[m + a] * dw_ref[0, k:k + 1, lanes]
                        p = term if p is None else p + term
                    part.append(p)
                if b == 0:
                    acc = part
                else:
                    rolled = [pltpu.roll(p, SUBLANES - b, axis=0) for p in part]
                    keep = row < (SUBLANES - b)
                    acc = [acc[g] + jnp.where(keep, rolled[g], rolled[g + 1]) for g in range(GROUPS)]
            outs.append(jnp.concatenate(acc, axis=0))
        conv = jnp.concatenate(outs, axis=1) + cb_ref[layer:layer + 1, :]
        mu = jnp.mean(conv, axis=-1, keepdims=True)
        cen = conv - mu
        var = jnp.mean(cen * cen, axis=-1, keepdims=True)
        z = cen * lax.rsqrt(var + EPS) * clg_ref[layer:layer + 1, :] + clb_ref[layer:layer + 1, :]
        zc_buf[pl.ds(r, CHUNK), :] = _silu(z).astype(_BF16)

    side_work = [proj_kv, functools.partial(proj_q, 0), functools.partial(proj_q, 1),
                 functools.partial(proj_gattn, 0), functools.partial(proj_gattn, 1),
                 proj_upool, proj_gpool, proj_gconv]
    chunks_per_projection = (TS // CHUNK) // len(side_work)
    for c in range(TS // CHUNK):
        if c % chunks_per_projection == 0:
            side_work[c // chunks_per_projection]()
        conv_chunk(c)
    _unrolled(TS // CHUNK, pool_chunk)

    n_local = POOL_WIDTH + CONV_WIDTH

    top_rows = lax.broadcasted_iota(jnp.int32, (BF16_ROWS, 2 * LANES), 0)
    top_lanes = lax.broadcasted_iota(jnp.int32, (BF16_ROWS, 2 * LANES), 1)
    sink_value_slot = jnp.logical_and(top_rows == 0, top_lanes < LANES)
    sink_key_slot = lax.broadcasted_iota(jnp.int32, (BF16_ROWS, HEAD_DIM), 0) == 0

    def window_keys(kvh, r):
        top = kbuf[kvh, pl.ds(r, BF16_ROWS), :].astype(_F32)
        top = jnp.where(sink_key_slot, 0.0, top).astype(_BF16)
        return jnp.concatenate([top, kbuf[kvh, pl.ds(r + BF16_ROWS, 2 * BLOCK - BF16_ROWS), :]], axis=0)

    def scores(n):
        r = n * BLOCK
        seq_start = first_tile.astype(jnp.int32) if n == 0 else 0
        out = []
        for kvh in range(N_KV_HEADS):
            k2 = window_keys(kvh, r)
            q4 = jnp.concatenate([qbuf[kvh * Q_PER_KV + g, pl.ds(r, BLOCK), :] for g in range(Q_PER_KV)], axis=0)
            s = lax.dot_general(q4, k2, (((1,), (1,)), ((), ())), preferred_element_type=_F32)
            out.append(s + bias_buf[seq_start, kvh])
        return out

    def window_values(buf, kvh, r):
        top = buf[kvh, pl.ds(r, BF16_ROWS), :].astype(_F32)
        top = jnp.where(sink_value_slot, 0.0, top).astype(_BF16)
        return jnp.concatenate([top, buf[kvh, pl.ds(r + BF16_ROWS, 2 * BLOCK - BF16_ROWS), :]], axis=0)

    def softmax_pv(n, s_list):
        r = n * BLOCK
        for kvh in range(N_KV_HEADS):
            probs = []
            for g in range(Q_PER_KV):
                sg = s_list[kvh][g * BLOCK:(g + 1) * BLOCK]
                m = jnp.max(sg, axis=-1, keepdims=True)
                probs.append(jnp.exp2(sg - m).astype(_BF16))
            p_even = jnp.concatenate([probs[0], probs[2]], axis=0)
            p_odd = jnp.concatenate([probs[1], probs[3]], axis=0)
            o = (jnp.dot(p_even, window_values(ve_buf, kvh, r), preferred_element_type=_F32)
                 + jnp.dot(p_odd, window_values(vo_buf, kvh, r), preferred_element_type=_F32))
            res = o[:, 0:LANES] / o[:, LANES:2 * LANES]
            for pair in range(2):
                c0 = (kvh * 2 + pair) * LANES
                gate = ga_buf[pl.ds(r, BLOCK), c0:c0 + LANES]
                mix_ref[pl.ds(r, BLOCK), n_local + c0:n_local + c0 + LANES] = (
                    res[pair * BLOCK:(pair + 1) * BLOCK] * gate).astype(_BF16)

    def out_attn(r0, rows):
        for q in range(D_MODEL // MXU_COLS):
            cols = slice(q * MXU_COLS, (q + 1) * MXU_COLS)
            o_ref[0, pl.ds(r0, rows), cols] = x_ref[0, pl.ds(r0, rows), cols] + jnp.dot(
                mix_ref[pl.ds(r0, rows), n_local:D_MIX], wout_ref[0, n_local:D_MIX, cols].astype(_BF16),
                preferred_element_type=_F32)

    s_cur = scores(0)
    for n in range(NB):
        s_next = scores(n + 1) if n + 1 < NB else None
        if n >= 2 and n % 2 == 0:
            out_attn((n - 2) * BLOCK, 2 * BLOCK)
        softmax_pv(n, s_cur)
        s_cur = s_next
    out_attn((NB - 2) * BLOCK, 2 * BLOCK)

    y_pool = jnp.dot(zp_buf[...], poolw_ref[0].astype(_BF16), preferred_element_type=_F32)
    y_pool = y_pool * pscale_ref[layer:layer + 1, :]
    mix_ref[:, 0:POOL_WIDTH] = (y_pool * gp_buf[...]).astype(_BF16)
    y_conv = jnp.dot(zc_buf[...], pw_ref[0].astype(_BF16), preferred_element_type=_F32)
    mix_ref[:, POOL_WIDTH:POOL_WIDTH + CONV_WIDTH] = (y_conv * gc_buf[...]).astype(_BF16)

    def final_chunk(c):
        r = c * CHUNK
        xc = o_ref[0, pl.ds(r, CHUNK), :]
        ms = jnp.mean(xc * xc, axis=-1, keepdims=True)
        o_ref[0, pl.ds(r, CHUNK), :] = xc * lax.rsqrt(ms + EPS) * fg_ref[...]

    half_rows = TS // 2
    for r0 in range(0, TS, half_rows):
        rows = pl.ds(r0, half_rows)
        for q in range(D_MODEL // MXU_COLS):
            cols = slice(q * MXU_COLS, (q + 1) * MXU_COLS)
            o_ref[0, rows, cols] = o_ref[0, rows, cols] + jnp.dot(
                mix_ref[rows, 0:n_local], wout_ref[0, 0:n_local, cols].astype(_BF16), preferred_element_type=_F32)
        if final:
            for c in range(r0 // CHUNK, (r0 + half_rows) // CHUNK):
                final_chunk(c)


def _bias_table():
    i = np.arange(BLOCK)[:, None]
    jj = np.arange(2 * BLOCK)[None, :]
    dist = BLOCK + i - jj
    in_band = (dist >= 0) & (dist < WINDOW)
    slopes = np.asarray([2.0 ** (-8.0 * (h + 1) / N_Q_HEADS) for h in range(N_Q_HEADS)], dtype=np.float32)
    tab = np.empty((2, N_KV_HEADS, Q_PER_KV * BLOCK, 2 * BLOCK), np.float32)
    for start in range(2):
        valid = in_band & ~((jj < BLOCK) & (start == 1))
        for h in range(N_Q_HEADS):
            bias = (-slopes[h] * LOG2E) * dist.astype(np.float64)
            kvh, g = divmod(h, Q_PER_KV)
            tab[start, kvh, g * BLOCK:(g + 1) * BLOCK] = np.where(valid, bias, -1e30).astype(np.float32)
    return tab


def _inv_count_table():
    t = np.arange(HALO_U, dtype=np.float32)[:, None]
    gw = POOL_WIDTH // POOL_GROUPS
    wnd = np.repeat(np.asarray(POOL_WINDOWS, np.float32), gw)[None, :]
    return (1.0 / np.minimum(t + 1.0, wnd)).astype(np.float32)


def _layer_call(batch, seq, layer, final):
    n_tiles = seq // TS
    const = lambda shape: pl.BlockSpec(shape, lambda b, j: (0,) * len(shape), pipeline_mode=pl.Buffered(1))
    per_layer = lambda shape: pl.BlockSpec((1,) + shape, lambda b, j: (layer,) + (0,) * len(shape),
                                           pipeline_mode=pl.Buffered(1))
    in_specs = [
        pl.BlockSpec((1, TS, D_MODEL), lambda b, j: (b, j, 0)),
        const((DEPTH, D_MODEL)),
        per_layer((D_MODEL, D_IN)),
        per_layer((POOL_WIDTH, POOL_WIDTH)),
        const((DEPTH, POOL_WIDTH)),
        const((HALO_U, POOL_WIDTH)),
        per_layer((CONV_KERNEL, CONV_WIDTH)),
        const((DEPTH, CONV_WIDTH)),
        const((DEPTH, CONV_WIDTH)),
        const((DEPTH, CONV_WIDTH)),
        per_layer((CONV_WIDTH, CONV_WIDTH)),
        pl.BlockSpec(memory_space=pltpu.SMEM),
        const((2, N_KV_HEADS, Q_PER_KV * BLOCK, 2 * BLOCK)),
        per_layer((D_MIX, D_MODEL)),
        const((1, D_MODEL)),
    ]
    scratch = [
        pltpu.VMEM((TS, D_MODEL), _BF16),
        pltpu.VMEM((TS + HALO_U, POOL_WIDTH), _F32),
        pltpu.VMEM((TS + HALO_H, CONV_WIDTH), _F32),
        pltpu.VMEM((TS, POOL_WIDTH), _F32),
        pltpu.VMEM((TS, CONV_WIDTH), _F32),
        pltpu.VMEM((TS, ATTN_WIDTH), _F32),
        pltpu.VMEM((TS, POOL_WIDTH), _BF16),
        pltpu.VMEM((TS, CONV_WIDTH), _BF16),
        pltpu.VMEM((N_KV_HEADS, TS + BLOCK, HEAD_DIM), _BF16),
        pltpu.VMEM((N_KV_HEADS, TS + BLOCK, 2 * LANES), _BF16),
        pltpu.VMEM((N_KV_HEADS, TS + BLOCK, 2 * LANES), _BF16),
        pltpu.VMEM((N_Q_HEADS, TS, HEAD_DIM), _BF16),
        pltpu.VMEM((TS, D_MIX), _BF16),
        pltpu.VMEM((2, N_KV_HEADS, Q_PER_KV * BLOCK, 2 * BLOCK), _F32),
    ]
    return pl.pallas_call(
        functools.partial(_layer_body, layer=layer, final=final),
        out_shape=jax.ShapeDtypeStruct((batch, seq, D_MODEL), _F32),
        grid=(batch, n_tiles),
        in_specs=in_specs,
        out_specs=pl.BlockSpec((1, TS, D_MODEL), lambda b, j: (b, j, 0)),
        scratch_shapes=scratch,
        compiler_params=pltpu.CompilerParams(
            dimension_semantics=("arbitrary", "arbitrary"),
            vmem_limit_bytes=VMEM_LIMIT_BYTES,
        ),
        name="hybrid_layer_final" if final else "hybrid_layer",
    )


def kernel(x, ln_g, w_in, pool_w, pool_scale, conv_dw, conv_b, conv_ln_g, conv_ln_b, conv_pw, attn_sinks, w_out, final_g):
    batch, seq, d_model = x.shape
    assert d_model == D_MODEL and seq % TS == 0 and TS % BLOCK == 0
    bias = jnp.asarray(_bias_table())
    inv_cnt = jnp.asarray(_inv_count_table())
    poolw_bd = jnp.stack([jax.scipy.linalg.block_diag(*[pool_w[l, g] for g in range(POOL_GROUPS)])
                          for l in range(DEPTH)])
    params = (ln_g, w_in, poolw_bd, pool_scale, inv_cnt, conv_dw, conv_b, conv_ln_g, conv_ln_b, conv_pw,
              attn_sinks, bias, w_out, final_g.reshape(1, -1))
    for l in range(DEPTH):
        x = _layer_call(batch, seq, l, final=(l == DEPTH - 1))(x, *params)
    return x
```

```python
import functools

import numpy as np
import jax
import jax.numpy as jnp
from jax import lax
from jax.experimental import pallas as pl
from jax.experimental.pallas import tpu as pltpu

D_MODEL = 1024
DEPTH = 2
POOL_WIDTH = 256
POOL_GROUPS = 4
POOL_WINDOWS = (2, 4, 8, 16)
CONV_WIDTH = 256
CONV_KERNEL = 31
HEAD_DIM = 64
N_Q_HEADS = 8
N_KV_HEADS = 2
Q_PER_KV = N_Q_HEADS // N_KV_HEADS
ATTN_WIDTH = N_Q_HEADS * HEAD_DIM
KV_WIDTH = N_KV_HEADS * HEAD_DIM
WINDOW = 128
BLOCK = 128
D_MIX = POOL_WIDTH + CONV_WIDTH + ATTN_WIDTH
D_IN = 2 * POOL_WIDTH + 3 * CONV_WIDTH + 2 * ATTN_WIDTH + 2 * KV_WIDTH
EPS = 1e-6

C_UPOOL = 0
C_GPOOL = C_UPOOL + POOL_WIDTH
C_CA = C_GPOOL + POOL_WIDTH
C_CB = C_CA + CONV_WIDTH
C_GCONV = C_CB + CONV_WIDTH
C_Q = C_GCONV + CONV_WIDTH
C_K = C_Q + ATTN_WIDTH
C_V = C_K + KV_WIDTH
C_GATTN = C_V + KV_WIDTH

SUBLANES = 8
LANES = 128
BF16_ROWS = 16
MXU_COLS = 256
TS = 1024
NB = TS // BLOCK
CHUNK = 64
GROUPS = CHUNK // SUBLANES
HALO_U = 16
HALO_H = 32
VMEM_LIMIT_BYTES = 63 * 1024 * 1024
LOG2E = 1.4426950408889634

_F32 = jnp.float32
_BF16 = jnp.bfloat16


def _silu(v):
    return v * jax.nn.sigmoid(v)


def _unrolled(n, body):
    for c in range(n):
        body(c)


def _shift_down(groups, d, row):
    rolled = [pltpu.roll(g, d, axis=0) for g in groups]
    take_prev = row < d
    out = [rolled[0]]
    for i in range(1, len(groups)):
        out.append(jnp.where(take_prev, rolled[i - 1], rolled[i]))
    return out


def _layer_body(x_ref, lng_ref, win_ref, poolw_ref, pscale_ref, invcnt_ref, dw_ref, cb_ref, clg_ref, clb_ref,
                pw_ref, sinks_ref, bias_ref, wout_ref, fg_ref, o_ref,
                hn_ref, ubuf, hbuf, gp_buf, gc_buf, ga_buf, zp_buf, zc_buf, kbuf, ve_buf, vo_buf, qbuf, mix_ref,
                bias_buf, win_bf, wout_bf, *, layer, final):
    j = pl.program_id(1)
    first_tile = j == 0

    @pl.when(jnp.logical_and(pl.program_id(0) == 0, first_tile))
    def _():
        for c in range(D_IN // MXU_COLS):
            cols = slice(c * MXU_COLS, (c + 1) * MXU_COLS)
            win_bf[:, cols] = win_ref[0, :, cols].astype(_BF16)
        for c in range(D_MODEL // MXU_COLS):
            cols = slice(c * MXU_COLS, (c + 1) * MXU_COLS)
            wout_bf[:, cols] = wout_ref[0, :, cols].astype(_BF16)

    row = lax.broadcasted_iota(jnp.int32, (SUBLANES, LANES), 0)
    lane = lax.broadcasted_iota(jnp.int32, (SUBLANES, LANES), 1)
    low_half = lane < HEAD_DIM

    @pl.when(first_tile)
    def _():
        sink_slot = lax.broadcasted_iota(jnp.int32, (BLOCK, 2 * BLOCK), 1) == 0
        for start in range(2):
            for h in range(N_Q_HEADS):
                kvh, g = divmod(h, Q_PER_KV)
                rows = pl.ds(g * BLOCK, BLOCK)
                bias_buf[start, kvh, rows, :] = jnp.where(
                    sink_slot, sinks_ref[layer, h] * LOG2E, bias_ref[start, kvh, rows, :])
        ubuf[0:HALO_U, :] = jnp.zeros((HALO_U, POOL_WIDTH), _F32)
        hbuf[0:HALO_H, :] = jnp.zeros((HALO_H, CONV_WIDTH), _F32)
        kbuf[:, 0:BLOCK, :] = jnp.zeros((N_KV_HEADS, BLOCK, HEAD_DIM), _BF16)
        zeros = jnp.zeros((TS + BLOCK, LANES), _BF16)
        lane_t = lax.broadcasted_iota(jnp.int32, (TS + BLOCK, LANES), 1)
        ones_lo = jnp.where(lane_t < HEAD_DIM, 1.0, 0.0).astype(_BF16)
        ones_hi = jnp.where(lane_t < HEAD_DIM, 0.0, 1.0).astype(_BF16)
        for kvh in range(N_KV_HEADS):
            ve_buf[kvh, :, 0:LANES] = zeros
            vo_buf[kvh, :, 0:LANES] = zeros
            ve_buf[kvh, :, LANES:2 * LANES] = ones_lo
            vo_buf[kvh, :, LANES:2 * LANES] = ones_hi

    @pl.when(jnp.logical_not(first_tile))
    def _():
        ubuf[0:HALO_U, :] = ubuf[TS:TS + HALO_U, :]
        hbuf[0:HALO_H, :] = hbuf[TS:TS + HALO_H, :]
        kbuf[:, 0:BLOCK, :] = kbuf[:, TS:TS + BLOCK, :]
        ve_buf[:, 0:BLOCK, 0:LANES] = ve_buf[:, TS:TS + BLOCK, 0:LANES]
        vo_buf[:, 0:BLOCK, 0:LANES] = vo_buf[:, TS:TS + BLOCK, 0:LANES]

    def norm_chunk(c):
        r = c * CHUNK
        xc = x_ref[0, pl.ds(r, CHUNK), :]
        ms = jnp.mean(xc * xc, axis=-1, keepdims=True)
        y = xc * lax.rsqrt(ms + EPS) * lng_ref[layer:layer + 1, :]
        hn_ref[pl.ds(r, CHUNK), :] = y.astype(_BF16)

    _unrolled(TS // CHUNK, norm_chunk)

    def proj(c0, width, r0=0, rows=TS):
        w = win_bf[:, c0:c0 + width]
        return jnp.dot(hn_ref[pl.ds(r0, rows), :], w, preferred_element_type=_F32)

    for r0 in range(0, TS, TS // 2):
        rows = pl.ds(HALO_H + r0, TS // 2)
        hbuf[rows, :] = proj(C_CA, CONV_WIDTH, r0, TS // 2)
        hbuf[rows, :] = hbuf[rows, :] * jax.nn.sigmoid(proj(C_CB, CONV_WIDTH, r0, TS // 2))

    kv = proj(C_K, 2 * KV_WIDTH)
    keys = kv[:, 0:KV_WIDTH]
    vals = kv[:, KV_WIDTH:2 * KV_WIDTH]
    vals_swapped = pltpu.roll(vals, HEAD_DIM, axis=1)
    lo = lax.broadcasted_iota(jnp.int32, (TS, LANES), 1) < HEAD_DIM
    for kvh in range(N_KV_HEADS):
        kbuf[kvh, BLOCK:BLOCK + TS, :] = keys[:, kvh * HEAD_DIM:(kvh + 1) * HEAD_DIM].astype(_BF16)
        own_lo, own_hi = (vals, vals_swapped) if kvh == 0 else (vals_swapped, vals)
        ve_buf[kvh, BLOCK:BLOCK + TS, 0:LANES] = jnp.where(lo, own_lo, 0.0).astype(_BF16)
        vo_buf[kvh, BLOCK:BLOCK + TS, 0:LANES] = jnp.where(lo, 0.0, own_hi).astype(_BF16)

    heads_per_group = MXU_COLS // HEAD_DIM
    for half in range(ATTN_WIDTH // MXU_COLS):
        qh = proj(C_Q + half * MXU_COLS, MXU_COLS) * (HEAD_DIM ** -0.5 * LOG2E)
        for i in range(heads_per_group):
            qbuf[half * heads_per_group + i, :, :] = qh[:, i * HEAD_DIM:(i + 1) * HEAD_DIM].astype(_BF16)
    for half in range(ATTN_WIDTH // MXU_COLS):
        ga_buf[:, half * MXU_COLS:(half + 1) * MXU_COLS] = _silu(proj(C_GATTN + half * MXU_COLS, MXU_COLS))
    gp_buf[...] = _silu(proj(C_GPOOL, POOL_WIDTH))
    gc_buf[...] = _silu(proj(C_GCONV, CONV_WIDTH))

    ubuf[HALO_U:HALO_U + TS, :] = proj(C_UPOOL, POOL_WIDTH)

    inv_w = [jnp.where(low_half, 1.0 / POOL_WINDOWS[0], 1.0 / POOL_WINDOWS[1]).astype(_F32),
             jnp.where(low_half, 1.0 / POOL_WINDOWS[2], 1.0 / POOL_WINDOWS[3]).astype(_F32)]

    def pool_chunk(c):
        r = c * CHUNK
        n_in = GROUPS + HALO_U // SUBLANES
        for col in range(2):
            lanes = slice(col * LANES, (col + 1) * LANES)
            u = [ubuf[pl.ds(r + SUBLANES * i, SUBLANES), lanes] for i in range(n_in)]
            s2 = [a + b for a, b in zip(u, _shift_down(u, 1, row))]
            s4 = [a + b for a, b in zip(s2, _shift_down(s2, 2, row))]
            if col == 0:
                wide, narrow = s4, s2
            else:
                s8 = [a + b for a, b in zip(s4, _shift_down(s4, 4, row))]
                s16 = [s8[0]] + [s8[i] + s8[i - 1] for i in range(1, n_in)]
                wide, narrow = s16, s8
            pieces = []
            for g in range(GROUPS):
                i = g + HALO_U // SUBLANES
                inv = inv_w[col]
                if c == 0 and g < HALO_U // SUBLANES:
                    tab = invcnt_ref[SUBLANES * g:SUBLANES * (g + 1), lanes]
                    inv = jnp.where(first_tile, tab, inv)
                pooled = jnp.where(low_half, narrow[i], wide[i]) * inv
                pieces.append(pooled - u[i])
            zp_buf[pl.ds(r, CHUNK), lanes] = jnp.concatenate(pieces, axis=0).astype(_BF16)

    taps_by_b = [[] for _ in range(SUBLANES)]
    for k in range(CONV_KERNEL):
        a, b = divmod(k + HALO_H - (CONV_KERNEL - 1), SUBLANES)
        taps_by_b[b].append((a, k))
    n_h = GROUPS + HALO_H // SUBLANES

    def conv_chunk(c):
        r = c * CHUNK
        outs = []
        for col in range(2):
            lanes = slice(col * LANES, (col + 1) * LANES)
            h = [hbuf[pl.ds(r + SUBLANES * m, SUBLANES), lanes] for m in range(n_h)]
            acc = None
            for b in range(SUBLANES):
                n_p = GROUPS if b == 0 else GROUPS + 1
                part = []
                for m in range(n_p):
                    p = None
                    for a, k in taps_by_b[b]:
                        term = h[m + a] * dw_ref[0, k:k + 1, lanes]
                        p = term if p is None else p + term
                    part.append(p)
                if b == 0:
                    acc = part
                else:
                    rolled = [pltpu.roll(p, SUBLANES - b, axis=0) for p in part]
                    keep = row < (SUBLANES - b)
                    acc = [acc[g] + jnp.where(keep, rolled[g], rolled[g + 1]) for g in range(GROUPS)]
            outs.append(jnp.concatenate(acc, axis=0))
        conv = jnp.concatenate(outs, axis=1) + cb_ref[layer:layer + 1, :]
        mu = jnp.mean(conv, axis=-1, keepdims=True)
        cen = conv - mu
        var = jnp.mean(cen * cen, axis=-1, keepdims=True)
        z = cen * lax.rsqrt(var + EPS) * clg_ref[layer:layer + 1, :] + clb_ref[layer:layer + 1, :]
        zc_buf[pl.ds(r, CHUNK), :] = _silu(z).astype(_BF16)

    _unrolled(TS // CHUNK, conv_chunk)
    _unrolled(TS // CHUNK, pool_chunk)

    n_local = POOL_WIDTH + CONV_WIDTH

    top_rows = lax.broadcasted_iota(jnp.int32, (BF16_ROWS, 2 * LANES), 0)
    top_lanes = lax.broadcasted_iota(jnp.int32, (BF16_ROWS, 2 * LANES), 1)
    sink_value_slot = jnp.logical_and(top_rows == 0, top_lanes < LANES)
    sink_key_slot = lax.broadcasted_iota(jnp.int32, (BF16_ROWS, HEAD_DIM), 0) == 0

    def window_keys(kvh, r):
        top = kbuf[kvh, pl.ds(r, BF16_ROWS), :].astype(_F32)
        top = jnp.where(sink_key_slot, 0.0, top).astype(_BF16)
        return jnp.concatenate([top, kbuf[kvh, pl.ds(r + BF16_ROWS, 2 * BLOCK - BF16_ROWS), :]], axis=0)

    def scores(n):
        r = n * BLOCK
        seq_start = first_tile.astype(jnp.int32) if n == 0 else 0
        out = []
        for kvh in range(N_KV_HEADS):
            k2 = window_keys(kvh, r)
            q4 = jnp.concatenate([qbuf[kvh * Q_PER_KV + g, pl.ds(r, BLOCK), :] for g in range(Q_PER_KV)], axis=0)
            s = lax.dot_general(q4, k2, (((1,), (1,)), ((), ())), preferred_element_type=_F32)
            out.append(s + bias_buf[seq_start, kvh])
        return out

    def window_values(buf, kvh, r):
        top = buf[kvh, pl.ds(r, BF16_ROWS), :].astype(_F32)
        top = jnp.where(sink_value_slot, 0.0, top).astype(_BF16)
        return jnp.concatenate([top, buf[kvh, pl.ds(r + BF16_ROWS, 2 * BLOCK - BF16_ROWS), :]], axis=0)

    def softmax_pv(n, s_list):
        r = n * BLOCK
        for kvh in range(N_KV_HEADS):
            probs = []
            for g in range(Q_PER_KV):
                sg = s_list[kvh][g * BLOCK:(g + 1) * BLOCK]
                m = jnp.max(sg, axis=-1, keepdims=True)
                probs.append(jnp.exp2(sg - m).astype(_BF16))
            p_even = jnp.concatenate([probs[0], probs[2]], axis=0)
            p_odd = jnp.concatenate([probs[1], probs[3]], axis=0)
            o = (jnp.dot(p_even, window_values(ve_buf, kvh, r), preferred_element_type=_F32)
                 + jnp.dot(p_odd, window_values(vo_buf, kvh, r), preferred_element_type=_F32))
            res = o[:, 0:LANES] / o[:, LANES:2 * LANES]
            for pair in range(2):
                c0 = (kvh * 2 + pair) * LANES
                gate = ga_buf[pl.ds(r, BLOCK), c0:c0 + LANES]
                mix_ref[pl.ds(r, BLOCK), n_local + c0:n_local + c0 + LANES] = (
                    res[pair * BLOCK:(pair + 1) * BLOCK] * gate).astype(_BF16)

    def out_attn(r0, rows):
        for q in range(D_MODEL // MXU_COLS):
            cols = slice(q * MXU_COLS, (q + 1) * MXU_COLS)
            o_ref[0, pl.ds(r0, rows), cols] = x_ref[0, pl.ds(r0, rows), cols] + jnp.dot(
                mix_ref[pl.ds(r0, rows), n_local:D_MIX], wout_bf[n_local:D_MIX, cols],
                preferred_element_type=_F32)

    s_cur = scores(0)
    for n in range(NB):
        s_next = scores(n + 1) if n + 1 < NB else None
        if n >= 2 and n % 2 == 0:
            out_attn((n - 2) * BLOCK, 2 * BLOCK)
        softmax_pv(n, s_cur)
        s_cur = s_next
    out_attn((NB - 2) * BLOCK, 2 * BLOCK)

    y_pool = jnp.dot(zp_buf[...], poolw_ref[0].astype(_BF16), preferred_element_type=_F32)
    y_pool = y_pool * pscale_ref[layer:layer + 1, :]
    mix_ref[:, 0:POOL_WIDTH] = (y_pool * gp_buf[...]).astype(_BF16)
    y_conv = jnp.dot(zc_buf[...], pw_ref[0].astype(_BF16), preferred_element_type=_F32)
    mix_ref[:, POOL_WIDTH:POOL_WIDTH + CONV_WIDTH] = (y_conv * gc_buf[...]).astype(_BF16)

    def final_chunk(c):
        r = c * CHUNK
        xc = o_ref[0, pl.ds(r, CHUNK), :]
        ms = jnp.mean(xc * xc, axis=-1, keepdims=True)
        o_ref[0, pl.ds(r, CHUNK), :] = xc * lax.rsqrt(ms + EPS) * fg_ref[...]

    half_rows = TS // 2
    for r0 in range(0, TS, half_rows):
        rows = pl.ds(r0, half_rows)
        for q in range(D_MODEL // MXU_COLS):
            cols = slice(q * MXU_COLS, (q + 1) * MXU_COLS)
            o_ref[0, rows, cols] = o_ref[0, rows, cols] + jnp.dot(
                mix_ref[rows, 0:n_local], wout_bf[0:n_local, cols], preferred_element_type=_F32)
        if final:
            for c in range(r0 // CHUNK, (r0 + half_rows) // CHUNK):
                final_chunk(c)


def _bias_table():
    i = np.arange(BLOCK)[:, None]
    jj = np.arange(2 * BLOCK)[None, :]
    dist = BLOCK + i - jj
    in_band = (dist >= 0) & (dist < WINDOW)
    slopes = np.asarray([2.0 ** (-8.0 * (h + 1) / N_Q_HEADS) for h in range(N_Q_HEADS)], dtype=np.float32)
    tab = np.empty((2, N_KV_HEADS, Q_PER_KV * BLOCK, 2 * BLOCK), np.float32)
    for start in range(2):
        valid = in_band & ~((jj < BLOCK) & (start == 1))
        for h in range(N_Q_HEADS):
            bias = (-slopes[h] * LOG2E) * dist.astype(np.float64)
            kvh, g = divmod(h, Q_PER_KV)
            tab[start, kvh, g * BLOCK:(g + 1) * BLOCK] = np.where(valid, bias, -1e30).astype(np.float32)
    return tab


def _inv_count_table():
    t = np.arange(HALO_U, dtype=np.float32)[:, None]
    gw = POOL_WIDTH // POOL_GROUPS
    wnd = np.repeat(np.asarray(POOL_WINDOWS, np.float32), gw)[None, :]
    return (1.0 / np.minimum(t + 1.0, wnd)).astype(np.float32)


def _layer_call(batch, seq, layer, final):
    n_tiles = seq // TS
    const = lambda shape: pl.BlockSpec(shape, lambda b, j: (0,) * len(shape), pipeline_mode=pl.Buffered(1))
    per_layer = lambda shape: pl.BlockSpec((1,) + shape, lambda b, j: (layer,) + (0,) * len(shape),
                                           pipeline_mode=pl.Buffered(1))
    in_specs = [
        pl.BlockSpec((1, TS, D_MODEL), lambda b, j: (b, j, 0)),
        const((DEPTH, D_MODEL)),
        per_layer((D_MODEL, D_IN)),
        per_layer((POOL_WIDTH, POOL_WIDTH)),
        const((DEPTH, POOL_WIDTH)),
        const((HALO_U, POOL_WIDTH)),
        per_layer((CONV_KERNEL, CONV_WIDTH)),
        const((DEPTH, CONV_WIDTH)),
        const((DEPTH, CONV_WIDTH)),
        const((DEPTH, CONV_WIDTH)),
        per_layer((CONV_WIDTH, CONV_WIDTH)),
        pl.BlockSpec(memory_space=pltpu.SMEM),
        const((2, N_KV_HEADS, Q_PER_KV * BLOCK, 2 * BLOCK)),
        per_layer((D_MIX, D_MODEL)),
        const((1, D_MODEL)),
    ]
    scratch = [
        pltpu.VMEM((TS, D_MODEL), _BF16),
        pltpu.VMEM((TS + HALO_U, POOL_WIDTH), _F32),
        pltpu.VMEM((TS + HALO_H, CONV_WIDTH), _F32),
        pltpu.VMEM((TS, POOL_WIDTH), _F32),
        pltpu.VMEM((TS, CONV_WIDTH), _F32),
        pltpu.VMEM((TS, ATTN_WIDTH), _F32),
        pltpu.VMEM((TS, POOL_WIDTH), _BF16),
        pltpu.VMEM((TS, CONV_WIDTH), _BF16),
        pltpu.VMEM((N_KV_HEADS, TS + BLOCK, HEAD_DIM), _BF16),
        pltpu.VMEM((N_KV_HEADS, TS + BLOCK, 2 * LANES), _BF16),
        pltpu.VMEM((N_KV_HEADS, TS + BLOCK, 2 * LANES), _BF16),
        pltpu.VMEM((N_Q_HEADS, TS, HEAD_DIM), _BF16),
        pltpu.VMEM((TS, D_MIX), _BF16),
        pltpu.VMEM((2, N_KV_HEADS, Q_PER_KV * BLOCK, 2 * BLOCK), _F32),
        pltpu.VMEM((D_MODEL, D_IN), _BF16),
        pltpu.VMEM((D_MIX, D_MODEL), _BF16),
    ]
    return pl.pallas_call(
        functools.partial(_layer_body, layer=layer, final=final),
        out_shape=jax.ShapeDtypeStruct((batch, seq, D_MODEL), _F32),
        grid=(batch, n_tiles),
        in_specs=in_specs,
        out_specs=pl.BlockSpec((1, TS, D_MODEL), lambda b, j: (b, j, 0)),
        scratch_shapes=scratch,
        compiler_params=pltpu.CompilerParams(
            dimension_semantics=("arbitrary", "arbitrary"),
            vmem_limit_bytes=VMEM_LIMIT_BYTES,
        ),
        name="hybrid_layer_final" if final else "hybrid_layer",
    )


def kernel(x, ln_g, w_in, pool_w, pool_scale, conv_dw, conv_b, conv_ln_g, conv_ln_b, conv_pw, attn_sinks, w_out, final_g):
    batch, seq, d_model = x.shape
    assert d_model == D_MODEL and seq % TS == 0 and TS % BLOCK == 0
    bias = jnp.asarray(_bias_table())
    inv_cnt = jnp.asarray(_inv_count_table())
    poolw_bd = jnp.stack([jax.scipy.linalg.block_diag(*[pool_w[l, g] for g in range(POOL_GROUPS)])
                          for l in range(DEPTH)])
    params = (ln_g, w_in, poolw_bd, pool_scale, inv_cnt, conv_dw, conv_b, conv_ln_g, conv_ln_b, conv_pw,
              attn_sinks, bias, w_out, final_g.reshape(1, -1))
    for l in range(DEPTH):
        x = _layer_call(batch, seq, l, final=(l == DEPTH - 1))(x, *params)
    return x
```

```python
import functools

import numpy as np
import jax
import jax.numpy as jnp
from jax import lax
from jax.experimental import pallas as pl
from jax.experimental.pallas import tpu as pltpu

D_MODEL = 1024
DEPTH = 2
POOL_WIDTH = 256
POOL_GROUPS = 4
POOL_WINDOWS = (2, 4, 8, 16)
CONV_WIDTH = 256
CONV_KERNEL = 31
HEAD_DIM = 64
N_Q_HEADS = 8
N_KV_HEADS = 2
Q_PER_KV = N_Q_HEADS // N_KV_HEADS
ATTN_WIDTH = N_Q_HEADS * HEAD_DIM
KV_WIDTH = N_KV_HEADS * HEAD_DIM
WINDOW = 128
BLOCK = 128
D_MIX = POOL_WIDTH + CONV_WIDTH + ATTN_WIDTH
D_IN = 2 * POOL_WIDTH + 3 * CONV_WIDTH + 2 * ATTN_WIDTH + 2 * KV_WIDTH
EPS = 1e-6

C_UPOOL = 0
C_GPOOL = C_UPOOL + POOL_WIDTH
C_CA = C_GPOOL + POOL_WIDTH
C_CB = C_CA + CONV_WIDTH
C_GCONV = C_CB + CONV_WIDTH
C_Q = C_GCONV + CONV_WIDTH
C_K = C_Q + ATTN_WIDTH
C_V = C_K + KV_WIDTH
C_GATTN = C_V + KV_WIDTH

SUBLANES = 8
LANES = 128
BF16_ROWS = 16
MXU_COLS = 256
TS = 512
NB = TS // BLOCK
CHUNK = 64
GROUPS = CHUNK // SUBLANES
HALO_U = 16
HALO_H = 32
VMEM_LIMIT_BYTES = 63 * 1024 * 1024
LOG2E = 1.4426950408889634

_F32 = jnp.float32
_BF16 = jnp.bfloat16

_SET_SHAPES = (
    ("hbuf", (TS + HALO_H, CONV_WIDTH), _F32),
    ("ubuf", (TS + HALO_U, POOL_WIDTH), _F32),
    ("gp", (TS, POOL_WIDTH), _F32),
    ("gc", (TS, CONV_WIDTH), _F32),
    ("ga", (TS, ATTN_WIDTH), _F32),
    ("kbuf", (N_KV_HEADS, TS + BLOCK, HEAD_DIM), _BF16),
    ("ve", (N_KV_HEADS, TS + BLOCK, 2 * LANES), _BF16),
    ("vo", (N_KV_HEADS, TS + BLOCK, 2 * LANES), _BF16),
    ("qbuf", (N_Q_HEADS, TS, HEAD_DIM), _BF16),
)


def _silu(v):
    return v * jax.nn.sigmoid(v)


def _shift_down(groups, d, row):
    rolled = [pltpu.roll(g, d, axis=0) for g in groups]
    take_prev = row < d
    out = [rolled[0]]
    for i in range(1, len(groups)):
        out.append(jnp.where(take_prev, rolled[i - 1], rolled[i]))
    return out


def _layer_body(xp_ref, xc_ref, lng_ref, win_ref, poolw_ref, pscale_ref, invcnt_ref, dw_ref, cb_ref, clg_ref,
                clb_ref, pw_ref, sinks_ref, bias_ref, wout_ref, fg_ref, o_ref, *scratch, layer, final, n_tiles):
    n_set = len(_SET_SHAPES)
    set_a = dict(zip((name for name, _, _ in _SET_SHAPES), scratch[0:n_set]))
    set_b = dict(zip((name for name, _, _ in _SET_SHAPES), scratch[n_set:2 * n_set]))
    hn_ref, zp_buf, zc_buf, mix_ref, bias_buf, win_bf, wout_bf = scratch[2 * n_set:]

    j = pl.program_id(1)
    step = pl.program_id(0) * (n_tiles + 1) + j
    n_local = POOL_WIDTH + CONV_WIDTH

    @pl.when(step == 0)
    def _():
        for c in range(D_IN // MXU_COLS):
            cols = slice(c * MXU_COLS, (c + 1) * MXU_COLS)
            win_bf[:, cols] = win_ref[0, :, cols].astype(_BF16)
        for c in range(D_MODEL // MXU_COLS):
            cols = slice(c * MXU_COLS, (c + 1) * MXU_COLS)
            wout_bf[:, cols] = wout_ref[0, :, cols].astype(_BF16)
        sink_slot = lax.broadcasted_iota(jnp.int32, (BLOCK, 2 * BLOCK), 1) == 0
        for start in range(2):
            for h in range(N_Q_HEADS):
                kvh, g = divmod(h, Q_PER_KV)
                rows = pl.ds(g * BLOCK, BLOCK)
                bias_buf[start, kvh, rows, :] = jnp.where(
                    sink_slot, sinks_ref[layer, h] * LOG2E, bias_ref[start, kvh, rows, :])
        lane_t = lax.broadcasted_iota(jnp.int32, (TS + BLOCK, LANES), 1)
        ones_lo = jnp.where(lane_t < HEAD_DIM, 1.0, 0.0).astype(_BF16)
        ones_hi = jnp.where(lane_t < HEAD_DIM, 0.0, 1.0).astype(_BF16)
        for s in (set_a, set_b):
            for name, shape, dtype in _SET_SHAPES:
                if name in ("ve", "vo"):
                    for kvh in range(N_KV_HEADS):
                        s[name][kvh, :, 0:LANES] = jnp.zeros((TS + BLOCK, LANES), dtype)
                        s[name][kvh, :, LANES:2 * LANES] = ones_lo if name == "ve" else ones_hi
                else:
                    s[name][...] = jnp.zeros(shape, dtype)

    row = lax.broadcasted_iota(jnp.int32, (SUBLANES, LANES), 0)
    lane = lax.broadcasted_iota(jnp.int32, (SUBLANES, LANES), 1)
    low_half = lane < HEAD_DIM

    def project_tile(dst, src):
        seq_start = j == 0

        def halo(tail, shape, dtype):
            return jnp.where(seq_start, jnp.zeros(shape, dtype), tail)

        dst["ubuf"][0:HALO_U, :] = halo(src["ubuf"][TS:TS + HALO_U, :], (HALO_U, POOL_WIDTH), _F32)
        dst["hbuf"][0:HALO_H, :] = halo(src["hbuf"][TS:TS + HALO_H, :], (HALO_H, CONV_WIDTH), _F32)
        for kvh in range(N_KV_HEADS):
            for name, width in (("kbuf", HEAD_DIM), ("ve", LANES), ("vo", LANES)):
                tail = src[name][kvh, TS:TS + BLOCK, 0:width].astype(_F32)
                dst[name][kvh, 0:BLOCK, 0:width] = halo(tail, (BLOCK, width), _F32).astype(_BF16)

        for c in range(TS // CHUNK):
            r = c * CHUNK
            xc = xp_ref[0, pl.ds(r, CHUNK), :]
            ms = jnp.mean(xc * xc, axis=-1, keepdims=True)
            y = xc * lax.rsqrt(ms + EPS) * lng_ref[layer:layer + 1, :]
            hn_ref[pl.ds(r, CHUNK), :] = y.astype(_BF16)

        def proj(c0, width, r0=0, rows=TS):
            return jnp.dot(hn_ref[pl.ds(r0, rows), :], win_bf[:, c0:c0 + width], preferred_element_type=_F32)

        for r0 in range(0, TS, TS // 2):
            rows = pl.ds(HALO_H + r0, TS // 2)
            dst["hbuf"][rows, :] = proj(C_CA, CONV_WIDTH, r0, TS // 2)
            dst["hbuf"][rows, :] = dst["hbuf"][rows, :] * jax.nn.sigmoid(proj(C_CB, CONV_WIDTH, r0, TS // 2))

        kv = proj(C_K, 2 * KV_WIDTH)
        keys = kv[:, 0:KV_WIDTH]
        vals = kv[:, KV_WIDTH:2 * KV_WIDTH]
        vals_swapped = pltpu.roll(vals, HEAD_DIM, axis=1)
        lo = lax.broadcasted_iota(jnp.int32, (TS, LANES), 1) < HEAD_DIM
        for kvh in range(N_KV_HEADS):
            dst["kbuf"][kvh, BLOCK:BLOCK + TS, :] = keys[:, kvh * HEAD_DIM:(kvh + 1) * HEAD_DIM].astype(_BF16)
            own_lo, own_hi = (vals, vals_swapped) if kvh == 0 else (vals_swapped, vals)
            dst["ve"][kvh, BLOCK:BLOCK + TS, 0:LANES] = jnp.where(lo, own_lo, 0.0).astype(_BF16)
            dst["vo"][kvh, BLOCK:BLOCK + TS, 0:LANES] = jnp.where(lo, 0.0, own_hi).astype(_BF16)

        heads_per_group = MXU_COLS // HEAD_DIM
        for half in range(ATTN_WIDTH // MXU_COLS):
            qh = proj(C_Q + half * MXU_COLS, MXU_COLS) * (HEAD_DIM ** -0.5 * LOG2E)
            for i in range(heads_per_group):
                dst["qbuf"][half * heads_per_group + i, :, :] = qh[:, i * HEAD_DIM:(i + 1) * HEAD_DIM].astype(_BF16)
        for half in range(ATTN_WIDTH // MXU_COLS):
            dst["ga"][:, half * MXU_COLS:(half + 1) * MXU_COLS] = _silu(proj(C_GATTN + half * MXU_COLS, MXU_COLS))
        dst["gp"][...] = _silu(proj(C_GPOOL, POOL_WIDTH))
        dst["gc"][...] = _silu(proj(C_GCONV, CONV_WIDTH))
        dst["ubuf"][HALO_U:HALO_U + TS, :] = proj(C_UPOOL, POOL_WIDTH)

    def finish_tile(cur):
        seq_start = j == 1
        ubuf, hbuf, kbuf, ve_buf, vo_buf, qbuf = (cur[n] for n in ("ubuf", "hbuf", "kbuf", "ve", "vo", "qbuf"))
        gp_buf, gc_buf, ga_buf = cur["gp"], cur["gc"], cur["ga"]

        inv_w = [jnp.where(low_half, 1.0 / POOL_WINDOWS[0], 1.0 / POOL_WINDOWS[1]).astype(_F32),
                 jnp.where(low_half, 1.0 / POOL_WINDOWS[2], 1.0 / POOL_WINDOWS[3]).astype(_F32)]

        def pool_chunk(c):
            r = c * CHUNK
            n_in = GROUPS + HALO_U // SUBLANES
            for col in range(2):
                lanes = slice(col * LANES, (col + 1) * LANES)
                u = [ubuf[pl.ds(r + SUBLANES * i, SUBLANES), lanes] for i in range(n_in)]
                s2 = [a + b for a, b in zip(u, _shift_down(u, 1, row))]
                s4 = [a + b for a, b in zip(s2, _shift_down(s2, 2, row))]
                if col == 0:
                    wide, narrow = s4, s2
                else:
                    s8 = [a + b for a, b in zip(s4, _shift_down(s4, 4, row))]
                    s16 = [s8[0]] + [s8[i] + s8[i - 1] for i in range(1, n_in)]
                    wide, narrow = s16, s8
                pieces = []
                for g in range(GROUPS):
                    i = g + HALO_U // SUBLANES
                    inv = inv_w[col]
                    if c == 0 and g < HALO_U // SUBLANES:
                        tab = invcnt_ref[SUBLANES * g:SUBLANES * (g + 1), lanes]
                        inv = jnp.where(seq_start, tab, inv)
                    pooled = jnp.where(low_half, narrow[i], wide[i]) * inv
                    pieces.append(pooled - u[i])
                zp_buf[pl.ds(r, CHUNK), lanes] = jnp.concatenate(pieces, axis=0).astype(_BF16)

        taps_by_b = [[] for _ in range(SUBLANES)]
        for k in range(CONV_KERNEL):
            a, b = divmod(k + HALO_H - (CONV_KERNEL - 1), SUBLANES)
            taps_by_b[b].append((a, k))
        n_h = GROUPS + HALO_H // SUBLANES

        def conv_chunk(c):
            r = c * CHUNK
            outs = []
            for col in range(2):
                lanes = slice(col * LANES, (col + 1) * LANES)
                h = [hbuf[pl.ds(r + SUBLANES * m, SUBLANES), lanes] for m in range(n_h)]
                acc = None
                for b in range(SUBLANES):
                    n_p = GROUPS if b == 0 else GROUPS + 1
                    part = []
                    for m in range(n_p):
                        p = None
                        for a, k in taps_by_b[b]:
                            term = h[m + a] * dw_ref[0, k:k + 1, lanes]
                            p = term if p is None else p + term
                        part.append(p)
                    if b == 0:
                        acc = part
                    else:
                        rolled = [pltpu.roll(p, SUBLANES - b, axis=0) for p in part]
                        keep = row < (SUBLANES - b)
                        acc = [acc[g] + jnp.where(keep, rolled[g], rolled[g + 1]) for g in range(GROUPS)]
                outs.append(jnp.concatenate(acc, axis=0))
            conv = jnp.concatenate(outs, axis=1) + cb_ref[layer:layer + 1, :]
            mu = jnp.mean(conv, axis=-1, keepdims=True)
            cen = conv - mu
            var = jnp.mean(cen * cen, axis=-1, keepdims=True)
            z = cen * lax.rsqrt(var + EPS) * clg_ref[layer:layer + 1, :] + clb_ref[layer:layer + 1, :]
            zc_buf[pl.ds(r, CHUNK), :] = _silu(z).astype(_BF16)

        for c in range(TS // CHUNK):
            conv_chunk(c)
        for c in range(TS // CHUNK):
            pool_chunk(c)

        top_rows = lax.broadcasted_iota(jnp.int32, (BF16_ROWS, 2 * LANES), 0)
        top_lanes = lax.broadcasted_iota(jnp.int32, (BF16_ROWS, 2 * LANES), 1)
        sink_value_slot = jnp.logical_and(top_rows == 0, top_lanes < LANES)
        sink_key_slot = lax.broadcasted_iota(jnp.int32, (BF16_ROWS, HEAD_DIM), 0) == 0

        def window_keys(kvh, r):
            top = kbuf[kvh, pl.ds(r, BF16_ROWS), :].astype(_F32)
            top = jnp.where(sink_key_slot, 0.0, top).astype(_BF16)
            return jnp.concatenate([top, kbuf[kvh, pl.ds(r + BF16_ROWS, 2 * BLOCK - BF16_ROWS), :]], axis=0)

        def scores(n):
            r = n * BLOCK
            table = seq_start.astype(jnp.int32) if n == 0 else 0
            out = []
            for kvh in range(N_KV_HEADS):
                k2 = window_keys(kvh, r)
                q4 = jnp.concatenate(
                    [qbuf[kvh * Q_PER_KV + g, pl.ds(r, BLOCK), :] for g in range(Q_PER_KV)], axis=0)
                s = lax.dot_general(q4, k2, (((1,), (1,)), ((), ())), preferred_element_type=_F32)
                out.append(s + bias_buf[table, kvh])
            return out

        def window_values(buf, kvh, r):
            top = buf[kvh, pl.ds(r, BF16_ROWS), :].astype(_F32)
            top = jnp.where(sink_value_slot, 0.0, top).astype(_BF16)
            return jnp.concatenate([top, buf[kvh, pl.ds(r + BF16_ROWS, 2 * BLOCK - BF16_ROWS), :]], axis=0)

        def softmax_pv(n, s_list):
            r = n * BLOCK
            for kvh in range(N_KV_HEADS):
                probs = []
                for g in range(Q_PER_KV):
                    sg = s_list[kvh][g * BLOCK:(g + 1) * BLOCK]
                    m = jnp.max(sg, axis=-1, keepdims=True)
                    probs.append(jnp.exp2(sg - m).astype(_BF16))
                p_even = jnp.concatenate([probs[0], probs[2]], axis=0)
                p_odd = jnp.concatenate([probs[1], probs[3]], axis=0)
                o = (jnp.dot(p_even, window_values(ve_buf, kvh, r), preferred_element_type=_F32)
                     + jnp.dot(p_odd, window_values(vo_buf, kvh, r), preferred_element_type=_F32))
                res = o[:, 0:LANES] / o[:, LANES:2 * LANES]
                for pair in range(2):
                    c0 = (kvh * 2 + pair) * LANES
                    gate = ga_buf[pl.ds(r, BLOCK), c0:c0 + LANES]
                    mix_ref[pl.ds(r, BLOCK), n_local + c0:n_local + c0 + LANES] = (
                        res[pair * BLOCK:(pair + 1) * BLOCK] * gate).astype(_BF16)

        def out_attn(r0, rows):
            for q in range(D_MODEL // MXU_COLS):
                cols = slice(q * MXU_COLS, (q + 1) * MXU_COLS)
                o_ref[0, pl.ds(r0, rows), cols] = xc_ref[0, pl.ds(r0, rows), cols] + jnp.dot(
                    mix_ref[pl.ds(r0, rows), n_local:D_MIX], wout_bf[n_local:D_MIX, cols],
                    preferred_element_type=_F32)

        s_cur = scores(0)
        for n in range(NB):
            s_next = scores(n + 1) if n + 1 < NB else None
            if n >= 2 and n % 2 == 0:
                out_attn((n - 2) * BLOCK, 2 * BLOCK)
            softmax_pv(n, s_cur)
            s_cur = s_next
        out_attn((NB - 2) * BLOCK, 2 * BLOCK)

        y_pool = jnp.dot(zp_buf[...], poolw_ref[0].astype(_BF16), preferred_element_type=_F32)
        y_pool = y_pool * pscale_ref[layer:layer + 1, :]
        mix_ref[:, 0:POOL_WIDTH] = (y_pool * gp_buf[...]).astype(_BF16)
        y_conv = jnp.dot(zc_buf[...], pw_ref[0].astype(_BF16), preferred_element_type=_F32)
        mix_ref[:, POOL_WIDTH:n_local] = (y_conv * gc_buf[...]).astype(_BF16)

        def final_chunk(c):
            r = c * CHUNK
            xc = o_ref[0, pl.ds(r, CHUNK), :]
            ms = jnp.mean(xc * xc, axis=-1, keepdims=True)
            o_ref[0, pl.ds(r, CHUNK), :] = xc * lax.rsqrt(ms + EPS) * fg_ref[...]

        half_rows = TS // 2
        for r0 in range(0, TS, half_rows):
            rows = pl.ds(r0, half_rows)
            for q in range(D_MODEL // MXU_COLS):
                cols = slice(q * MXU_COLS, (q + 1) * MXU_COLS)
                o_ref[0, rows, cols] = o_ref[0, rows, cols] + jnp.dot(
                    mix_ref[rows, 0:n_local], wout_bf[0:n_local, cols], preferred_element_type=_F32)
            if final:
                for c in range(r0 // CHUNK, (r0 + half_rows) // CHUNK):
                    final_chunk(c)

    parity = lax.rem(step, 2)

    @pl.when(parity == 0)
    def _():
        project_tile(set_a, set_b)
        finish_tile(set_b)

    @pl.when(parity == 1)
    def _():
        project_tile(set_b, set_a)
        finish_tile(set_a)


def _bias_table():
    i = np.arange(BLOCK)[:, None]
    jj = np.arange(2 * BLOCK)[None, :]
    dist = BLOCK + i - jj
    in_band = (dist >= 0) & (dist < WINDOW)
    slopes = np.asarray([2.0 ** (-8.0 * (h + 1) / N_Q_HEADS) for h in range(N_Q_HEADS)], dtype=np.float32)
    tab = np.empty((2, N_KV_HEADS, Q_PER_KV * BLOCK, 2 * BLOCK), np.float32)
    for start in range(2):
        valid = in_band & ~((jj < BLOCK) & (start == 1))
        for h in range(N_Q_HEADS):
            bias = (-slopes[h] * LOG2E) * dist.astype(np.float64)
            kvh, g = divmod(h, Q_PER_KV)
            tab[start, kvh, g * BLOCK:(g + 1) * BLOCK] = np.where(valid, bias, -1e30).astype(np.float32)
    return tab


def _inv_count_table():
    t = np.arange(HALO_U, dtype=np.float32)[:, None]
    gw = POOL_WIDTH // POOL_GROUPS
    wnd = np.repeat(np.asarray(POOL_WINDOWS, np.float32), gw)[None, :]
    return (1.0 / np.minimum(t + 1.0, wnd)).astype(np.float32)


def _layer_call(batch, seq, layer, final):
    n_tiles = seq // TS
    const = lambda shape: pl.BlockSpec(shape, lambda b, j: (0,) * len(shape), pipeline_mode=pl.Buffered(1))
    per_layer = lambda shape: pl.BlockSpec((1,) + shape, lambda b, j: (layer,) + (0,) * len(shape),
                                           pipeline_mode=pl.Buffered(1))
    projected = lambda b, j: (b, jnp.minimum(j, n_tiles - 1), 0)
    finished = lambda b, j: (b, jnp.maximum(j - 1, 0), 0)
    in_specs = [
        pl.BlockSpec((1, TS, D_MODEL), projected),
        pl.BlockSpec((1, TS, D_MODEL), finished),
        const((DEPTH, D_MODEL)),
        per_layer((D_MODEL, D_IN)),
        per_layer((POOL_WIDTH, POOL_WIDTH)),
        const((DEPTH, POOL_WIDTH)),
        const((HALO_U, POOL_WIDTH)),
        per_layer((CONV_KERNEL, CONV_WIDTH)),
        const((DEPTH, CONV_WIDTH)),
        const((DEPTH, CONV_WIDTH)),
        const((DEPTH, CONV_WIDTH)),
        per_layer((CONV_WIDTH, CONV_WIDTH)),
        pl.BlockSpec(memory_space=pltpu.SMEM),
        const((2, N_KV_HEADS, Q_PER_KV * BLOCK, 2 * BLOCK)),
        per_layer((D_MIX, D_MODEL)),
        const((1, D_MODEL)),
    ]
    set_scratch = [pltpu.VMEM(shape, dtype) for _, shape, dtype in _SET_SHAPES]
    scratch = set_scratch + set_scratch + [
        pltpu.VMEM((TS, D_MODEL), _BF16),
        pltpu.VMEM((TS, POOL_WIDTH), _BF16),
        pltpu.VMEM((TS, CONV_WIDTH), _BF16),
        pltpu.VMEM((TS, D_MIX), _BF16),
        pltpu.VMEM((2, N_KV_HEADS, Q_PER_KV * BLOCK, 2 * BLOCK), _F32),
        pltpu.VMEM((D_MODEL, D_IN), _BF16),
        pltpu.VMEM((D_MIX, D_MODEL), _BF16),
    ]
    return pl.pallas_call(
        functools.partial(_layer_body, layer=layer, final=final, n_tiles=n_tiles),
        out_shape=jax.ShapeDtypeStruct((batch, seq, D_MODEL), _F32),
        grid=(batch, n_tiles + 1),
        in_specs=in_specs,
        out_specs=pl.BlockSpec((1, TS, D_MODEL), finished),
        scratch_shapes=scratch,
        compiler_params=pltpu.CompilerParams(
            dimension_semantics=("arbitrary", "arbitrary"),
            vmem_limit_bytes=VMEM_LIMIT_BYTES,
        ),
        name="hybrid_layer_final" if final else "hybrid_layer",
    )


def kernel(x, ln_g, w_in, pool_w, pool_scale, conv_dw, conv_b, conv_ln_g, conv_ln_b, conv_pw, attn_sinks, w_out, final_g):
    batch, seq, d_model = x.shape
    assert d_model == D_MODEL and seq % TS == 0 and TS % BLOCK == 0
    bias = jnp.asarray(_bias_table())
    inv_cnt = jnp.asarray(_inv_count_table())
    poolw_bd = jnp.stack([jax.scipy.linalg.block_diag(*[pool_w[l, g] for g in range(POOL_GROUPS)])
                          for l in range(DEPTH)])
    params = (ln_g, w_in, poolw_bd, pool_scale, inv_cnt, conv_dw, conv_b, conv_ln_g, conv_ln_b, conv_pw,
              attn_sinks, bias, w_out, final_g.reshape(1, -1))
    for l in range(DEPTH):
        x = _layer_call(batch, seq, l, final=(l == DEPTH - 1))(x, x, *params)
    return x
```

```python
import functools

import numpy as np
import jax
import jax.numpy as jnp
from jax import lax
from jax.experimental import pallas as pl
from jax.experimental.pallas import tpu as pltpu

D_MODEL = 1024
DEPTH = 2
POOL_WIDTH = 256
POOL_GROUPS = 4
POOL_WINDOWS = (2, 4, 8, 16)
CONV_WIDTH = 256
CONV_KERNEL = 31
HEAD_DIM = 64
N_Q_HEADS = 8
N_KV_HEADS = 2
Q_PER_KV = N_Q_HEADS // N_KV_HEADS
ATTN_WIDTH = N_Q_HEADS * HEAD_DIM
KV_WIDTH = N_KV_HEADS * HEAD_DIM
WINDOW = 128
BLOCK = 128
D_MIX = POOL_WIDTH + CONV_WIDTH + ATTN_WIDTH
D_IN = 2 * POOL_WIDTH + 3 * CONV_WIDTH + 2 * ATTN_WIDTH + 2 * KV_WIDTH
EPS = 1e-6

C_UPOOL = 0
C_GPOOL = C_UPOOL + POOL_WIDTH
C_CA = C_GPOOL + POOL_WIDTH
C_CB = C_CA + CONV_WIDTH
C_GCONV = C_CB + CONV_WIDTH
C_Q = C_GCONV + CONV_WIDTH
C_K = C_Q + ATTN_WIDTH
C_V = C_K + KV_WIDTH
C_GATTN = C_V + KV_WIDTH

SUBLANES = 8
LANES = 128
BF16_ROWS = 16
MXU_COLS = 256
TS = 1024
NB = TS // BLOCK
OUT_BLOCKS = 4
CHUNK = 64
GROUPS = CHUNK // SUBLANES
HALO_U = 16
HALO_H = 32
VMEM_LIMIT_BYTES = 63 * 1024 * 1024
LOG2E = 1.4426950408889634

_F32 = jnp.float32
_BF16 = jnp.bfloat16


def _silu(v):
    return v * jax.nn.sigmoid(v)


def _unrolled(n, body):
    for c in range(n):
        body(c)


def _shift_down(groups, d, row):
    rolled = [pltpu.roll(g, d, axis=0) for g in groups]
    take_prev = row < d
    out = [rolled[0]]
    for i in range(1, len(groups)):
        out.append(jnp.where(take_prev, rolled[i - 1], rolled[i]))
    return out


def _layer_body(x_ref, lng_ref, win_ref, poolw_ref, pscale_ref, invcnt_ref, dw_ref, cb_ref, clg_ref, clb_ref,
                pw_ref, sinks_ref, bias_ref, wout_ref, fg_ref, o_ref,
                hn_ref, ubuf, hbuf, gp_buf, gc_buf, ga_buf, zp_buf, zc_buf, kbuf, ve_buf, vo_buf, qbuf, mix_ref,
                bias_buf, win_bf, wout_bf, *, layer, final):
    j = pl.program_id(1)
    first_tile = j == 0

    @pl.when(jnp.logical_and(pl.program_id(0) == 0, first_tile))
    def _():
        for c in range(D_IN // MXU_COLS):
            cols = slice(c * MXU_COLS, (c + 1) * MXU_COLS)
            win_bf[:, cols] = win_ref[0, :, cols].astype(_BF16)
        for c in range(D_MODEL // MXU_COLS):
            cols = slice(c * MXU_COLS, (c + 1) * MXU_COLS)
            wout_bf[:, cols] = wout_ref[0, :, cols].astype(_BF16)

    row = lax.broadcasted_iota(jnp.int32, (SUBLANES, LANES), 0)
    lane = lax.broadcasted_iota(jnp.int32, (SUBLANES, LANES), 1)
    low_half = lane < HEAD_DIM

    @pl.when(first_tile)
    def _():
        sink_slot = lax.broadcasted_iota(jnp.int32, (BLOCK, 2 * BLOCK), 1) == 0
        for start in range(2):
            for h in range(N_Q_HEADS):
                kvh, g = divmod(h, Q_PER_KV)
                rows = pl.ds(g * BLOCK, BLOCK)
                bias_buf[start, kvh, rows, :] = jnp.where(
                    sink_slot, sinks_ref[layer, h] * LOG2E, bias_ref[start, kvh, rows, :])
        ubuf[0:HALO_U, :] = jnp.zeros((HALO_U, POOL_WIDTH), _F32)
        hbuf[0:HALO_H, :] = jnp.zeros((HALO_H, CONV_WIDTH), _F32)
        kbuf[:, 0:BLOCK, :] = jnp.zeros((N_KV_HEADS, BLOCK, HEAD_DIM), _BF16)
        zeros = jnp.zeros((TS + BLOCK, LANES), _BF16)
        lane_t = lax.broadcasted_iota(jnp.int32, (TS + BLOCK, LANES), 1)
        ones_lo = jnp.where(lane_t < HEAD_DIM, 1.0, 0.0).astype(_BF16)
        ones_hi = jnp.where(lane_t < HEAD_DIM, 0.0, 1.0).astype(_BF16)
        for kvh in range(N_KV_HEADS):
            ve_buf[kvh, :, 0:LANES] = zeros
            vo_buf[kvh, :, 0:LANES] = zeros
            ve_buf[kvh, :, LANES:2 * LANES] = ones_lo
            vo_buf[kvh, :, LANES:2 * LANES] = ones_hi

    @pl.when(jnp.logical_not(first_tile))
    def _():
        ubuf[0:HALO_U, :] = ubuf[TS:TS + HALO_U, :]
        hbuf[0:HALO_H, :] = hbuf[TS:TS + HALO_H, :]
        kbuf[:, 0:BLOCK, :] = kbuf[:, TS:TS + BLOCK, :]
        ve_buf[:, 0:BLOCK, 0:LANES] = ve_buf[:, TS:TS + BLOCK, 0:LANES]
        vo_buf[:, 0:BLOCK, 0:LANES] = vo_buf[:, TS:TS + BLOCK, 0:LANES]

    def norm_chunk(c):
        r = c * CHUNK
        xc = x_ref[0, pl.ds(r, CHUNK), :]
        ms = jnp.mean(xc * xc, axis=-1, keepdims=True)
        y = xc * lax.rsqrt(ms + EPS) * lng_ref[layer:layer + 1, :]
        hn_ref[pl.ds(r, CHUNK), :] = y.astype(_BF16)

    _unrolled(TS // CHUNK, norm_chunk)

    def proj(c0, width, r0=0, rows=TS):
        w = win_bf[:, c0:c0 + width]
        return jnp.dot(hn_ref[pl.ds(r0, rows), :], w, preferred_element_type=_F32)

    for r0 in range(0, TS, TS // 2):
        rows = pl.ds(HALO_H + r0, TS // 2)
        hbuf[rows, :] = proj(C_CA, CONV_WIDTH, r0, TS // 2)
        hbuf[rows, :] = hbuf[rows, :] * jax.nn.sigmoid(proj(C_CB, CONV_WIDTH, r0, TS // 2))

    kv = proj(C_K, 2 * KV_WIDTH)
    keys = kv[:, 0:KV_WIDTH]
    vals = kv[:, KV_WIDTH:2 * KV_WIDTH]
    vals_swapped = pltpu.roll(vals, HEAD_DIM, axis=1)
    lo = lax.broadcasted_iota(jnp.int32, (TS, LANES), 1) < HEAD_DIM
    for kvh in range(N_KV_HEADS):
        kbuf[kvh, BLOCK:BLOCK + TS, :] = keys[:, kvh * HEAD_DIM:(kvh + 1) * HEAD_DIM].astype(_BF16)
        own_lo, own_hi = (vals, vals_swapped) if kvh == 0 else (vals_swapped, vals)
        ve_buf[kvh, BLOCK:BLOCK + TS, 0:LANES] = jnp.where(lo, own_lo, 0.0).astype(_BF16)
        vo_buf[kvh, BLOCK:BLOCK + TS, 0:LANES] = jnp.where(lo, 0.0, own_hi).astype(_BF16)

    heads_per_group = MXU_COLS // HEAD_DIM
    for half in range(ATTN_WIDTH // MXU_COLS):
        qh = proj(C_Q + half * MXU_COLS, MXU_COLS) * (HEAD_DIM ** -0.5 * LOG2E)
        for i in range(heads_per_group):
            qbuf[half * heads_per_group + i, :, :] = qh[:, i * HEAD_DIM:(i + 1) * HEAD_DIM].astype(_BF16)
    for half in range(ATTN_WIDTH // MXU_COLS):
        ga_buf[:, half * MXU_COLS:(half + 1) * MXU_COLS] = _silu(proj(C_GATTN + half * MXU_COLS, MXU_COLS))

    ubuf[HALO_U:HALO_U + TS, :] = proj(C_UPOOL, POOL_WIDTH)
    gp_buf[...] = _silu(proj(C_GPOOL, POOL_WIDTH))
    gc_buf[...] = _silu(proj(C_GCONV, CONV_WIDTH))

    inv_w = [jnp.where(low_half, 1.0 / POOL_WINDOWS[0], 1.0 / POOL_WINDOWS[1]).astype(_F32),
             jnp.where(low_half, 1.0 / POOL_WINDOWS[2], 1.0 / POOL_WINDOWS[3]).astype(_F32)]

    def pool_chunk(c):
        r = c * CHUNK
        n_in = GROUPS + HALO_U // SUBLANES
        for col in range(2):
            lanes = slice(col * LANES, (col + 1) * LANES)
            u = [ubuf[pl.ds(r + SUBLANES * i, SUBLANES), lanes] for i in range(n_in)]
            s2 = [a + b for a, b in zip(u, _shift_down(u, 1, row))]
            s4 = [a + b for a, b in zip(s2, _shift_down(s2, 2, row))]
            if col == 0:
                wide, narrow = s4, s2
            else:
                s8 = [a + b for a, b in zip(s4, _shift_down(s4, 4, row))]
                s16 = [s8[0]] + [s8[i] + s8[i - 1] for i in range(1, n_in)]
                wide, narrow = s16, s8
            pieces = []
            for g in range(GROUPS):
                i = g + HALO_U // SUBLANES
                inv = inv_w[col]
                if c == 0 and g < HALO_U // SUBLANES:
                    tab = invcnt_ref[SUBLANES * g:SUBLANES * (g + 1), lanes]
                    inv = jnp.where(first_tile, tab, inv)
                pooled = jnp.where(low_half, narrow[i], wide[i]) * inv
                pieces.append(pooled - u[i])
            zp_buf[pl.ds(r, CHUNK), lanes] = jnp.concatenate(pieces, axis=0).astype(_BF16)

    taps_by_b = [[] for _ in range(SUBLANES)]
    for k in range(CONV_KERNEL):
        a, b = divmod(k + HALO_H - (CONV_KERNEL - 1), SUBLANES)
        taps_by_b[b].append((a, k))
    n_h = GROUPS + HALO_H // SUBLANES

    def conv_chunk(c):
        r = c * CHUNK
        outs = []
        for col in range(2):
            lanes = slice(col * LANES, (col + 1) * LANES)
            h = [hbuf[pl.ds(r + SUBLANES * m, SUBLANES), lanes] for m in range(n_h)]
            acc = None
            for b in range(SUBLANES):
                n_p = GROUPS if b == 0 else GROUPS + 1
                part = []
                for m in range(n_p):
                    p = None
                    for a, k in taps_by_b[b]:
                        term = h[m + a] * dw_ref[0, k:k + 1, lanes]
                        p = term if p is None else p + term
                    part.append(p)
                if b == 0:
                    acc = part
                else:
                    rolled = [pltpu.roll(p, SUBLANES - b, axis=0) for p in part]
                    keep = row < (SUBLANES - b)
                    acc = [acc[g] + jnp.where(keep, rolled[g], rolled[g + 1]) for g in range(GROUPS)]
            outs.append(jnp.concatenate(acc, axis=0))
        conv = jnp.concatenate(outs, axis=1) + cb_ref[layer:layer + 1, :]
        mu = jnp.mean(conv, axis=-1, keepdims=True)
        cen = conv - mu
        var = jnp.mean(cen * cen, axis=-1, keepdims=True)
        z = cen * lax.rsqrt(var + EPS) * clg_ref[layer:layer + 1, :] + clb_ref[layer:layer + 1, :]
        zc_buf[pl.ds(r, CHUNK), :] = _silu(z).astype(_BF16)

    _unrolled(TS // CHUNK, conv_chunk)
    _unrolled(TS // CHUNK, pool_chunk)

    n_local = POOL_WIDTH + CONV_WIDTH

    top_rows = lax.broadcasted_iota(jnp.int32, (BF16_ROWS, 2 * LANES), 0)
    top_lanes = lax.broadcasted_iota(jnp.int32, (BF16_ROWS, 2 * LANES), 1)
    sink_value_slot = jnp.logical_and(top_rows == 0, top_lanes < LANES)
    sink_key_slot = lax.broadcasted_iota(jnp.int32, (BF16_ROWS, HEAD_DIM), 0) == 0

    def window_keys(kvh, r):
        top = kbuf[kvh, pl.ds(r, BF16_ROWS), :].astype(_F32)
        top = jnp.where(sink_key_slot, 0.0, top).astype(_BF16)
        return jnp.concatenate([top, kbuf[kvh, pl.ds(r + BF16_ROWS, 2 * BLOCK - BF16_ROWS), :]], axis=0)

    def scores(n):
        r = n * BLOCK
        seq_start = first_tile.astype(jnp.int32) if n == 0 else 0
        out = []
        for kvh in range(N_KV_HEADS):
            k2 = window_keys(kvh, r)
            q4 = jnp.concatenate([qbuf[kvh * Q_PER_KV + g, pl.ds(r, BLOCK), :] for g in range(Q_PER_KV)], axis=0)
            s = lax.dot_general(q4, k2, (((1,), (1,)), ((), ())), preferred_element_type=_F32)
            out.append(s + bias_buf[seq_start, kvh])
        return out

    def window_values(buf, kvh, r):
        top = buf[kvh, pl.ds(r, BF16_ROWS), :].astype(_F32)
        top = jnp.where(sink_value_slot, 0.0, top).astype(_BF16)
        return jnp.concatenate([top, buf[kvh, pl.ds(r + BF16_ROWS, 2 * BLOCK - BF16_ROWS), :]], axis=0)

    def softmax_pv(n, s_list):
        r = n * BLOCK
        for kvh in range(N_KV_HEADS):
            probs = []
            for g in range(Q_PER_KV):
                sg = s_list[kvh][g * BLOCK:(g + 1) * BLOCK]
                m = jnp.max(sg, axis=-1, keepdims=True)
                probs.append(jnp.exp2(sg - m).astype(_BF16))
            p_even = jnp.concatenate([probs[0], probs[2]], axis=0)
            p_odd = jnp.concatenate([probs[1], probs[3]], axis=0)
            o = (jnp.dot(p_even, window_values(ve_buf, kvh, r), preferred_element_type=_F32)
                 + jnp.dot(p_odd, window_values(vo_buf, kvh, r), preferred_element_type=_F32))
            res = o[:, 0:LANES] / o[:, LANES:2 * LANES]
            for pair in range(2):
                c0 = (kvh * 2 + pair) * LANES
                gate = ga_buf[pl.ds(r, BLOCK), c0:c0 + LANES]
                mix_ref[pl.ds(r, BLOCK), n_local + c0:n_local + c0 + LANES] = (
                    res[pair * BLOCK:(pair + 1) * BLOCK] * gate).astype(_BF16)

    def out_attn(r0, rows):
        for q in range(D_MODEL // MXU_COLS):
            cols = slice(q * MXU_COLS, (q + 1) * MXU_COLS)
            o_ref[0, pl.ds(r0, rows), cols] = x_ref[0, pl.ds(r0, rows), cols] + jnp.dot(
                mix_ref[pl.ds(r0, rows), n_local:D_MIX], wout_bf[n_local:D_MIX, cols],
                preferred_element_type=_F32)

    s_cur = scores(0)
    for n in range(NB):
        s_next = scores(n + 1) if n + 1 < NB else None
        if n >= OUT_BLOCKS and n % OUT_BLOCKS == 0:
            out_attn((n - OUT_BLOCKS) * BLOCK, OUT_BLOCKS * BLOCK)
        softmax_pv(n, s_cur)
        s_cur = s_next
    out_attn((NB - OUT_BLOCKS) * BLOCK, OUT_BLOCKS * BLOCK)

    y_pool = jnp.dot(zp_buf[...], poolw_ref[0].astype(_BF16), preferred_element_type=_F32)
    y_pool = y_pool * pscale_ref[layer:layer + 1, :]
    mix_ref[:, 0:POOL_WIDTH] = (y_pool * gp_buf[...]).astype(_BF16)
    y_conv = jnp.dot(zc_buf[...], pw_ref[0].astype(_BF16), preferred_element_type=_F32)
    mix_ref[:, POOL_WIDTH:POOL_WIDTH + CONV_WIDTH] = (y_conv * gc_buf[...]).astype(_BF16)

    def final_chunk(c):
        r = c * CHUNK
        xc = o_ref[0, pl.ds(r, CHUNK), :]
        ms = jnp.mean(xc * xc, axis=-1, keepdims=True)
        o_ref[0, pl.ds(r, CHUNK), :] = xc * lax.rsqrt(ms + EPS) * fg_ref[...]

    half_rows = TS // 2
    for r0 in range(0, TS, half_rows):
        rows = pl.ds(r0, half_rows)
        for q in range(D_MODEL // MXU_COLS):
            cols = slice(q * MXU_COLS, (q + 1) * MXU_COLS)
            o_ref[0, rows, cols] = o_ref[0, rows, cols] + jnp.dot(
                mix_ref[rows, 0:n_local], wout_bf[0:n_local, cols], preferred_element_type=_F32)
        if final:
            for c in range(r0 // CHUNK, (r0 + half_rows) // CHUNK):
                final_chunk(c)


def _bias_table():
    i = np.arange(BLOCK)[:, None]
    jj = np.arange(2 * BLOCK)[None, :]
    dist = BLOCK + i - jj
    in_band = (dist >= 0) & (dist < WINDOW)
    slopes = np.asarray([2.0 ** (-8.0 * (h + 1) / N_Q_HEADS) for h in range(N_Q_HEADS)], dtype=np.float32)
    tab = np.empty((2, N_KV_HEADS, Q_PER_KV * BLOCK, 2 * BLOCK), np.float32)
    for start in range(2):
        valid = in_band & ~((jj < BLOCK) & (start == 1))
        for h in range(N_Q_HEADS):
            bias = (-slopes[h] * LOG2E) * dist.astype(np.float64)
            kvh, g = divmod(h, Q_PER_KV)
            tab[start, kvh, g * BLOCK:(g + 1) * BLOCK] = np.where(valid, bias, -1e30).astype(np.float32)
    return tab


def _inv_count_table():
    t = np.arange(HALO_U, dtype=np.float32)[:, None]
    gw = POOL_WIDTH // POOL_GROUPS
    wnd = np.repeat(np.asarray(POOL_WINDOWS, np.float32), gw)[None, :]
    return (1.0 / np.minimum(t + 1.0, wnd)).astype(np.float32)


def _layer_call(batch, seq, layer, final):
    n_tiles = seq // TS
    const = lambda shape: pl.BlockSpec(shape, lambda b, j: (0,) * len(shape), pipeline_mode=pl.Buffered(1))
    per_layer = lambda shape: pl.BlockSpec((1,) + shape, lambda b, j: (layer,) + (0,) * len(shape),
                                           pipeline_mode=pl.Buffered(1))
    in_specs = [
        pl.BlockSpec((1, TS, D_MODEL), lambda b, j: (b, j, 0)),
        const((DEPTH, D_MODEL)),
        per_layer((D_MODEL, D_IN)),
        per_layer((POOL_WIDTH, POOL_WIDTH)),
        const((DEPTH, POOL_WIDTH)),
        const((HALO_U, POOL_WIDTH)),
        per_layer((CONV_KERNEL, CONV_WIDTH)),
        const((DEPTH, CONV_WIDTH)),
        const((DEPTH, CONV_WIDTH)),
        const((DEPTH, CONV_WIDTH)),
        per_layer((CONV_WIDTH, CONV_WIDTH)),
        pl.BlockSpec(memory_space=pltpu.SMEM),
        const((2, N_KV_HEADS, Q_PER_KV * BLOCK, 2 * BLOCK)),
        per_layer((D_MIX, D_MODEL)),
        const((1, D_MODEL)),
    ]
    scratch = [
        pltpu.VMEM((TS, D_MODEL), _BF16),
        pltpu.VMEM((TS + HALO_U, POOL_WIDTH), _F32),
        pltpu.VMEM((TS + HALO_H, CONV_WIDTH), _F32),
        pltpu.VMEM((TS, POOL_WIDTH), _F32),
        pltpu.VMEM((TS, CONV_WIDTH), _F32),
        pltpu.VMEM((TS, ATTN_WIDTH), _F32),
        pltpu.VMEM((TS, POOL_WIDTH), _BF16),
        pltpu.VMEM((TS, CONV_WIDTH), _BF16),
        pltpu.VMEM((N_KV_HEADS, TS + BLOCK, HEAD_DIM), _BF16),
        pltpu.VMEM((N_KV_HEADS, TS + BLOCK, 2 * LANES), _BF16),
        pltpu.VMEM((N_KV_HEADS, TS + BLOCK, 2 * LANES), _BF16),
        pltpu.VMEM((N_Q_HEADS, TS, HEAD_DIM), _BF16),
        pltpu.VMEM((TS, D_MIX), _BF16),
        pltpu.VMEM((2, N_KV_HEADS, Q_PER_KV * BLOCK, 2 * BLOCK), _F32),
        pltpu.VMEM((D_MODEL, D_IN), _BF16),
        pltpu.VMEM((D_MIX, D_MODEL), _BF16),
    ]
    return pl.pallas_call(
        functools.partial(_layer_body, layer=layer, final=final),
        out_shape=jax.ShapeDtypeStruct((batch, seq, D_MODEL), _F32),
        grid=(batch, n_tiles),
        in_specs=in_specs,
        out_specs=pl.BlockSpec((1, TS, D_MODEL), lambda b, j: (b, j, 0)),
        scratch_shapes=scratch,
        compiler_params=pltpu.CompilerParams(
            dimension_semantics=("arbitrary", "arbitrary"),
            vmem_limit_bytes=VMEM_LIMIT_BYTES,
        ),
        name="hybrid_layer_final" if final else "hybrid_layer",
    )


def kernel(x, ln_g, w_in, pool_w, pool_scale, conv_dw, conv_b, conv_ln_g, conv_ln_b, conv_pw, attn_sinks, w_out, final_g):
    batch, seq, d_model = x.shape
    assert d_model == D_MODEL and seq % TS == 0 and TS % BLOCK == 0
    bias = jnp.asarray(_bias_table())
    inv_cnt = jnp.asarray(_inv_count_table())
    poolw_bd = jnp.stack([jax.scipy.linalg.block_diag(*[pool_w[l, g] for g in range(POOL_GROUPS)])
                          for l in range(DEPTH)])
    params = (ln_g, w_in, poolw_bd, pool_scale, inv_cnt, conv_dw, conv_b, conv_ln_g, conv_ln_b, conv_pw,
              attn_sinks, bias, w_out, final_g.reshape(1, -1))
    for l in range(DEPTH):
        x = _layer_call(batch, seq, l, final=(l == DEPTH - 1))(x, *params)
    return x
```

```python
import functools

import numpy as np
import jax
import jax.numpy as jnp
from jax import lax
from jax.experimental import pallas as pl
from jax.experimental.pallas import tpu as pltpu

D_MODEL = 1024
DEPTH = 2
POOL_WIDTH = 256
POOL_GROUPS = 4
POOL_WINDOWS = (2, 4, 8, 16)
CONV_WIDTH = 256
CONV_KERNEL = 31
HEAD_DIM = 64
N_Q_HEADS = 8
N_KV_HEADS = 2
Q_PER_KV = N_Q_HEADS // N_KV_HEADS
ATTN_WIDTH = N_Q_HEADS * HEAD_DIM
KV_WIDTH = N_KV_HEADS * HEAD_DIM
WINDOW = 128
BLOCK = 128
D_MIX = POOL_WIDTH + CONV_WIDTH + ATTN_WIDTH
D_IN = 2 * POOL_WIDTH + 3 * CONV_WIDTH + 2 * ATTN_WIDTH + 2 * KV_WIDTH
EPS = 1e-6

C_UPOOL = 0
C_GPOOL = C_UPOOL + POOL_WIDTH
C_CA = C_GPOOL + POOL_WIDTH
C_CB = C_CA + CONV_WIDTH
C_GCONV = C_CB + CONV_WIDTH
C_Q = C_GCONV + CONV_WIDTH
C_K = C_Q + ATTN_WIDTH
C_V = C_K + KV_WIDTH
C_GATTN = C_V + KV_WIDTH

SUBLANES = 8
LANES = 128
BF16_ROWS = 16
MXU_COLS = 256
TS = 1024
NB = TS // BLOCK
OUT_BLOCKS = 4
CHUNK = 64
GROUPS = CHUNK // SUBLANES
HALO_U = 16
HALO_H = 32
VMEM_LIMIT_BYTES = 63 * 1024 * 1024
LOG2E = 1.4426950408889634

_F32 = jnp.float32
_BF16 = jnp.bfloat16


def _silu(v):
    return v * jax.nn.sigmoid(v)


def _unrolled(n, body):
    for c in range(n):
        body(c)


def _shift_down(groups, d, row):
    rolled = [pltpu.roll(g, d, axis=0) for g in groups]
    take_prev = row < d
    out = [rolled[0]]
    for i in range(1, len(groups)):
        out.append(jnp.where(take_prev, rolled[i - 1], rolled[i]))
    return out


def _layer_body(x_ref, lng_ref, win_ref, poolw_ref, pscale_ref, invcnt_ref, dw_ref, cb_ref, clg_ref, clb_ref,
                pw_ref, sinks_ref, bias_ref, wout_ref, fg_ref, o_ref,
                hn_ref, ubuf, hbuf, gp_buf, gc_buf, ga_buf, zp_buf, zc_buf, kbuf, ve_buf, vo_buf, qbuf, mix_ref,
                bias_buf, win_bf, wout_bf, *, layer, final):
    j = pl.program_id(1)
    first_tile = j == 0

    @pl.when(jnp.logical_and(pl.program_id(0) == 0, first_tile))
    def _():
        for c in range(D_IN // MXU_COLS):
            cols = slice(c * MXU_COLS, (c + 1) * MXU_COLS)
            win_bf[:, cols] = win_ref[0, :, cols].astype(_BF16)
        for c in range(D_MODEL // MXU_COLS):
            cols = slice(c * MXU_COLS, (c + 1) * MXU_COLS)
            wout_bf[:, cols] = wout_ref[0, :, cols].astype(_BF16)

    row = lax.broadcasted_iota(jnp.int32, (SUBLANES, LANES), 0)
    lane = lax.broadcasted_iota(jnp.int32, (SUBLANES, LANES), 1)
    low_half = lane < HEAD_DIM

    @pl.when(first_tile)
    def _():
        sink_slot = lax.broadcasted_iota(jnp.int32, (BLOCK, 2 * BLOCK), 1) == 0
        for start in range(2):
            for h in range(N_Q_HEADS):
                kvh, g = divmod(h, Q_PER_KV)
                rows = pl.ds(g * BLOCK, BLOCK)
                bias_buf[start, kvh, rows, :] = jnp.where(
                    sink_slot, sinks_ref[layer, h] * LOG2E, bias_ref[start, kvh, rows, :])
        ubuf[0:HALO_U, :] = jnp.zeros((HALO_U, POOL_WIDTH), _F32)
        hbuf[:, 0:HALO_H, :] = jnp.zeros((CONV_WIDTH // LANES, HALO_H, LANES), _F32)
        kbuf[:, 0:BLOCK, :] = jnp.zeros((N_KV_HEADS, BLOCK, HEAD_DIM), _BF16)
        zeros = jnp.zeros((TS + BLOCK, LANES), _BF16)
        lane_t = lax.broadcasted_iota(jnp.int32, (TS + BLOCK, LANES), 1)
        ones_lo = jnp.where(lane_t < HEAD_DIM, 1.0, 0.0).astype(_BF16)
        ones_hi = jnp.where(lane_t < HEAD_DIM, 0.0, 1.0).astype(_BF16)
        for kvh in range(N_KV_HEADS):
            ve_buf[kvh, :, 0:LANES] = zeros
            vo_buf[kvh, :, 0:LANES] = zeros
            ve_buf[kvh, :, LANES:2 * LANES] = ones_lo
            vo_buf[kvh, :, LANES:2 * LANES] = ones_hi

    @pl.when(jnp.logical_not(first_tile))
    def _():
        ubuf[0:HALO_U, :] = ubuf[TS:TS + HALO_U, :]
        hbuf[:, 0:HALO_H, :] = hbuf[:, TS:TS + HALO_H, :]
        kbuf[:, 0:BLOCK, :] = kbuf[:, TS:TS + BLOCK, :]
        ve_buf[:, 0:BLOCK, 0:LANES] = ve_buf[:, TS:TS + BLOCK, 0:LANES]
        vo_buf[:, 0:BLOCK, 0:LANES] = vo_buf[:, TS:TS + BLOCK, 0:LANES]

    def norm_chunk(c):
        r = c * CHUNK
        xc = x_ref[0, pl.ds(r, CHUNK), :]
        ms = jnp.mean(xc * xc, axis=-1, keepdims=True)
        y = xc * lax.rsqrt(ms + EPS) * lng_ref[layer:layer + 1, :]
        hn_ref[pl.ds(r, CHUNK), :] = y.astype(_BF16)

    _unrolled(TS // CHUNK, norm_chunk)

    def proj(c0, width, r0=0, rows=TS):
        w = win_bf[:, c0:c0 + width]
        return jnp.dot(hn_ref[pl.ds(r0, rows), :], w, preferred_element_type=_F32)

    for r0 in range(0, TS, TS // 2):
        rows = pl.ds(HALO_H + r0, TS // 2)
        glu = proj(C_CA, CONV_WIDTH, r0, TS // 2) * jax.nn.sigmoid(proj(C_CB, CONV_WIDTH, r0, TS // 2))
        for col in range(CONV_WIDTH // LANES):
            hbuf[col, rows, :] = glu[:, col * LANES:(col + 1) * LANES]

    kv = proj(C_K, 2 * KV_WIDTH)
    keys = kv[:, 0:KV_WIDTH]
    vals = kv[:, KV_WIDTH:2 * KV_WIDTH]
    vals_swapped = pltpu.roll(vals, HEAD_DIM, axis=1)
    lo = lax.broadcasted_iota(jnp.int32, (TS, LANES), 1) < HEAD_DIM
    for kvh in range(N_KV_HEADS):
        kbuf[kvh, BLOCK:BLOCK + TS, :] = keys[:, kvh * HEAD_DIM:(kvh + 1) * HEAD_DIM].astype(_BF16)
        own_lo, own_hi = (vals, vals_swapped) if kvh == 0 else (vals_swapped, vals)
        ve_buf[kvh, BLOCK:BLOCK + TS, 0:LANES] = jnp.where(lo, own_lo, 0.0).astype(_BF16)
        vo_buf[kvh, BLOCK:BLOCK + TS, 0:LANES] = jnp.where(lo, 0.0, own_hi).astype(_BF16)

    heads_per_group = MXU_COLS // HEAD_DIM
    for half in range(ATTN_WIDTH // MXU_COLS):
        qh = proj(C_Q + half * MXU_COLS, MXU_COLS) * (HEAD_DIM ** -0.5 * LOG2E)
        for i in range(heads_per_group):
            qbuf[half * heads_per_group + i, :, :] = qh[:, i * HEAD_DIM:(i + 1) * HEAD_DIM].astype(_BF16)
    for half in range(ATTN_WIDTH // MXU_COLS):
        ga_buf[:, half * MXU_COLS:(half + 1) * MXU_COLS] = _silu(proj(C_GATTN + half * MXU_COLS, MXU_COLS))

    ubuf[HALO_U:HALO_U + TS, :] = proj(C_UPOOL, POOL_WIDTH)
    gp_buf[...] = _silu(proj(C_GPOOL, POOL_WIDTH))
    gc_buf[...] = _silu(proj(C_GCONV, CONV_WIDTH))

    inv_w = [jnp.where(low_half, 1.0 / POOL_WINDOWS[0], 1.0 / POOL_WINDOWS[1]).astype(_F32),
             jnp.where(low_half, 1.0 / POOL_WINDOWS[2], 1.0 / POOL_WINDOWS[3]).astype(_F32)]

    def pool_chunk(c):
        r = c * CHUNK
        n_in = GROUPS + HALO_U // SUBLANES
        for col in range(2):
            lanes = slice(col * LANES, (col + 1) * LANES)
            u = [ubuf[pl.ds(r + SUBLANES * i, SUBLANES), lanes] for i in range(n_in)]
            s2 = [a + b for a, b in zip(u, _shift_down(u, 1, row))]
            s4 = [a + b for a, b in zip(s2, _shift_down(s2, 2, row))]
            if col == 0:
                wide, narrow = s4, s2
            else:
                s8 = [a + b for a, b in zip(s4, _shift_down(s4, 4, row))]
                s16 = [s8[0]] + [s8[i] + s8[i - 1] for i in range(1, n_in)]
                wide, narrow = s16, s8
            pieces = []
            for g in range(GROUPS):
                i = g + HALO_U // SUBLANES
                inv = inv_w[col]
                if c == 0 and g < HALO_U // SUBLANES:
                    tab = invcnt_ref[SUBLANES * g:SUBLANES * (g + 1), lanes]
                    inv = jnp.where(first_tile, tab, inv)
                pooled = jnp.where(low_half, narrow[i], wide[i]) * inv
                pieces.append(pooled - u[i])
            zp_buf[pl.ds(r, CHUNK), lanes] = jnp.concatenate(pieces, axis=0).astype(_BF16)

    tap0 = HALO_H - (CONV_KERNEL - 1)
    half = CHUNK // 2

    def conv_chunk(c):
        r = c * CHUNK
        for parity in range(2):
            outs = []
            for col in range(CONV_WIDTH // LANES):
                lanes = slice(col * LANES, (col + 1) * LANES)
                acc = None
                for k in range(CONV_KERNEL):
                    taps = hbuf[col, pl.ds(r + parity + k + tap0, half, stride=2), :]
                    term = taps * dw_ref[0, k:k + 1, lanes]
                    acc = term if acc is None else acc + term
                outs.append(acc)
            conv = jnp.concatenate(outs, axis=1) + cb_ref[layer:layer + 1, :]
            mu = jnp.mean(conv, axis=-1, keepdims=True)
            cen = conv - mu
            var = jnp.mean(cen * cen, axis=-1, keepdims=True)
            z = cen * lax.rsqrt(var + EPS) * clg_ref[layer:layer + 1, :] + clb_ref[layer:layer + 1, :]
            act = _silu(z)
            for col in range(CONV_WIDTH // LANES):
                zc_buf[col, pl.ds(r + parity, half, stride=2), :] = act[:, col * LANES:(col + 1) * LANES]

    _unrolled(TS // CHUNK, conv_chunk)
    _unrolled(TS // CHUNK, pool_chunk)

    n_local = POOL_WIDTH + CONV_WIDTH

    top_rows = lax.broadcasted_iota(jnp.int32, (BF16_ROWS, 2 * LANES), 0)
    top_lanes = lax.broadcasted_iota(jnp.int32, (BF16_ROWS, 2 * LANES), 1)
    sink_value_slot = jnp.logical_and(top_rows == 0, top_lanes < LANES)
    sink_key_slot = lax.broadcasted_iota(jnp.int32, (BF16_ROWS, HEAD_DIM), 0) == 0

    def window_keys(kvh, r):
        top = kbuf[kvh, pl.ds(r, BF16_ROWS), :].astype(_F32)
        top = jnp.where(sink_key_slot, 0.0, top).astype(_BF16)
        return jnp.concatenate([top, kbuf[kvh, pl.ds(r + BF16_ROWS, 2 * BLOCK - BF16_ROWS), :]], axis=0)

    def scores(n):
        r = n * BLOCK
        seq_start = first_tile.astype(jnp.int32) if n == 0 else 0
        out = []
        for kvh in range(N_KV_HEADS):
            k2 = window_keys(kvh, r)
            q4 = jnp.concatenate([qbuf[kvh * Q_PER_KV + g, pl.ds(r, BLOCK), :] for g in range(Q_PER_KV)], axis=0)
            s = lax.dot_general(q4, k2, (((1,), (1,)), ((), ())), preferred_element_type=_F32)
            out.append(s + bias_buf[seq_start, kvh])
        return out

    def window_values(buf, kvh, r):
        top = buf[kvh, pl.ds(r, BF16_ROWS), :].astype(_F32)
        top = jnp.where(sink_value_slot, 0.0, top).astype(_BF16)
        return jnp.concatenate([top, buf[kvh, pl.ds(r + BF16_ROWS, 2 * BLOCK - BF16_ROWS), :]], axis=0)

    def softmax_pv(n, s_list):
        r = n * BLOCK
        for kvh in range(N_KV_HEADS):
            probs = []
            for g in range(Q_PER_KV):
                sg = s_list[kvh][g * BLOCK:(g + 1) * BLOCK]
                m = jnp.max(sg, axis=-1, keepdims=True)
                probs.append(jnp.exp2(sg - m).astype(_BF16))
            p_even = jnp.concatenate([probs[0], probs[2]], axis=0)
            p_odd = jnp.concatenate([probs[1], probs[3]], axis=0)
            o = (jnp.dot(p_even, window_values(ve_buf, kvh, r), preferred_element_type=_F32)
                 + jnp.dot(p_odd, window_values(vo_buf, kvh, r), preferred_element_type=_F32))
            res = o[:, 0:LANES] / o[:, LANES:2 * LANES]
            for pair in range(2):
                c0 = (kvh * 2 + pair) * LANES
                gate = ga_buf[pl.ds(r, BLOCK), c0:c0 + LANES]
                mix_ref[pl.ds(r, BLOCK), n_local + c0:n_local + c0 + LANES] = (
                    res[pair * BLOCK:(pair + 1) * BLOCK] * gate).astype(_BF16)

    def out_attn(r0, rows):
        for q in range(D_MODEL // MXU_COLS):
            cols = slice(q * MXU_COLS, (q + 1) * MXU_COLS)
            o_ref[0, pl.ds(r0, rows), cols] = x_ref[0, pl.ds(r0, rows), cols] + jnp.dot(
                mix_ref[pl.ds(r0, rows), n_local:D_MIX], wout_bf[n_local:D_MIX, cols],
                preferred_element_type=_F32)

    s_cur = scores(0)
    for n in range(NB):
        s_next = scores(n + 1) if n + 1 < NB else None
        if n >= OUT_BLOCKS and n % OUT_BLOCKS == 0:
            out_attn((n - OUT_BLOCKS) * BLOCK, OUT_BLOCKS * BLOCK)
        softmax_pv(n, s_cur)
        s_cur = s_next
    out_attn((NB - OUT_BLOCKS) * BLOCK, OUT_BLOCKS * BLOCK)

    y_pool = jnp.dot(zp_buf[...], poolw_ref[0].astype(_BF16), preferred_element_type=_F32)
    y_pool = y_pool * pscale_ref[layer:layer + 1, :]
    mix_ref[:, 0:POOL_WIDTH] = (y_pool * gp_buf[...]).astype(_BF16)
    conv_act = jnp.concatenate([zc_buf[col] for col in range(CONV_WIDTH // LANES)], axis=1).astype(_BF16)
    y_conv = jnp.dot(conv_act, pw_ref[0].astype(_BF16), preferred_element_type=_F32)
    mix_ref[:, POOL_WIDTH:POOL_WIDTH + CONV_WIDTH] = (y_conv * gc_buf[...]).astype(_BF16)

    def final_chunk(c):
        r = c * CHUNK
        xc = o_ref[0, pl.ds(r, CHUNK), :]
        ms = jnp.mean(xc * xc, axis=-1, keepdims=True)
        o_ref[0, pl.ds(r, CHUNK), :] = xc * lax.rsqrt(ms + EPS) * fg_ref[...]

    half_rows = TS // 2
    for r0 in range(0, TS, half_rows):
        rows = pl.ds(r0, half_rows)
        for q in range(D_MODEL // MXU_COLS):
            cols = slice(q * MXU_COLS, (q + 1) * MXU_COLS)
            o_ref[0, rows, cols] = o_ref[0, rows, cols] + jnp.dot(
                mix_ref[rows, 0:n_local], wout_bf[0:n_local, cols], preferred_element_type=_F32)
        if final:
            for c in range(r0 // CHUNK, (r0 + half_rows) // CHUNK):
                final_chunk(c)


def _bias_table():
    i = np.arange(BLOCK)[:, None]
    jj = np.arange(2 * BLOCK)[None, :]
    dist = BLOCK + i - jj
    in_band = (dist >= 0) & (dist < WINDOW)
    slopes = np.asarray([2.0 ** (-8.0 * (h + 1) / N_Q_HEADS) for h in range(N_Q_HEADS)], dtype=np.float32)
    tab = np.empty((2, N_KV_HEADS, Q_PER_KV * BLOCK, 2 * BLOCK), np.float32)
    for start in range(2):
        valid = in_band & ~((jj < BLOCK) & (start == 1))
        for h in range(N_Q_HEADS):
            bias = (-slopes[h] * LOG2E) * dist.astype(np.float64)
            kvh, g = divmod(h, Q_PER_KV)
            tab[start, kvh, g * BLOCK:(g + 1) * BLOCK] = np.where(valid, bias, -1e30).astype(np.float32)
    return tab


def _inv_count_table():
    t = np.arange(HALO_U, dtype=np.float32)[:, None]
    gw = POOL_WIDTH // POOL_GROUPS
    wnd = np.repeat(np.asarray(POOL_WINDOWS, np.float32), gw)[None, :]
    return (1.0 / np.minimum(t + 1.0, wnd)).astype(np.float32)


def _layer_call(batch, seq, layer, final):
    n_tiles = seq // TS
    const = lambda shape: pl.BlockSpec(shape, lambda b, j: (0,) * len(shape), pipeline_mode=pl.Buffered(1))
    per_layer = lambda shape: pl.BlockSpec((1,) + shape, lambda b, j: (layer,) + (0,) * len(shape),
                                           pipeline_mode=pl.Buffered(1))
    in_specs = [
        pl.BlockSpec((1, TS, D_MODEL), lambda b, j: (b, j, 0)),
        const((DEPTH, D_MODEL)),
        per_layer((D_MODEL, D_IN)),
        per_layer((POOL_WIDTH, POOL_WIDTH)),
        const((DEPTH, POOL_WIDTH)),
        const((HALO_U, POOL_WIDTH)),
        per_layer((CONV_KERNEL, CONV_WIDTH)),
        const((DEPTH, CONV_WIDTH)),
        const((DEPTH, CONV_WIDTH)),
        const((DEPTH, CONV_WIDTH)),
        per_layer((CONV_WIDTH, CONV_WIDTH)),
        pl.BlockSpec(memory_space=pltpu.SMEM),
        const((2, N_KV_HEADS, Q_PER_KV * BLOCK, 2 * BLOCK)),
        per_layer((D_MIX, D_MODEL)),
        const((1, D_MODEL)),
    ]
    scratch = [
        pltpu.VMEM((TS, D_MODEL), _BF16),
        pltpu.VMEM((TS + HALO_U, POOL_WIDTH), _F32),
        pltpu.VMEM((CONV_WIDTH // LANES, TS + HALO_H, LANES), _F32),
        pltpu.VMEM((TS, POOL_WIDTH), _F32),
        pltpu.VMEM((TS, CONV_WIDTH), _F32),
        pltpu.VMEM((TS, ATTN_WIDTH), _F32),
        pltpu.VMEM((TS, POOL_WIDTH), _BF16),
        pltpu.VMEM((CONV_WIDTH // LANES, TS, LANES), _F32),
        pltpu.VMEM((N_KV_HEADS, TS + BLOCK, HEAD_DIM), _BF16),
        pltpu.VMEM((N_KV_HEADS, TS + BLOCK, 2 * LANES), _BF16),
        pltpu.VMEM((N_KV_HEADS, TS + BLOCK, 2 * LANES), _BF16),
        pltpu.VMEM((N_Q_HEADS, TS, HEAD_DIM), _BF16),
        pltpu.VMEM((TS, D_MIX), _BF16),
        pltpu.VMEM((2, N_KV_HEADS, Q_PER_KV * BLOCK, 2 * BLOCK), _F32),
        pltpu.VMEM((D_MODEL, D_IN), _BF16),
        pltpu.VMEM((D_MIX, D_MODEL), _BF16),
    ]
    return pl.pallas_call(
        functools.partial(_layer_body, layer=layer, final=final),
        out_shape=jax.ShapeDtypeStruct((batch, seq, D_MODEL), _F32),
        grid=(batch, n_tiles),
        in_specs=in_specs,
        out_specs=pl.BlockSpec((1, TS, D_MODEL), lambda b, j: (b, j, 0)),
        scratch_shapes=scratch,
        compiler_params=pltpu.CompilerParams(
            dimension_semantics=("arbitrary", "arbitrary"),
            vmem_limit_bytes=VMEM_LIMIT_BYTES,
        ),
        name="hybrid_layer_final" if final else "hybrid_layer",
    )


def kernel(x, ln_g, w_in, pool_w, pool_scale, conv_dw, conv_b, conv_ln_g, conv_ln_b, conv_pw, attn_sinks, w_out, final_g):
    batch, seq, d_model = x.shape
    assert d_model == D_MODEL and seq % TS == 0 and TS % BLOCK == 0
    bias = jnp.asarray(_bias_table())
    inv_cnt = jnp.asarray(_inv_count_table())
    poolw_bd = jnp.stack([jax.scipy.linalg.block_diag(*[pool_w[l, g] for g in range(POOL_GROUPS)])
                          for l in range(DEPTH)])
    params = (ln_g, w_in, poolw_bd, pool_scale, inv_cnt, conv_dw, conv_b, conv_ln_g, conv_ln_b, conv_pw,
              attn_sinks, bias, w_out, final_g.reshape(1, -1))
    for l in range(DEPTH):
        x = _layer_call(batch, seq, l, final=(l == DEPTH - 1))(x, *params)
    return x
```

```python
import functools

import numpy as np
import jax
import jax.numpy as jnp
from jax import lax
from jax.experimental import pallas as pl
from jax.experimental.pallas import tpu as pltpu

D_MODEL = 1024
DEPTH = 2
POOL_WIDTH = 256
POOL_GROUPS = 4
POOL_WINDOWS = (2, 4, 8, 16)
CONV_WIDTH = 256
CONV_KERNEL = 31
HEAD_DIM = 64
N_Q_HEADS = 8
N_KV_HEADS = 2
Q_PER_KV = N_Q_HEADS // N_KV_HEADS
ATTN_WIDTH = N_Q_HEADS * HEAD_DIM
KV_WIDTH = N_KV_HEADS * HEAD_DIM
WINDOW = 128
BLOCK = 128
D_MIX = POOL_WIDTH + CONV_WIDTH + ATTN_WIDTH
D_IN = 2 * POOL_WIDTH + 3 * CONV_WIDTH + 2 * ATTN_WIDTH + 2 * KV_WIDTH
EPS = 1e-6

C_UPOOL = 0
C_GPOOL = C_UPOOL + POOL_WIDTH
C_CA = C_GPOOL + POOL_WIDTH
C_CB = C_CA + CONV_WIDTH
C_GCONV = C_CB + CONV_WIDTH
C_Q = C_GCONV + CONV_WIDTH
C_K = C_Q + ATTN_WIDTH
C_V = C_K + KV_WIDTH
C_GATTN = C_V + KV_WIDTH

SUBLANES = 8
LANES = 128
BF16_ROWS = 16
MXU_COLS = 256
TS = 1024
NB = TS // BLOCK
OUT_BLOCKS = 8
CHUNK = 64
GROUPS = CHUNK // SUBLANES
HALO_U = 16
HALO_H = 32
VMEM_LIMIT_BYTES = 63 * 1024 * 1024
LOG2E = 1.4426950408889634

_F32 = jnp.float32
_BF16 = jnp.bfloat16


def _silu(v):
    return v * jax.nn.sigmoid(v)


def _unrolled(n, body):
    for c in range(n):
        body(c)


def _shift_down(groups, d, row):
    rolled = [pltpu.roll(g, d, axis=0) for g in groups]
    take_prev = row < d
    out = [rolled[0]]
    for i in range(1, len(groups)):
        out.append(jnp.where(take_prev, rolled[i - 1], rolled[i]))
    return out


def _layer_body(x_ref, lng_ref, win_ref, poolw_ref, pscale_ref, invcnt_ref, dw_ref, cb_ref, clg_ref, clb_ref,
                pw_ref, sinks_ref, bias_ref, wout_ref, fg_ref, o_ref,
                hn_ref, ubuf, hbuf, gp_buf, gc_buf, ga_buf, zp_buf, zc_buf, kbuf, ve_buf, vo_buf, qbuf, mix_ref,
                bias_buf, win_bf, wout_bf, *, layer, final):
    j = pl.program_id(1)
    first_tile = j == 0

    @pl.when(jnp.logical_and(pl.program_id(0) == 0, first_tile))
    def _():
        for c in range(D_IN // MXU_COLS):
            cols = slice(c * MXU_COLS, (c + 1) * MXU_COLS)
            win_bf[:, cols] = win_ref[0, :, cols].astype(_BF16)
        for c in range(D_MODEL // MXU_COLS):
            cols = slice(c * MXU_COLS, (c + 1) * MXU_COLS)
            wout_bf[:, cols] = wout_ref[0, :, cols].astype(_BF16)

    row = lax.broadcasted_iota(jnp.int32, (SUBLANES, LANES), 0)
    lane = lax.broadcasted_iota(jnp.int32, (SUBLANES, LANES), 1)
    low_half = lane < HEAD_DIM

    @pl.when(first_tile)
    def _():
        sink_slot = lax.broadcasted_iota(jnp.int32, (BLOCK, 2 * BLOCK), 1) == 0
        for start in range(2):
            for h in range(N_Q_HEADS):
                kvh, g = divmod(h, Q_PER_KV)
                rows = pl.ds(g * BLOCK, BLOCK)
                bias_buf[start, kvh, rows, :] = jnp.where(
                    sink_slot, sinks_ref[layer, h] * LOG2E, bias_ref[start, kvh, rows, :])
        ubuf[0:HALO_U, :] = jnp.zeros((HALO_U, POOL_WIDTH), _F32)
        hbuf[0:HALO_H, :] = jnp.zeros((HALO_H, CONV_WIDTH), _F32)
        kbuf[:, 0:BLOCK, :] = jnp.zeros((N_KV_HEADS, BLOCK, HEAD_DIM), _BF16)
        zeros = jnp.zeros((TS + BLOCK, LANES), _BF16)
        lane_t = lax.broadcasted_iota(jnp.int32, (TS + BLOCK, LANES), 1)
        ones_lo = jnp.where(lane_t < HEAD_DIM, 1.0, 0.0).astype(_BF16)
        ones_hi = jnp.where(lane_t < HEAD_DIM, 0.0, 1.0).astype(_BF16)
        for kvh in range(N_KV_HEADS):
            ve_buf[kvh, :, 0:LANES] = zeros
            vo_buf[kvh, :, 0:LANES] = zeros
            ve_buf[kvh, :, LANES:2 * LANES] = ones_lo
            vo_buf[kvh, :, LANES:2 * LANES] = ones_hi

    @pl.when(jnp.logical_not(first_tile))
    def _():
        ubuf[0:HALO_U, :] = ubuf[TS:TS + HALO_U, :]
        hbuf[0:HALO_H, :] = hbuf[TS:TS + HALO_H, :]
        kbuf[:, 0:BLOCK, :] = kbuf[:, TS:TS + BLOCK, :]
        ve_buf[:, 0:BLOCK, 0:LANES] = ve_buf[:, TS:TS + BLOCK, 0:LANES]
        vo_buf[:, 0:BLOCK, 0:LANES] = vo_buf[:, TS:TS + BLOCK, 0:LANES]

    def norm_chunk(c):
        r = c * CHUNK
        xc = x_ref[0, pl.ds(r, CHUNK), :]
        ms = jnp.mean(xc * xc, axis=-1, keepdims=True)
        y = xc * lax.rsqrt(ms + EPS) * lng_ref[layer:layer + 1, :]
        hn_ref[pl.ds(r, CHUNK), :] = y.astype(_BF16)

    _unrolled(TS // CHUNK, norm_chunk)

    def proj(c0, width, r0=0, rows=TS):
        w = win_bf[:, c0:c0 + width]
        return jnp.dot(hn_ref[pl.ds(r0, rows), :], w, preferred_element_type=_F32)

    for r0 in range(0, TS, TS // 2):
        rows = pl.ds(HALO_H + r0, TS // 2)
        hbuf[rows, :] = proj(C_CA, CONV_WIDTH, r0, TS // 2)
        hbuf[rows, :] = hbuf[rows, :] * jax.nn.sigmoid(proj(C_CB, CONV_WIDTH, r0, TS // 2))

    kv = proj(C_K, 2 * KV_WIDTH)
    keys = kv[:, 0:KV_WIDTH]
    vals = kv[:, KV_WIDTH:2 * KV_WIDTH]
    vals_swapped = pltpu.roll(vals, HEAD_DIM, axis=1)
    lo = lax.broadcasted_iota(jnp.int32, (TS, LANES), 1) < HEAD_DIM
    for kvh in range(N_KV_HEADS):
        kbuf[kvh, BLOCK:BLOCK + TS, :] = keys[:, kvh * HEAD_DIM:(kvh + 1) * HEAD_DIM].astype(_BF16)
        own_lo, own_hi = (vals, vals_swapped) if kvh == 0 else (vals_swapped, vals)
        ve_buf[kvh, BLOCK:BLOCK + TS, 0:LANES] = jnp.where(lo, own_lo, 0.0).astype(_BF16)
        vo_buf[kvh, BLOCK:BLOCK + TS, 0:LANES] = jnp.where(lo, 0.0, own_hi).astype(_BF16)

    heads_per_group = MXU_COLS // HEAD_DIM
    for half in range(ATTN_WIDTH // MXU_COLS):
        qh = proj(C_Q + half * MXU_COLS, MXU_COLS) * (HEAD_DIM ** -0.5 * LOG2E)
        for i in range(heads_per_group):
            qbuf[half * heads_per_group + i, :, :] = qh[:, i * HEAD_DIM:(i + 1) * HEAD_DIM].astype(_BF16)
    for half in range(ATTN_WIDTH // MXU_COLS):
        ga_buf[:, half * MXU_COLS:(half + 1) * MXU_COLS] = _silu(proj(C_GATTN + half * MXU_COLS, MXU_COLS))

    ubuf[HALO_U:HALO_U + TS, :] = proj(C_UPOOL, POOL_WIDTH)
    gp_buf[...] = _silu(proj(C_GPOOL, POOL_WIDTH))
    gc_buf[...] = _silu(proj(C_GCONV, CONV_WIDTH))

    inv_w = [jnp.where(low_half, 1.0 / POOL_WINDOWS[0], 1.0 / POOL_WINDOWS[1]).astype(_F32),
             jnp.where(low_half, 1.0 / POOL_WINDOWS[2], 1.0 / POOL_WINDOWS[3]).astype(_F32)]

    def pool_chunk(c):
        r = c * CHUNK
        n_in = GROUPS + HALO_U // SUBLANES
        for col in range(2):
            lanes = slice(col * LANES, (col + 1) * LANES)
            u = [ubuf[pl.ds(r + SUBLANES * i, SUBLANES), lanes] for i in range(n_in)]
            s2 = [a + b for a, b in zip(u, _shift_down(u, 1, row))]
            s4 = [a + b for a, b in zip(s2, _shift_down(s2, 2, row))]
            if col == 0:
                wide, narrow = s4, s2
            else:
                s8 = [a + b for a, b in zip(s4, _shift_down(s4, 4, row))]
                s16 = [s8[0]] + [s8[i] + s8[i - 1] for i in range(1, n_in)]
                wide, narrow = s16, s8
            pieces = []
            for g in range(GROUPS):
                i = g + HALO_U // SUBLANES
                inv = inv_w[col]
                if c == 0 and g < HALO_U // SUBLANES:
                    tab = invcnt_ref[SUBLANES * g:SUBLANES * (g + 1), lanes]
                    inv = jnp.where(first_tile, tab, inv)
                pooled = jnp.where(low_half, narrow[i], wide[i]) * inv
                pieces.append(pooled - u[i])
            zp_buf[pl.ds(r, CHUNK), lanes] = jnp.concatenate(pieces, axis=0).astype(_BF16)

    taps_by_b = [[] for _ in range(SUBLANES)]
    for k in range(CONV_KERNEL):
        a, b = divmod(k + HALO_H - (CONV_KERNEL - 1), SUBLANES)
        taps_by_b[b].append((a, k))
    n_h = GROUPS + HALO_H // SUBLANES

    def conv_chunk(c):
        r = c * CHUNK
        outs = []
        for col in range(2):
            lanes = slice(col * LANES, (col + 1) * LANES)
            h = [hbuf[pl.ds(r + SUBLANES * m, SUBLANES), lanes] for m in range(n_h)]
            acc = None
            for b in range(SUBLANES):
                n_p = GROUPS if b == 0 else GROUPS + 1
                part = []
                for m in range(n_p):
                    p = None
                    for a, k in taps_by_b[b]:
                        term = h[m + a] * dw_ref[0, k:k + 1, lanes]
                        p = term if p is None else p + term
                    part.append(p)
                if b == 0:
                    acc = part
                else:
                    rolled = [pltpu.roll(p, SUBLANES - b, axis=0) for p in part]
                    keep = row < (SUBLANES - b)
                    acc = [acc[g] + jnp.where(keep, rolled[g], rolled[g + 1]) for g in range(GROUPS)]
            outs.append(jnp.concatenate(acc, axis=0))
        conv = jnp.concatenate(outs, axis=1) + cb_ref[layer:layer + 1, :]
        mu = jnp.mean(conv, axis=-1, keepdims=True)
        cen = conv - mu
        var = jnp.mean(cen * cen, axis=-1, keepdims=True)
        z = cen * lax.rsqrt(var + EPS) * clg_ref[layer:layer + 1, :] + clb_ref[layer:layer + 1, :]
        zc_buf[pl.ds(r, CHUNK), :] = _silu(z).astype(_BF16)

    _unrolled(TS // CHUNK, conv_chunk)
    _unrolled(TS // CHUNK, pool_chunk)

    n_local = POOL_WIDTH + CONV_WIDTH

    top_rows = lax.broadcasted_iota(jnp.int32, (BF16_ROWS, 2 * LANES), 0)
    top_lanes = lax.broadcasted_iota(jnp.int32, (BF16_ROWS, 2 * LANES), 1)
    sink_value_slot = jnp.logical_and(top_rows == 0, top_lanes < LANES)
    sink_key_slot = lax.broadcasted_iota(jnp.int32, (BF16_ROWS, HEAD_DIM), 0) == 0

    def window_keys(kvh, r):
        top = kbuf[kvh, pl.ds(r, BF16_ROWS), :].astype(_F32)
        top = jnp.where(sink_key_slot, 0.0, top).astype(_BF16)
        return jnp.concatenate([top, kbuf[kvh, pl.ds(r + BF16_ROWS, 2 * BLOCK - BF16_ROWS), :]], axis=0)

    def scores(n):
        r = n * BLOCK
        seq_start = first_tile.astype(jnp.int32) if n == 0 else 0
        out = []
        for kvh in range(N_KV_HEADS):
            k2 = window_keys(kvh, r)
            q4 = jnp.concatenate([qbuf[kvh * Q_PER_KV + g, pl.ds(r, BLOCK), :] for g in range(Q_PER_KV)], axis=0)
            s = lax.dot_general(q4, k2, (((1,), (1,)), ((), ())), preferred_element_type=_F32)
            out.append(s + bias_buf[seq_start, kvh])
        return out

    def window_values(buf, kvh, r):
        top = buf[kvh, pl.ds(r, BF16_ROWS), :].astype(_F32)
        top = jnp.where(sink_value_slot, 0.0, top).astype(_BF16)
        return jnp.concatenate([top, buf[kvh, pl.ds(r + BF16_ROWS, 2 * BLOCK - BF16_ROWS), :]], axis=0)

    def softmax_pv(n, s_list):
        r = n * BLOCK
        for kvh in range(N_KV_HEADS):
            probs = []
            for g in range(Q_PER_KV):
                sg = s_list[kvh][g * BLOCK:(g + 1) * BLOCK]
                m = jnp.max(sg, axis=-1, keepdims=True)
                probs.append(jnp.exp2(sg - m).astype(_BF16))
            p_even = jnp.concatenate([probs[0], probs[2]], axis=0)
            p_odd = jnp.concatenate([probs[1], probs[3]], axis=0)
            o = (jnp.dot(p_even, window_values(ve_buf, kvh, r), preferred_element_type=_F32)
                 + jnp.dot(p_odd, window_values(vo_buf, kvh, r), preferred_element_type=_F32))
            res = o[:, 0:LANES] / o[:, LANES:2 * LANES]
            for pair in range(2):
                c0 = (kvh * 2 + pair) * LANES
                gate = ga_buf[pl.ds(r, BLOCK), c0:c0 + LANES]
                mix_ref[pl.ds(r, BLOCK), n_local + c0:n_local + c0 + LANES] = (
                    res[pair * BLOCK:(pair + 1) * BLOCK] * gate).astype(_BF16)

    def out_attn(r0, rows):
        for q in range(D_MODEL // MXU_COLS):
            cols = slice(q * MXU_COLS, (q + 1) * MXU_COLS)
            o_ref[0, pl.ds(r0, rows), cols] = x_ref[0, pl.ds(r0, rows), cols] + jnp.dot(
                mix_ref[pl.ds(r0, rows), n_local:D_MIX], wout_bf[n_local:D_MIX, cols],
                preferred_element_type=_F32)

    s_cur = scores(0)
    for n in range(NB):
        s_next = scores(n + 1) if n + 1 < NB else None
        if n >= OUT_BLOCKS and n % OUT_BLOCKS == 0:
            out_attn((n - OUT_BLOCKS) * BLOCK, OUT_BLOCKS * BLOCK)
        softmax_pv(n, s_cur)
        s_cur = s_next
    out_attn((NB - OUT_BLOCKS) * BLOCK, OUT_BLOCKS * BLOCK)

    y_pool = jnp.dot(zp_buf[...], poolw_ref[0].astype(_BF16), preferred_element_type=_F32)
    y_pool = y_pool * pscale_ref[layer:layer + 1, :]
    mix_ref[:, 0:POOL_WIDTH] = (y_pool * gp_buf[...]).astype(_BF16)
    y_conv = jnp.dot(zc_buf[...], pw_ref[0].astype(_BF16), preferred_element_type=_F32)
    mix_ref[:, POOL_WIDTH:POOL_WIDTH + CONV_WIDTH] = (y_conv * gc_buf[...]).astype(_BF16)

    def final_chunk(c):
        r = c * CHUNK
        xc = o_ref[0, pl.ds(r, CHUNK), :]
        ms = jnp.mean(xc * xc, axis=-1, keepdims=True)
        o_ref[0, pl.ds(r, CHUNK), :] = xc * lax.rsqrt(ms + EPS) * fg_ref[...]

    half_rows = TS
    for r0 in range(0, TS, half_rows):
        rows = pl.ds(r0, half_rows)
        for q in range(D_MODEL // MXU_COLS):
            cols = slice(q * MXU_COLS, (q + 1) * MXU_COLS)
            o_ref[0, rows, cols] = o_ref[0, rows, cols] + jnp.dot(
                mix_ref[rows, 0:n_local], wout_bf[0:n_local, cols], preferred_element_type=_F32)
        if final:
            for c in range(r0 // CHUNK, (r0 + half_rows) // CHUNK):
                final_chunk(c)


def _bias_table():
    i = np.arange(BLOCK)[:, None]
    jj = np.arange(2 * BLOCK)[None, :]
    dist = BLOCK + i - jj
    in_band = (dist >= 0) & (dist < WINDOW)
    slopes = np.asarray([2.0 ** (-8.0 * (h + 1) / N_Q_HEADS) for h in range(N_Q_HEADS)], dtype=np.float32)
    tab = np.empty((2, N_KV_HEADS, Q_PER_KV * BLOCK, 2 * BLOCK), np.float32)
    for start in range(2):
        valid = in_band & ~((jj < BLOCK) & (start == 1))
        for h in range(N_Q_HEADS):
            bias = (-slopes[h] * LOG2E) * dist.astype(np.float64)
            kvh, g = divmod(h, Q_PER_KV)
            tab[start, kvh, g * BLOCK:(g + 1) * BLOCK] = np.where(valid, bias, -1e30).astype(np.float32)
    return tab


def _inv_count_table():
    t = np.arange(HALO_U, dtype=np.float32)[:, None]
    gw = POOL_WIDTH // POOL_GROUPS
    wnd = np.repeat(np.asarray(POOL_WINDOWS, np.float32), gw)[None, :]
    return (1.0 / np.minimum(t + 1.0, wnd)).astype(np.float32)


def _layer_call(batch, seq, layer, final):
    n_tiles = seq // TS
    const = lambda shape: pl.BlockSpec(shape, lambda b, j: (0,) * len(shape), pipeline_mode=pl.Buffered(1))
    per_layer = lambda shape: pl.BlockSpec((1,) + shape, lambda b, j: (layer,) + (0,) * len(shape),
                                           pipeline_mode=pl.Buffered(1))
    in_specs = [
        pl.BlockSpec((1, TS, D_MODEL), lambda b, j: (b, j, 0)),
        const((DEPTH, D_MODEL)),
        per_layer((D_MODEL, D_IN)),
        per_layer((POOL_WIDTH, POOL_WIDTH)),
        const((DEPTH, POOL_WIDTH)),
        const((HALO_U, POOL_WIDTH)),
        per_layer((CONV_KERNEL, CONV_WIDTH)),
        const((DEPTH, CONV_WIDTH)),
        const((DEPTH, CONV_WIDTH)),
        const((DEPTH, CONV_WIDTH)),
        per_layer((CONV_WIDTH, CONV_WIDTH)),
        pl.BlockSpec(memory_space=pltpu.SMEM),
        const((2, N_KV_HEADS, Q_PER_KV * BLOCK, 2 * BLOCK)),
        per_layer((D_MIX, D_MODEL)),
        const((1, D_MODEL)),
    ]
    scratch = [
        pltpu.VMEM((TS, D_MODEL), _BF16),
        pltpu.VMEM((TS + HALO_U, POOL_WIDTH), _F32),
        pltpu.VMEM((TS + HALO_H, CONV_WIDTH), _F32),
        pltpu.VMEM((TS, POOL_WIDTH), _F32),
        pltpu.VMEM((TS, CONV_WIDTH), _F32),
        pltpu.VMEM((TS, ATTN_WIDTH), _F32),
        pltpu.VMEM((TS, POOL_WIDTH), _BF16),
        pltpu.VMEM((TS, CONV_WIDTH), _BF16),
        pltpu.VMEM((N_KV_HEADS, TS + BLOCK, HEAD_DIM), _BF16),
        pltpu.VMEM((N_KV_HEADS, TS + BLOCK, 2 * LANES), _BF16),
        pltpu.VMEM((N_KV_HEADS, TS + BLOCK, 2 * LANES), _BF16),
        pltpu.VMEM((N_Q_HEADS, TS, HEAD_DIM), _BF16),
        pltpu.VMEM((TS, D_MIX), _BF16),
        pltpu.VMEM((2, N_KV_HEADS, Q_PER_KV * BLOCK, 2 * BLOCK), _F32),
        pltpu.VMEM((D_MODEL, D_IN), _BF16),
        pltpu.VMEM((D_MIX, D_MODEL), _BF16),
    ]
    return pl.pallas_call(
        functools.partial(_layer_body, layer=layer, final=final),
        out_shape=jax.ShapeDtypeStruct((batch, seq, D_MODEL), _F32),
        grid=(batch, n_tiles),
        in_specs=in_specs,
        out_specs=pl.BlockSpec((1, TS, D_MODEL), lambda b, j: (b, j, 0)),
        scratch_shapes=scratch,
        compiler_params=pltpu.CompilerParams(
            dimension_semantics=("arbitrary", "arbitrary"),
            vmem_limit_bytes=VMEM_LIMIT_BYTES,
        ),
        name="hybrid_layer_final" if final else "hybrid_layer",
    )


def kernel(x, ln_g, w_in, pool_w, pool_scale, conv_dw, conv_b, conv_ln_g, conv_ln_b, conv_pw, attn_sinks, w_out, final_g):
    batch, seq, d_model = x.shape
    assert d_model == D_MODEL and seq % TS == 0 and TS % BLOCK == 0
    bias = jnp.asarray(_bias_table())
    inv_cnt = jnp.asarray(_inv_count_table())
    poolw_bd = jnp.stack([jax.scipy.linalg.block_diag(*[pool_w[l, g] for g in range(POOL_GROUPS)])
                          for l in range(DEPTH)])
    params = (ln_g, w_in, poolw_bd, pool_scale, inv_cnt, conv_dw, conv_b, conv_ln_g, conv_ln_b, conv_pw,
              attn_sinks, bias, w_out, final_g.reshape(1, -1))
    for l in range(DEPTH):
        x = _layer_call(batch, seq, l, final=(l == DEPTH - 1))(x, *params)
    return x
```

```python
import functools

import numpy as np
import jax
import jax.numpy as jnp
from jax import lax
from jax.experimental import pallas as pl
from jax.experimental.pallas import tpu as pltpu

D_MODEL = 1024
DEPTH = 2
POOL_WIDTH = 256
POOL_GROUPS = 4
POOL_WINDOWS = (2, 4, 8, 16)
CONV_WIDTH = 256
CONV_KERNEL = 31
HEAD_DIM = 64
N_Q_HEADS = 8
N_KV_HEADS = 2
Q_PER_KV = N_Q_HEADS // N_KV_HEADS
ATTN_WIDTH = N_Q_HEADS * HEAD_DIM
KV_WIDTH = N_KV_HEADS * HEAD_DIM
WINDOW = 128
BLOCK = 128
D_MIX = POOL_WIDTH + CONV_WIDTH + ATTN_WIDTH
D_IN = 2 * POOL_WIDTH + 3 * CONV_WIDTH + 2 * ATTN_WIDTH + 2 * KV_WIDTH
EPS = 1e-6

C_UPOOL = 0
C_GPOOL = C_UPOOL + POOL_WIDTH
C_CA = C_GPOOL + POOL_WIDTH
C_CB = C_CA + CONV_WIDTH
C_GCONV = C_CB + CONV_WIDTH
C_Q = C_GCONV + CONV_WIDTH
C_K = C_Q + ATTN_WIDTH
C_V = C_K + KV_WIDTH
C_GATTN = C_V + KV_WIDTH

SUBLANES = 8
LANES = 128
BF16_ROWS = 16
MXU_COLS = 256
TS = 1024
NB = TS // BLOCK
CHUNK = 64
GROUPS = CHUNK // SUBLANES
HALO_U = 16
HALO_H = 32
VMEM_LIMIT_BYTES = 63 * 1024 * 1024
LOG2E = 1.4426950408889634

_F32 = jnp.float32
_BF16 = jnp.bfloat16


def _silu(v):
    return v * jax.nn.sigmoid(v)


def _unrolled(n, body):
    for c in range(n):
        body(c)


def _shift_down(groups, d, row):
    rolled = [pltpu.roll(g, d, axis=0) for g in groups]
    take_prev = row < d
    out = [rolled[0]]
    for i in range(1, len(groups)):
        out.append(jnp.where(take_prev, rolled[i - 1], rolled[i]))
    return out


def _layer_body(x_ref, lng_ref, win_ref, poolw_ref, pscale_ref, invcnt_ref, dw_ref, cb_ref, clg_ref, clb_ref,
                pw_ref, sinks_ref, bias_ref, wout_ref, fg_ref, o_ref,
                hn_ref, ubuf, hbuf, gp_buf, gc_buf, ga_buf, zp_buf, zc_buf, kbuf, ve_buf, vo_buf, qbuf, mix_ref,
                bias_buf, win_bf, wout_bf, *, layer, final):
    j = pl.program_id(1)
    first_tile = j == 0

    @pl.when(jnp.logical_and(pl.program_id(0) == 0, first_tile))
    def _():
        for c in range(D_IN // MXU_COLS):
            cols = slice(c * MXU_COLS, (c + 1) * MXU_COLS)
            win_bf[:, cols] = win_ref[0, :, cols].astype(_BF16)
        for c in range(D_MODEL // MXU_COLS):
            cols = slice(c * MXU_COLS, (c + 1) * MXU_COLS)
            wout_bf[:, cols] = wout_ref[0, :, cols].astype(_BF16)

    row = lax.broadcasted_iota(jnp.int32, (SUBLANES, LANES), 0)
    lane = lax.broadcasted_iota(jnp.int32, (SUBLANES, LANES), 1)
    low_half = lane < HEAD_DIM

    @pl.when(first_tile)
    def _():
        sink_slot = lax.broadcasted_iota(jnp.int32, (BLOCK, 2 * BLOCK), 1) == 0
        for start in range(2):
            for h in range(N_Q_HEADS):
                kvh, g = divmod(h, Q_PER_KV)
                rows = pl.ds(g * BLOCK, BLOCK)
                bias_buf[start, kvh, rows, :] = jnp.where(
                    sink_slot, sinks_ref[layer, h] * LOG2E, bias_ref[start, kvh, rows, :])
        ubuf[0:HALO_U, :] = jnp.zeros((HALO_U, POOL_WIDTH), _F32)
        hbuf[0:HALO_H, :] = jnp.zeros((HALO_H, CONV_WIDTH), _F32)
        kbuf[:, 0:BLOCK, :] = jnp.zeros((N_KV_HEADS, BLOCK, HEAD_DIM), _BF16)
        zeros = jnp.zeros((TS + BLOCK, LANES), _BF16)
        lane_t = lax.broadcasted_iota(jnp.int32, (TS + BLOCK, LANES), 1)
        ones_lo = jnp.where(lane_t < HEAD_DIM, 1.0, 0.0).astype(_BF16)
        ones_hi = jnp.where(lane_t < HEAD_DIM, 0.0, 1.0).astype(_BF16)
        for kvh in range(N_KV_HEADS):
            ve_buf[kvh, :, 0:LANES] = zeros
            vo_buf[kvh, :, 0:LANES] = zeros
            ve_buf[kvh, :, LANES:2 * LANES] = ones_lo
            vo_buf[kvh, :, LANES:2 * LANES] = ones_hi

    @pl.when(jnp.logical_not(first_tile))
    def _():
        ubuf[0:HALO_U, :] = ubuf[TS:TS + HALO_U, :]
        hbuf[0:HALO_H, :] = hbuf[TS:TS + HALO_H, :]
        kbuf[:, 0:BLOCK, :] = kbuf[:, TS:TS + BLOCK, :]
        ve_buf[:, 0:BLOCK, 0:LANES] = ve_buf[:, TS:TS + BLOCK, 0:LANES]
        vo_buf[:, 0:BLOCK, 0:LANES] = vo_buf[:, TS:TS + BLOCK, 0:LANES]

    def norm_chunk(c):
        r = c * CHUNK
        xc = x_ref[0, pl.ds(r, CHUNK), :]
        ms = jnp.mean(xc * xc, axis=-1, keepdims=True)
        y = xc * lax.rsqrt(ms + EPS) * lng_ref[layer:layer + 1, :]
        hn_ref[pl.ds(r, CHUNK), :] = y.astype(_BF16)

    _unrolled(TS // CHUNK, norm_chunk)

    def proj(c0, width, r0=0, rows=TS):
        w = win_bf[:, c0:c0 + width]
        return jnp.dot(hn_ref[pl.ds(r0, rows), :], w, preferred_element_type=_F32)

    for r0 in range(0, TS, TS // 2):
        rows = pl.ds(HALO_H + r0, TS // 2)
        hbuf[rows, :] = proj(C_CA, CONV_WIDTH, r0, TS // 2)
        hbuf[rows, :] = hbuf[rows, :] * jax.nn.sigmoid(proj(C_CB, CONV_WIDTH, r0, TS // 2))

    kv = proj(C_K, 2 * KV_WIDTH)
    keys = kv[:, 0:KV_WIDTH]
    vals = kv[:, KV_WIDTH:2 * KV_WIDTH]
    vals_swapped = pltpu.roll(vals, HEAD_DIM, axis=1)
    lo = lax.broadcasted_iota(jnp.int32, (TS, LANES), 1) < HEAD_DIM
    for kvh in range(N_KV_HEADS):
        kbuf[kvh, BLOCK:BLOCK + TS, :] = keys[:, kvh * HEAD_DIM:(kvh + 1) * HEAD_DIM].astype(_BF16)
        own_lo, own_hi = (vals, vals_swapped) if kvh == 0 else (vals_swapped, vals)
        ve_buf[kvh, BLOCK:BLOCK + TS, 0:LANES] = jnp.where(lo, own_lo, 0.0).astype(_BF16)
        vo_buf[kvh, BLOCK:BLOCK + TS, 0:LANES] = jnp.where(lo, 0.0, own_hi).astype(_BF16)

    heads_per_group = MXU_COLS // HEAD_DIM
    for half in range(ATTN_WIDTH // MXU_COLS):
        qh = proj(C_Q + half * MXU_COLS, MXU_COLS) * (HEAD_DIM ** -0.5 * LOG2E)
        for i in range(heads_per_group):
            qbuf[half * heads_per_group + i, :, :] = qh[:, i * HEAD_DIM:(i + 1) * HEAD_DIM].astype(_BF16)
    for half in range(ATTN_WIDTH // MXU_COLS):
        ga_buf[:, half * MXU_COLS:(half + 1) * MXU_COLS] = _silu(proj(C_GATTN + half * MXU_COLS, MXU_COLS))

    ubuf[HALO_U:HALO_U + TS, :] = proj(C_UPOOL, POOL_WIDTH)
    gp_buf[...] = _silu(proj(C_GPOOL, POOL_WIDTH))
    gc_buf[...] = _silu(proj(C_GCONV, CONV_WIDTH))

    inv_w = [jnp.where(low_half, 1.0 / POOL_WINDOWS[0], 1.0 / POOL_WINDOWS[1]).astype(_F32),
             jnp.where(low_half, 1.0 / POOL_WINDOWS[2], 1.0 / POOL_WINDOWS[3]).astype(_F32)]

    def pool_chunk(c):
        r = c * CHUNK
        n_in = GROUPS + HALO_U // SUBLANES
        for col in range(2):
            lanes = slice(col * LANES, (col + 1) * LANES)
            u = [ubuf[pl.ds(r + SUBLANES * i, SUBLANES), lanes] for i in range(n_in)]
            s2 = [a + b for a, b in zip(u, _shift_down(u, 1, row))]
            s4 = [a + b for a, b in zip(s2, _shift_down(s2, 2, row))]
            if col == 0:
                wide, narrow = s4, s2
            else:
                s8 = [a + b for a, b in zip(s4, _shift_down(s4, 4, row))]
                s16 = [s8[0]] + [s8[i] + s8[i - 1] for i in range(1, n_in)]
                wide, narrow = s16, s8
            pieces = []
            for g in range(GROUPS):
                i = g + HALO_U // SUBLANES
                inv = inv_w[col]
                if c == 0 and g < HALO_U // SUBLANES:
                    tab = invcnt_ref[SUBLANES * g:SUBLANES * (g + 1), lanes]
                    inv = jnp.where(first_tile, tab, inv)
                pooled = jnp.where(low_half, narrow[i], wide[i]) * inv
                pieces.append(pooled - u[i])
            zp_buf[pl.ds(r, CHUNK), lanes] = jnp.concatenate(pieces, axis=0).astype(_BF16)

    taps_by_b = [[] for _ in range(SUBLANES)]
    for k in range(CONV_KERNEL):
        a, b = divmod(k + HALO_H - (CONV_KERNEL - 1), SUBLANES)
        taps_by_b[b].append((a, k))
    n_h = GROUPS + HALO_H // SUBLANES

    def conv_chunk(c):
        r = c * CHUNK
        outs = []
        for col in range(2):
            lanes = slice(col * LANES, (col + 1) * LANES)
            h = [hbuf[pl.ds(r + SUBLANES * m, SUBLANES), lanes] for m in range(n_h)]
            acc = None
            for b in range(SUBLANES):
                n_p = GROUPS if b == 0 else GROUPS + 1
                part = []
                for m in range(n_p):
                    p = None
                    for a, k in taps_by_b[b]:
                        term = h[m + a] * dw_ref[0, k:k + 1, lanes]
                        p = term if p is None else p + term
                    part.append(p)
                if b == 0:
                    acc = part
                else:
                    rolled = [pltpu.roll(p, SUBLANES - b, axis=0) for p in part]
                    keep = row < (SUBLANES - b)
                    acc = [acc[g] + jnp.where(keep, rolled[g], rolled[g + 1]) for g in range(GROUPS)]
            outs.append(jnp.concatenate(acc, axis=0))
        conv = jnp.concatenate(outs, axis=1) + cb_ref[layer:layer + 1, :]
        mu = jnp.mean(conv, axis=-1, keepdims=True)
        cen = conv - mu
        var = jnp.mean(cen * cen, axis=-1, keepdims=True)
        z = cen * lax.rsqrt(var + EPS) * clg_ref[layer:layer + 1, :] + clb_ref[layer:layer + 1, :]
        zc_buf[pl.ds(r, CHUNK), :] = _silu(z).astype(_BF16)

    _unrolled(TS // CHUNK, conv_chunk)
    _unrolled(TS // CHUNK, pool_chunk)

    n_local = POOL_WIDTH + CONV_WIDTH

    top_rows = lax.broadcasted_iota(jnp.int32, (BF16_ROWS, 2 * LANES), 0)
    top_lanes = lax.broadcasted_iota(jnp.int32, (BF16_ROWS, 2 * LANES), 1)
    sink_value_slot = jnp.logical_and(top_rows == 0, top_lanes < LANES)
    sink_key_slot = lax.broadcasted_iota(jnp.int32, (BF16_ROWS, HEAD_DIM), 0) == 0

    def window_keys(kvh, r):
        top = kbuf[kvh, pl.ds(r, BF16_ROWS), :].astype(_F32)
        top = jnp.where(sink_key_slot, 0.0, top).astype(_BF16)
        return jnp.concatenate([top, kbuf[kvh, pl.ds(r + BF16_ROWS, 2 * BLOCK - BF16_ROWS), :]], axis=0)

    def scores(n):
        r = n * BLOCK
        seq_start = first_tile.astype(jnp.int32) if n == 0 else 0
        out = []
        for kvh in range(N_KV_HEADS):
            k2 = window_keys(kvh, r)
            q4 = jnp.concatenate([qbuf[kvh * Q_PER_KV + g, pl.ds(r, BLOCK), :] for g in range(Q_PER_KV)], axis=0)
            s = lax.dot_general(q4, k2, (((1,), (1,)), ((), ())), preferred_element_type=_F32)
            out.append(s + bias_buf[seq_start, kvh])
        return out

    def window_values(buf, kvh, r):
        top = buf[kvh, pl.ds(r, BF16_ROWS), :].astype(_F32)
        top = jnp.where(sink_value_slot, 0.0, top).astype(_BF16)
        return jnp.concatenate([top, buf[kvh, pl.ds(r + BF16_ROWS, 2 * BLOCK - BF16_ROWS), :]], axis=0)

    def softmax_pv(n, s_list):
        r = n * BLOCK
        for kvh in range(N_KV_HEADS):
            probs = []
            for g in range(Q_PER_KV):
                sg = s_list[kvh][g * BLOCK:(g + 1) * BLOCK]
                m = jnp.max(sg, axis=-1, keepdims=True)
                probs.append(jnp.exp2(sg - m).astype(_BF16))
            p_even = jnp.concatenate([probs[0], probs[2]], axis=0)
            p_odd = jnp.concatenate([probs[1], probs[3]], axis=0)
            o = (jnp.dot(p_even, window_values(ve_buf, kvh, r), preferred_element_type=_F32)
                 + jnp.dot(p_odd, window_values(vo_buf, kvh, r), preferred_element_type=_F32))
            res = o[:, 0:LANES] / o[:, LANES:2 * LANES]
            for pair in range(2):
                c0 = (kvh * 2 + pair) * LANES
                gate = ga_buf[pl.ds(r, BLOCK), c0:c0 + LANES]
                mix_ref[pl.ds(r, BLOCK), n_local + c0:n_local + c0 + LANES] = (
                    res[pair * BLOCK:(pair + 1) * BLOCK] * gate).astype(_BF16)

    s_cur = scores(0)
    for n in range(NB):
        s_next = scores(n + 1) if n + 1 < NB else None
        softmax_pv(n, s_cur)
        s_cur = s_next

    y_pool = jnp.dot(zp_buf[...], poolw_ref[0].astype(_BF16), preferred_element_type=_F32)
    y_pool = y_pool * pscale_ref[layer:layer + 1, :]
    mix_ref[:, 0:POOL_WIDTH] = (y_pool * gp_buf[...]).astype(_BF16)
    y_conv = jnp.dot(zc_buf[...], pw_ref[0].astype(_BF16), preferred_element_type=_F32)
    mix_ref[:, POOL_WIDTH:POOL_WIDTH + CONV_WIDTH] = (y_conv * gc_buf[...]).astype(_BF16)

    def final_chunk(c):
        r = c * CHUNK
        xc = o_ref[0, pl.ds(r, CHUNK), :]
        ms = jnp.mean(xc * xc, axis=-1, keepdims=True)
        o_ref[0, pl.ds(r, CHUNK), :] = xc * lax.rsqrt(ms + EPS) * fg_ref[...]

    for q in range(D_MODEL // MXU_COLS):
        cols = slice(q * MXU_COLS, (q + 1) * MXU_COLS)
        o_ref[0, :, cols] = x_ref[0, :, cols] + jnp.dot(
            mix_ref[...], wout_bf[:, cols], preferred_element_type=_F32)
    if final:
        _unrolled(TS // CHUNK, final_chunk)


def _bias_table():
    i = np.arange(BLOCK)[:, None]
    jj = np.arange(2 * BLOCK)[None, :]
    dist = BLOCK + i - jj
    in_band = (dist >= 0) & (dist < WINDOW)
    slopes = np.asarray([2.0 ** (-8.0 * (h + 1) / N_Q_HEADS) for h in range(N_Q_HEADS)], dtype=np.float32)
    tab = np.empty((2, N_KV_HEADS, Q_PER_KV * BLOCK, 2 * BLOCK), np.float32)
    for start in range(2):
        valid = in_band & ~((jj < BLOCK) & (start == 1))
        for h in range(N_Q_HEADS):
            bias = (-slopes[h] * LOG2E) * dist.astype(np.float64)
            kvh, g = divmod(h, Q_PER_KV)
            tab[start, kvh, g * BLOCK:(g + 1) * BLOCK] = np.where(valid, bias, -1e30).astype(np.float32)
    return tab


def _inv_count_table():
    t = np.arange(HALO_U, dtype=np.float32)[:, None]
    gw = POOL_WIDTH // POOL_GROUPS
    wnd = np.repeat(np.asarray(POOL_WINDOWS, np.float32), gw)[None, :]
    return (1.0 / np.minimum(t + 1.0, wnd)).astype(np.float32)


def _layer_call(batch, seq, layer, final):
    n_tiles = seq // TS
    const = lambda shape: pl.BlockSpec(shape, lambda b, j: (0,) * len(shape), pipeline_mode=pl.Buffered(1))
    per_layer = lambda shape: pl.BlockSpec((1,) + shape, lambda b, j: (layer,) + (0,) * len(shape),
                                           pipeline_mode=pl.Buffered(1))
    in_specs = [
        pl.BlockSpec((1, TS, D_MODEL), lambda b, j: (b, j, 0)),
        const((DEPTH, D_MODEL)),
        per_layer((D_MODEL, D_IN)),
        per_layer((POOL_WIDTH, POOL_WIDTH)),
        const((DEPTH, POOL_WIDTH)),
        const((HALO_U, POOL_WIDTH)),
        per_layer((CONV_KERNEL, CONV_WIDTH)),
        const((DEPTH, CONV_WIDTH)),
        const((DEPTH, CONV_WIDTH)),
        const((DEPTH, CONV_WIDTH)),
        per_layer((CONV_WIDTH, CONV_WIDTH)),
        pl.BlockSpec(memory_space=pltpu.SMEM),
        const((2, N_KV_HEADS, Q_PER_KV * BLOCK, 2 * BLOCK)),
        per_layer((D_MIX, D_MODEL)),
        const((1, D_MODEL)),
    ]
    scratch = [
        pltpu.VMEM((TS, D_MODEL), _BF16),
        pltpu.VMEM((TS + HALO_U, POOL_WIDTH), _F32),
        pltpu.VMEM((TS + HALO_H, CONV_WIDTH), _F32),
        pltpu.VMEM((TS, POOL_WIDTH), _F32),
        pltpu.VMEM((TS, CONV_WIDTH), _F32),
        pltpu.VMEM((TS, ATTN_WIDTH), _F32),
        pltpu.VMEM((TS, POOL_WIDTH), _BF16),
        pltpu.VMEM((TS, CONV_WIDTH), _BF16),
        pltpu.VMEM((N_KV_HEADS, TS + BLOCK, HEAD_DIM), _BF16),
        pltpu.VMEM((N_KV_HEADS, TS + BLOCK, 2 * LANES), _BF16),
        pltpu.VMEM((N_KV_HEADS, TS + BLOCK, 2 * LANES), _BF16),
        pltpu.VMEM((N_Q_HEADS, TS, HEAD_DIM), _BF16),
        pltpu.VMEM((TS, D_MIX), _BF16),
        pltpu.VMEM((2, N_KV_HEADS, Q_PER_KV * BLOCK, 2 * BLOCK), _F32),
        pltpu.VMEM((D_MODEL, D_IN), _BF16),
        pltpu.VMEM((D_MIX, D_MODEL), _BF16),
    ]
    return pl.pallas_call(
        functools.partial(_layer_body, layer=layer, final=final),
        out_shape=jax.ShapeDtypeStruct((batch, seq, D_MODEL), _F32),
        grid=(batch, n_tiles),
        in_specs=in_specs,
        out_specs=pl.BlockSpec((1, TS, D_MODEL), lambda b, j: (b, j, 0)),
        scratch_shapes=scratch,
        compiler_params=pltpu.CompilerParams(
            dimension_semantics=("arbitrary", "arbitrary"),
            vmem_limit_bytes=VMEM_LIMIT_BYTES,
        ),
        name="hybrid_layer_final" if final else "hybrid_layer",
    )


def kernel(x, ln_g, w_in, pool_w, pool_scale, conv_dw, conv_b, conv_ln_g, conv_ln_b, conv_pw, attn_sinks, w_out, final_g):
    batch, seq, d_model = x.shape
    assert d_model == D_MODEL and seq % TS == 0 and TS % BLOCK == 0
    bias = jnp.asarray(_bias_table())
    inv_cnt = jnp.asarray(_inv_count_table())
    poolw_bd = jnp.stack([jax.scipy.linalg.block_diag(*[pool_w[l, g] for g in range(POOL_GROUPS)])
                          for l in range(DEPTH)])
    params = (ln_g, w_in, poolw_bd, pool_scale, inv_cnt, conv_dw, conv_b, conv_ln_g, conv_ln_b, conv_pw,
              attn_sinks, bias, w_out, final_g.reshape(1, -1))
    for l in range(DEPTH):
        x = _layer_call(batch, seq, l, final=(l == DEPTH - 1))(x, *params)
    return x
```

```python
import functools

import numpy as np
import jax
import jax.numpy as jnp
from jax import lax
from jax.experimental import pallas as pl
from jax.experimental.pallas import tpu as pltpu

D_MODEL = 1024
DEPTH = 2
POOL_WIDTH = 256
POOL_GROUPS = 4
POOL_WINDOWS = (2, 4, 8, 16)
CONV_WIDTH = 256
CONV_KERNEL = 31
HEAD_DIM = 64
N_Q_HEADS = 8
N_KV_HEADS = 2
Q_PER_KV = N_Q_HEADS // N_KV_HEADS
ATTN_WIDTH = N_Q_HEADS * HEAD_DIM
KV_WIDTH = N_KV_HEADS * HEAD_DIM
WINDOW = 128
BLOCK = 128
D_MIX = POOL_WIDTH + CONV_WIDTH + ATTN_WIDTH
D_IN = 2 * POOL_WIDTH + 3 * CONV_WIDTH + 2 * ATTN_WIDTH + 2 * KV_WIDTH
EPS = 1e-6

C_UPOOL = 0
C_GPOOL = C_UPOOL + POOL_WIDTH
C_CA = C_GPOOL + POOL_WIDTH
C_CB = C_CA + CONV_WIDTH
C_GCONV = C_CB + CONV_WIDTH
C_Q = C_GCONV + CONV_WIDTH
C_K = C_Q + ATTN_WIDTH
C_V = C_K + KV_WIDTH
C_GATTN = C_V + KV_WIDTH

SUBLANES = 8
LANES = 128
BF16_ROWS = 16
MXU_COLS = 256
TS = 1024
NB = TS // BLOCK
OUT_BLOCKS = 8
CHUNK = 64
GROUPS = CHUNK // SUBLANES
HALO_U = 16
HALO_H = 32
VMEM_LIMIT_BYTES = 63 * 1024 * 1024
LOG2E = 1.4426950408889634

_F32 = jnp.float32
_BF16 = jnp.bfloat16


def _silu(v):
    return v * jax.nn.sigmoid(v)


def _unrolled(n, body):
    for c in range(n):
        body(c)


def _shift_down(groups, d, row):
    rolled = [pltpu.roll(g, d, axis=0) for g in groups]
    take_prev = row < d
    out = [rolled[0]]
    for i in range(1, len(groups)):
        out.append(jnp.where(take_prev, rolled[i - 1], rolled[i]))
    return out


def _layer_body(x_ref, lng_ref, win_ref, poolw_ref, pscale_ref, invcnt_ref, dw_ref, cb_ref, clg_ref, clb_ref,
                pw_ref, sinks_ref, bias_ref, wout_ref, fg_ref, o_ref,
                hn_ref, ubuf, hbuf, gp_buf, gc_buf, ga_buf, zp_buf, zc_buf, kbuf, v_buf, qbuf, mix_ref,
                bias_buf, win_bf, wout_bf, *, layer, final):
    j = pl.program_id(1)
    first_tile = j == 0

    @pl.when(jnp.logical_and(pl.program_id(0) == 0, first_tile))
    def _():
        for c in range(D_IN // MXU_COLS):
            cols = slice(c * MXU_COLS, (c + 1) * MXU_COLS)
            win_bf[:, cols] = win_ref[0, :, cols].astype(_BF16)
        for c in range(D_MODEL // MXU_COLS):
            cols = slice(c * MXU_COLS, (c + 1) * MXU_COLS)
            wout_bf[:, cols] = wout_ref[0, :, cols].astype(_BF16)

    row = lax.broadcasted_iota(jnp.int32, (SUBLANES, LANES), 0)
    lane = lax.broadcasted_iota(jnp.int32, (SUBLANES, LANES), 1)
    low_half = lane < HEAD_DIM

    @pl.when(first_tile)
    def _():
        sink_slot = lax.broadcasted_iota(jnp.int32, (BLOCK, 2 * BLOCK), 1) == 0
        for start in range(2):
            for h in range(N_Q_HEADS):
                kvh, g = divmod(h, Q_PER_KV)
                rows = pl.ds(g * BLOCK, BLOCK)
                bias_buf[start, kvh, rows, :] = jnp.where(
                    sink_slot, sinks_ref[layer, h] * LOG2E, bias_ref[start, kvh, rows, :])
        ubuf[0:HALO_U, :] = jnp.zeros((HALO_U, POOL_WIDTH), _F32)
        hbuf[0:HALO_H, :] = jnp.zeros((HALO_H, CONV_WIDTH), _F32)
        kbuf[:, 0:BLOCK, :] = jnp.zeros((N_KV_HEADS, BLOCK, HEAD_DIM), _BF16)
        lane_t = lax.broadcasted_iota(jnp.int32, (BLOCK, LANES), 1)
        ones_hi = jnp.where(lane_t < HEAD_DIM, 0.0, 1.0).astype(_BF16)
        for kvh in range(N_KV_HEADS):
            v_buf[kvh, 0:BLOCK, :] = ones_hi

    @pl.when(jnp.logical_not(first_tile))
    def _():
        ubuf[0:HALO_U, :] = ubuf[TS:TS + HALO_U, :]
        hbuf[0:HALO_H, :] = hbuf[TS:TS + HALO_H, :]
        kbuf[:, 0:BLOCK, :] = kbuf[:, TS:TS + BLOCK, :]
        v_buf[:, 0:BLOCK, :] = v_buf[:, TS:TS + BLOCK, :]

    def norm_chunk(c):
        r = c * CHUNK
        xc = x_ref[0, pl.ds(r, CHUNK), :]
        ms = jnp.mean(xc * xc, axis=-1, keepdims=True)
        y = xc * lax.rsqrt(ms + EPS) * lng_ref[layer:layer + 1, :]
        hn_ref[pl.ds(r, CHUNK), :] = y.astype(_BF16)

    _unrolled(TS // CHUNK, norm_chunk)

    def proj(c0, width, r0=0, rows=TS):
        w = win_bf[:, c0:c0 + width]
        return jnp.dot(hn_ref[pl.ds(r0, rows), :], w, preferred_element_type=_F32)

    for r0 in range(0, TS, TS // 2):
        rows = pl.ds(HALO_H + r0, TS // 2)
        hbuf[rows, :] = proj(C_CA, CONV_WIDTH, r0, TS // 2)
        hbuf[rows, :] = hbuf[rows, :] * jax.nn.sigmoid(proj(C_CB, CONV_WIDTH, r0, TS // 2))

    kv = proj(C_K, 2 * KV_WIDTH)
    keys = kv[:, 0:KV_WIDTH]
    vals = kv[:, KV_WIDTH:2 * KV_WIDTH]
    vals_swapped = pltpu.roll(vals, HEAD_DIM, axis=1)
    lo = lax.broadcasted_iota(jnp.int32, (TS, LANES), 1) < HEAD_DIM
    for kvh in range(N_KV_HEADS):
        kbuf[kvh, BLOCK:BLOCK + TS, :] = keys[:, kvh * HEAD_DIM:(kvh + 1) * HEAD_DIM].astype(_BF16)
        own_lo = vals if kvh == 0 else vals_swapped
        v_buf[kvh, BLOCK:BLOCK + TS, :] = jnp.where(lo, own_lo, 1.0).astype(_BF16)

    heads_per_group = MXU_COLS // HEAD_DIM
    for half in range(ATTN_WIDTH // MXU_COLS):
        qh = proj(C_Q + half * MXU_COLS, MXU_COLS) * (HEAD_DIM ** -0.5 * LOG2E)
        for i in range(heads_per_group):
            qbuf[half * heads_per_group + i, :, :] = qh[:, i * HEAD_DIM:(i + 1) * HEAD_DIM].astype(_BF16)
    for half in range(ATTN_WIDTH // MXU_COLS):
        ga_buf[:, half * MXU_COLS:(half + 1) * MXU_COLS] = _silu(proj(C_GATTN + half * MXU_COLS, MXU_COLS))

    ubuf[HALO_U:HALO_U + TS, :] = proj(C_UPOOL, POOL_WIDTH)
    gp_buf[...] = _silu(proj(C_GPOOL, POOL_WIDTH))
    gc_buf[...] = _silu(proj(C_GCONV, CONV_WIDTH))

    inv_w = [jnp.where(low_half, 1.0 / POOL_WINDOWS[0], 1.0 / POOL_WINDOWS[1]).astype(_F32),
             jnp.where(low_half, 1.0 / POOL_WINDOWS[2], 1.0 / POOL_WINDOWS[3]).astype(_F32)]

    def pool_chunk(c):
        r = c * CHUNK
        n_in = GROUPS + HALO_U // SUBLANES
        for col in range(2):
            lanes = slice(col * LANES, (col + 1) * LANES)
            u = [ubuf[pl.ds(r + SUBLANES * i, SUBLANES), lanes] for i in range(n_in)]
            s2 = [a + b for a, b in zip(u, _shift_down(u, 1, row))]
            s4 = [a + b for a, b in zip(s2, _shift_down(s2, 2, row))]
            if col == 0:
                wide, narrow = s4, s2
            else:
                s8 = [a + b for a, b in zip(s4, _shift_down(s4, 4, row))]
                s16 = [s8[0]] + [s8[i] + s8[i - 1] for i in range(1, n_in)]
                wide, narrow = s16, s8
            pieces = []
            for g in range(GROUPS):
                i = g + HALO_U // SUBLANES
                inv = inv_w[col]
                if c == 0 and g < HALO_U // SUBLANES:
                    tab = invcnt_ref[SUBLANES * g:SUBLANES * (g + 1), lanes]
                    inv = jnp.where(first_tile, tab, inv)
                pooled = jnp.where(low_half, narrow[i], wide[i]) * inv
                pieces.append(pooled - u[i])
            zp_buf[pl.ds(r, CHUNK), lanes] = jnp.concatenate(pieces, axis=0).astype(_BF16)

    taps_by_b = [[] for _ in range(SUBLANES)]
    for k in range(CONV_KERNEL):
        a, b = divmod(k + HALO_H - (CONV_KERNEL - 1), SUBLANES)
        taps_by_b[b].append((a, k))
    n_h = GROUPS + HALO_H // SUBLANES

    def conv_chunk(c):
        r = c * CHUNK
        outs = []
        for col in range(2):
            lanes = slice(col * LANES, (col + 1) * LANES)
            h = [hbuf[pl.ds(r + SUBLANES * m, SUBLANES), lanes] for m in range(n_h)]
            acc = None
            for b in range(SUBLANES):
                n_p = GROUPS if b == 0 else GROUPS + 1
                part = []
                for m in range(n_p):
                    p = None
                    for a, k in taps_by_b[b]:
                        term = h[m + a] * dw_ref[0, k:k + 1, lanes]
                        p = term if p is None else p + term
                    part.append(p)
                if b == 0:
                    acc = part
                else:
                    rolled = [pltpu.roll(p, SUBLANES - b, axis=0) for p in part]
                    keep = row < (SUBLANES - b)
                    acc = [acc[g] + jnp.where(keep, rolled[g], rolled[g + 1]) for g in range(GROUPS)]
            outs.append(jnp.concatenate(acc, axis=0))
        conv = jnp.concatenate(outs, axis=1) + cb_ref[layer:layer + 1, :]
        mu = jnp.mean(conv, axis=-1, keepdims=True)
        cen = conv - mu
        var = jnp.mean(cen * cen, axis=-1, keepdims=True)
        z = cen * lax.rsqrt(var + EPS) * clg_ref[layer:layer + 1, :] + clb_ref[layer:layer + 1, :]
        zc_buf[pl.ds(r, CHUNK), :] = _silu(z).astype(_BF16)

    _unrolled(TS // CHUNK, conv_chunk)
    _unrolled(TS // CHUNK, pool_chunk)

    n_local = POOL_WIDTH + CONV_WIDTH

    top_rows = lax.broadcasted_iota(jnp.int32, (BF16_ROWS, LANES), 0)
    top_lanes = lax.broadcasted_iota(jnp.int32, (BF16_ROWS, LANES), 1)
    sink_value_slot = jnp.logical_and(top_rows == 0, top_lanes < HEAD_DIM)
    lo_block = lax.broadcasted_iota(jnp.int32, (BLOCK, LANES), 1) < HEAD_DIM
    sink_key_slot = lax.broadcasted_iota(jnp.int32, (BF16_ROWS, HEAD_DIM), 0) == 0

    def window_keys(kvh, r):
        top = kbuf[kvh, pl.ds(r, BF16_ROWS), :].astype(_F32)
        top = jnp.where(sink_key_slot, 0.0, top).astype(_BF16)
        return jnp.concatenate([top, kbuf[kvh, pl.ds(r + BF16_ROWS, 2 * BLOCK - BF16_ROWS), :]], axis=0)

    def scores(n):
        r = n * BLOCK
        seq_start = first_tile.astype(jnp.int32) if n == 0 else 0
        out = []
        for kvh in range(N_KV_HEADS):
            k2 = window_keys(kvh, r)
            q4 = jnp.concatenate([qbuf[kvh * Q_PER_KV + g, pl.ds(r, BLOCK), :] for g in range(Q_PER_KV)], axis=0)
            s = lax.dot_general(q4, k2, (((1,), (1,)), ((), ())), preferred_element_type=_F32)
            out.append(s + bias_buf[seq_start, kvh])
        return out

    def window_values(kvh, r):
        top = v_buf[kvh, pl.ds(r, BF16_ROWS), :].astype(_F32)
        top = jnp.where(sink_value_slot, 0.0, top).astype(_BF16)
        return jnp.concatenate([top, v_buf[kvh, pl.ds(r + BF16_ROWS, 2 * BLOCK - BF16_ROWS), :]], axis=0)

    def softmax_pv(n, s_list):
        r = n * BLOCK
        for kvh in range(N_KV_HEADS):
            probs = []
            for g in range(Q_PER_KV):
                sg = s_list[kvh][g * BLOCK:(g + 1) * BLOCK]
                m = jnp.max(sg, axis=-1, keepdims=True)
                probs.append(jnp.exp2(sg - m).astype(_BF16))
            o = jnp.dot(jnp.concatenate(probs, axis=0), window_values(kvh, r), preferred_element_type=_F32)
            o_swapped = pltpu.roll(o, HEAD_DIM, axis=1)
            for pair in range(Q_PER_KV // 2):
                even = slice(2 * pair * BLOCK, (2 * pair + 1) * BLOCK)
                odd = slice((2 * pair + 1) * BLOCK, (2 * pair + 2) * BLOCK)
                num = jnp.where(lo_block, o[even], o_swapped[odd])
                den = jnp.where(lo_block, o_swapped[even], o[odd])
                c0 = (kvh * (Q_PER_KV // 2) + pair) * LANES
                gate = ga_buf[pl.ds(r, BLOCK), c0:c0 + LANES]
                mix_ref[pl.ds(r, BLOCK), n_local + c0:n_local + c0 + LANES] = (num / den * gate).astype(_BF16)

    def out_attn(r0, rows):
        for q in range(D_MODEL // MXU_COLS):
            cols = slice(q * MXU_COLS, (q + 1) * MXU_COLS)
            o_ref[0, pl.ds(r0, rows), cols] = x_ref[0, pl.ds(r0, rows), cols] + jnp.dot(
                mix_ref[pl.ds(r0, rows), n_local:D_MIX], wout_bf[n_local:D_MIX, cols],
                preferred_element_type=_F32)

    s_cur = scores(0)
    for n in range(NB):
        s_next = scores(n + 1) if n + 1 < NB else None
        if n >= OUT_BLOCKS and n % OUT_BLOCKS == 0:
            out_attn((n - OUT_BLOCKS) * BLOCK, OUT_BLOCKS * BLOCK)
        softmax_pv(n, s_cur)
        s_cur = s_next
    out_attn((NB - OUT_BLOCKS) * BLOCK, OUT_BLOCKS * BLOCK)

    y_pool = jnp.dot(zp_buf[...], poolw_ref[0].astype(_BF16), preferred_element_type=_F32)
    y_pool = y_pool * pscale_ref[layer:layer + 1, :]
    mix_ref[:, 0:POOL_WIDTH] = (y_pool * gp_buf[...]).astype(_BF16)
    y_conv = jnp.dot(zc_buf[...], pw_ref[0].astype(_BF16), preferred_element_type=_F32)
    mix_ref[:, POOL_WIDTH:POOL_WIDTH + CONV_WIDTH] = (y_conv * gc_buf[...]).astype(_BF16)

    def final_chunk(c):
        r = c * CHUNK
        xc = o_ref[0, pl.ds(r, CHUNK), :]
        ms = jnp.mean(xc * xc, axis=-1, keepdims=True)
        o_ref[0, pl.ds(r, CHUNK), :] = xc * lax.rsqrt(ms + EPS) * fg_ref[...]

    half_rows = TS
    for r0 in range(0, TS, half_rows):
        rows = pl.ds(r0, half_rows)
        for q in range(D_MODEL // MXU_COLS):
            cols = slice(q * MXU_COLS, (q + 1) * MXU_COLS)
            o_ref[0, rows, cols] = o_ref[0, rows, cols] + jnp.dot(
                mix_ref[rows, 0:n_local], wout_bf[0:n_local, cols], preferred_element_type=_F32)
        if final:
            for c in range(r0 // CHUNK, (r0 + half_rows) // CHUNK):
                final_chunk(c)


def _bias_table():
    i = np.arange(BLOCK)[:, None]
    jj = np.arange(2 * BLOCK)[None, :]
    dist = BLOCK + i - jj
    in_band = (dist >= 0) & (dist < WINDOW)
    slopes = np.asarray([2.0 ** (-8.0 * (h + 1) / N_Q_HEADS) for h in range(N_Q_HEADS)], dtype=np.float32)
    tab = np.empty((2, N_KV_HEADS, Q_PER_KV * BLOCK, 2 * BLOCK), np.float32)
    for start in range(2):
        valid = in_band & ~((jj < BLOCK) & (start == 1))
        for h in range(N_Q_HEADS):
            bias = (-slopes[h] * LOG2E) * dist.astype(np.float64)
            kvh, g = divmod(h, Q_PER_KV)
            tab[start, kvh, g * BLOCK:(g + 1) * BLOCK] = np.where(valid, bias, -1e30).astype(np.float32)
    return tab


def _inv_count_table():
    t = np.arange(HALO_U, dtype=np.float32)[:, None]
    gw = POOL_WIDTH // POOL_GROUPS
    wnd = np.repeat(np.asarray(POOL_WINDOWS, np.float32), gw)[None, :]
    return (1.0 / np.minimum(t + 1.0, wnd)).astype(np.float32)


def _layer_call(batch, seq, layer, final):
    n_tiles = seq // TS
    const = lambda shape: pl.BlockSpec(shape, lambda b, j: (0,) * len(shape), pipeline_mode=pl.Buffered(1))
    per_layer = lambda shape: pl.BlockSpec((1,) + shape, lambda b, j: (layer,) + (0,) * len(shape),
                                           pipeline_mode=pl.Buffered(1))
    in_specs = [
        pl.BlockSpec((1, TS, D_MODEL), lambda b, j: (b, j, 0)),
        const((DEPTH, D_MODEL)),
        per_layer((D_MODEL, D_IN)),
        per_layer((POOL_WIDTH, POOL_WIDTH)),
        const((DEPTH, POOL_WIDTH)),
        const((HALO_U, POOL_WIDTH)),
        per_layer((CONV_KERNEL, CONV_WIDTH)),
        const((DEPTH, CONV_WIDTH)),
        const((DEPTH, CONV_WIDTH)),
        const((DEPTH, CONV_WIDTH)),
        per_layer((CONV_WIDTH, CONV_WIDTH)),
        pl.BlockSpec(memory_space=pltpu.SMEM),
        const((2, N_KV_HEADS, Q_PER_KV * BLOCK, 2 * BLOCK)),
        per_layer((D_MIX, D_MODEL)),
        const((1, D_MODEL)),
    ]
    scratch = [
        pltpu.VMEM((TS, D_MODEL), _BF16),
        pltpu.VMEM((TS + HALO_U, POOL_WIDTH), _F32),
        pltpu.VMEM((TS + HALO_H, CONV_WIDTH), _F32),
        pltpu.VMEM((TS, POOL_WIDTH), _F32),
        pltpu.VMEM((TS, CONV_WIDTH), _F32),
        pltpu.VMEM((TS, ATTN_WIDTH), _F32),
        pltpu.VMEM((TS, POOL_WIDTH), _BF16),
        pltpu.VMEM((TS, CONV_WIDTH), _BF16),
        pltpu.VMEM((N_KV_HEADS, TS + BLOCK, HEAD_DIM), _BF16),
        pltpu.VMEM((N_KV_HEADS, TS + BLOCK, LANES), _BF16),
        pltpu.VMEM((N_Q_HEADS, TS, HEAD_DIM), _BF16),
        pltpu.VMEM((TS, D_MIX), _BF16),
        pltpu.VMEM((2, N_KV_HEADS, Q_PER_KV * BLOCK, 2 * BLOCK), _F32),
        pltpu.VMEM((D_MODEL, D_IN), _BF16),
        pltpu.VMEM((D_MIX, D_MODEL), _BF16),
    ]
    return pl.pallas_call(
        functools.partial(_layer_body, layer=layer, final=final),
        out_shape=jax.ShapeDtypeStruct((batch, seq, D_MODEL), _F32),
        grid=(batch, n_tiles),
        in_specs=in_specs,
        out_specs=pl.BlockSpec((1, TS, D_MODEL), lambda b, j: (b, j, 0)),
        scratch_shapes=scratch,
        compiler_params=pltpu.CompilerParams(
            dimension_semantics=("arbitrary", "arbitrary"),
            vmem_limit_bytes=VMEM_LIMIT_BYTES,
        ),
        name="hybrid_layer_final" if final else "hybrid_layer",
    )


def kernel(x, ln_g, w_in, pool_w, pool_scale, conv_dw, conv_b, conv_ln_g, conv_ln_b, conv_pw, attn_sinks, w_out, final_g):
    batch, seq, d_model = x.shape
    assert d_model == D_MODEL and seq % TS == 0 and TS % BLOCK == 0
    bias = jnp.asarray(_bias_table())
    inv_cnt = jnp.asarray(_inv_count_table())
    poolw_bd = jnp.stack([jax.scipy.linalg.block_diag(*[pool_w[l, g] for g in range(POOL_GROUPS)])
                          for l in range(DEPTH)])
    params = (ln_g, w_in, poolw_bd, pool_scale, inv_cnt, conv_dw, conv_b, conv_ln_g, conv_ln_b, conv_pw,
              attn_sinks, bias, w_out, final_g.reshape(1, -1))
    for l in range(DEPTH):
        x = _layer_call(batch, seq, l, final=(l == DEPTH - 1))(x, *params)
    return x
```

```python
import functools

import numpy as np
import jax
import jax.numpy as jnp
from jax import lax
from jax.experimental import pallas as pl
from jax.experimental.pallas import tpu as pltpu

D_MODEL = 1024
DEPTH = 2
POOL_WIDTH = 256
POOL_GROUPS = 4
POOL_WINDOWS = (2, 4, 8, 16)
CONV_WIDTH = 256
CONV_KERNEL = 31
HEAD_DIM = 64
N_Q_HEADS = 8
N_KV_HEADS = 2
Q_PER_KV = N_Q_HEADS // N_KV_HEADS
ATTN_WIDTH = N_Q_HEADS * HEAD_DIM
KV_WIDTH = N_KV_HEADS * HEAD_DIM
WINDOW = 128
BLOCK = 128
D_MIX = POOL_WIDTH + CONV_WIDTH + ATTN_WIDTH
D_IN = 2 * POOL_WIDTH + 3 * CONV_WIDTH + 2 * ATTN_WIDTH + 2 * KV_WIDTH
EPS = 1e-6

C_UPOOL = 0
C_GPOOL = C_UPOOL + POOL_WIDTH
C_CA = C_GPOOL + POOL_WIDTH
C_CB = C_CA + CONV_WIDTH
C_GCONV = C_CB + CONV_WIDTH
C_Q = C_GCONV + CONV_WIDTH
C_K = C_Q + ATTN_WIDTH
C_V = C_K + KV_WIDTH
C_GATTN = C_V + KV_WIDTH

SUBLANES = 8
LANES = 128
BF16_ROWS = 16
MXU_COLS = 256
TS = 1024
NB = TS // BLOCK
OUT_BLOCKS = 8
GLU_ROW_GROUPS = (128, 128, 256, 512)
CHUNK = 64
GROUPS = CHUNK // SUBLANES
HALO_U = 16
HALO_H = 32
VMEM_LIMIT_BYTES = 63 * 1024 * 1024
LOG2E = 1.4426950408889634

_F32 = jnp.float32
_BF16 = jnp.bfloat16


def _silu(v):
    return v * jax.nn.sigmoid(v)


def _unrolled(n, body):
    for c in range(n):
        body(c)


def _shift_down(groups, d, row):
    rolled = [pltpu.roll(g, d, axis=0) for g in groups]
    take_prev = row < d
    out = [rolled[0]]
    for i in range(1, len(groups)):
        out.append(jnp.where(take_prev, rolled[i - 1], rolled[i]))
    return out


def _layer_body(x_ref, lng_ref, win_ref, poolw_ref, pscale_ref, invcnt_ref, dw_ref, cb_ref, clg_ref, clb_ref,
                pw_ref, sinks_ref, bias_ref, wout_ref, fg_ref, o_ref,
                hn_ref, ubuf, hbuf, gp_buf, gc_buf, ga_buf, zp_buf, zc_buf, kbuf, ve_buf, vo_buf, qbuf, mix_ref,
                bias_buf, win_bf, wout_bf, *, layer, final):
    j = pl.program_id(1)
    first_tile = j == 0

    @pl.when(jnp.logical_and(pl.program_id(0) == 0, first_tile))
    def _():
        for c in range(D_IN // MXU_COLS):
            cols = slice(c * MXU_COLS, (c + 1) * MXU_COLS)
            win_bf[:, cols] = win_ref[0, :, cols].astype(_BF16)
        for c in range(D_MODEL // MXU_COLS):
            cols = slice(c * MXU_COLS, (c + 1) * MXU_COLS)
            wout_bf[:, cols] = wout_ref[0, :, cols].astype(_BF16)

    row = lax.broadcasted_iota(jnp.int32, (SUBLANES, LANES), 0)
    lane = lax.broadcasted_iota(jnp.int32, (SUBLANES, LANES), 1)
    low_half = lane < HEAD_DIM

    @pl.when(first_tile)
    def _():
        sink_slot = lax.broadcasted_iota(jnp.int32, (BLOCK, 2 * BLOCK), 1) == 0
        for start in range(2):
            for h in range(N_Q_HEADS):
                kvh, g = divmod(h, Q_PER_KV)
                rows = pl.ds(g * BLOCK, BLOCK)
                bias_buf[start, kvh, rows, :] = jnp.where(
                    sink_slot, sinks_ref[layer, h] * LOG2E, bias_ref[start, kvh, rows, :])
        ubuf[0:HALO_U, :] = jnp.zeros((HALO_U, POOL_WIDTH), _F32)
        hbuf[0:HALO_H, :] = jnp.zeros((HALO_H, CONV_WIDTH), _F32)
        kbuf[:, 0:BLOCK, :] = jnp.zeros((N_KV_HEADS, BLOCK, HEAD_DIM), _BF16)
        zeros = jnp.zeros((TS + BLOCK, LANES), _BF16)
        lane_t = lax.broadcasted_iota(jnp.int32, (TS + BLOCK, LANES), 1)
        ones_lo = jnp.where(lane_t < HEAD_DIM, 1.0, 0.0).astype(_BF16)
        ones_hi = jnp.where(lane_t < HEAD_DIM, 0.0, 1.0).astype(_BF16)
        for kvh in range(N_KV_HEADS):
            ve_buf[kvh, :, 0:LANES] = zeros
            vo_buf[kvh, :, 0:LANES] = zeros
            ve_buf[kvh, :, LANES:2 * LANES] = ones_lo
            vo_buf[kvh, :, LANES:2 * LANES] = ones_hi

    @pl.when(jnp.logical_not(first_tile))
    def _():
        ubuf[0:HALO_U, :] = ubuf[TS:TS + HALO_U, :]
        hbuf[0:HALO_H, :] = hbuf[TS:TS + HALO_H, :]
        kbuf[:, 0:BLOCK, :] = kbuf[:, TS:TS + BLOCK, :]
        ve_buf[:, 0:BLOCK, 0:LANES] = ve_buf[:, TS:TS + BLOCK, 0:LANES]
        vo_buf[:, 0:BLOCK, 0:LANES] = vo_buf[:, TS:TS + BLOCK, 0:LANES]

    def norm_chunk(c):
        r = c * CHUNK
        xc = x_ref[0, pl.ds(r, CHUNK), :]
        ms = jnp.mean(xc * xc, axis=-1, keepdims=True)
        y = xc * lax.rsqrt(ms + EPS) * lng_ref[layer:layer + 1, :]
        hn_ref[pl.ds(r, CHUNK), :] = y.astype(_BF16)

    _unrolled(TS // CHUNK, norm_chunk)

    def proj(c0, width, r0=0, rows=TS):
        w = win_bf[:, c0:c0 + width]
        return jnp.dot(hn_ref[pl.ds(r0, rows), :], w, preferred_element_type=_F32)

    r0 = 0
    for n_rows in GLU_ROW_GROUPS:
        rows = pl.ds(HALO_H + r0, n_rows)
        hbuf[rows, :] = proj(C_CA, CONV_WIDTH, r0, n_rows)
        hbuf[rows, :] = hbuf[rows, :] * jax.nn.sigmoid(proj(C_CB, CONV_WIDTH, r0, n_rows))
        r0 += n_rows

    kv = proj(C_K, 2 * KV_WIDTH)
    keys = kv[:, 0:KV_WIDTH]
    vals = kv[:, KV_WIDTH:2 * KV_WIDTH]
    vals_swapped = pltpu.roll(vals, HEAD_DIM, axis=1)
    lo = lax.broadcasted_iota(jnp.int32, (TS, LANES), 1) < HEAD_DIM
    for kvh in range(N_KV_HEADS):
        kbuf[kvh, BLOCK:BLOCK + TS, :] = keys[:, kvh * HEAD_DIM:(kvh + 1) * HEAD_DIM].astype(_BF16)
        own_lo, own_hi = (vals, vals_swapped) if kvh == 0 else (vals_swapped, vals)
        ve_buf[kvh, BLOCK:BLOCK + TS, 0:LANES] = jnp.where(lo, own_lo, 0.0).astype(_BF16)
        vo_buf[kvh, BLOCK:BLOCK + TS, 0:LANES] = jnp.where(lo, 0.0, own_hi).astype(_BF16)

    heads_per_group = MXU_COLS // HEAD_DIM
    for half in range(ATTN_WIDTH // MXU_COLS):
        qh = proj(C_Q + half * MXU_COLS, MXU_COLS) * (HEAD_DIM ** -0.5 * LOG2E)
        for i in range(heads_per_group):
            qbuf[half * heads_per_group + i, :, :] = qh[:, i * HEAD_DIM:(i + 1) * HEAD_DIM].astype(_BF16)
    for half in range(ATTN_WIDTH // MXU_COLS):
        ga_buf[:, half * MXU_COLS:(half + 1) * MXU_COLS] = _silu(proj(C_GATTN + half * MXU_COLS, MXU_COLS))

    ubuf[HALO_U:HALO_U + TS, :] = proj(C_UPOOL, POOL_WIDTH)
    gp_buf[...] = _silu(proj(C_GPOOL, POOL_WIDTH))
    gc_buf[...] = _silu(proj(C_GCONV, CONV_WIDTH))

    inv_w = [jnp.where(low_half, 1.0 / POOL_WINDOWS[0], 1.0 / POOL_WINDOWS[1]).astype(_F32),
             jnp.where(low_half, 1.0 / POOL_WINDOWS[2], 1.0 / POOL_WINDOWS[3]).astype(_F32)]

    def pool_chunk(c):
        r = c * CHUNK
        n_in = GROUPS + HALO_U // SUBLANES
        for col in range(2):
            lanes = slice(col * LANES, (col + 1) * LANES)
            u = [ubuf[pl.ds(r + SUBLANES * i, SUBLANES), lanes] for i in range(n_in)]
            s2 = [a + b for a, b in zip(u, _shift_down(u, 1, row))]
            s4 = [a + b for a, b in zip(s2, _shift_down(s2, 2, row))]
            if col == 0:
                wide, narrow = s4, s2
            else:
                s8 = [a + b for a, b in zip(s4, _shift_down(s4, 4, row))]
                s16 = [s8[0]] + [s8[i] + s8[i - 1] for i in range(1, n_in)]
                wide, narrow = s16, s8
            pieces = []
            for g in range(GROUPS):
                i = g + HALO_U // SUBLANES
                inv = inv_w[col]
                if c == 0 and g < HALO_U // SUBLANES:
                    tab = invcnt_ref[SUBLANES * g:SUBLANES * (g + 1), lanes]
                    inv = jnp.where(first_tile, tab, inv)
                pooled = jnp.where(low_half, narrow[i], wide[i]) * inv
                pieces.append(pooled - u[i])
            zp_buf[pl.ds(r, CHUNK), lanes] = jnp.concatenate(pieces, axis=0).astype(_BF16)

    taps_by_b = [[] for _ in range(SUBLANES)]
    for k in range(CONV_KERNEL):
        a, b = divmod(k + HALO_H - (CONV_KERNEL - 1), SUBLANES)
        taps_by_b[b].append((a, k))
    n_h = GROUPS + HALO_H // SUBLANES

    def conv_chunk(c):
        r = c * CHUNK
        outs = []
        for col in range(2):
            lanes = slice(col * LANES, (col + 1) * LANES)
            h = [hbuf[pl.ds(r + SUBLANES * m, SUBLANES), lanes] for m in range(n_h)]
            acc = None
            for b in range(SUBLANES):
                n_p = GROUPS if b == 0 else GROUPS + 1
                part = []
                for m in range(n_p):
                    p = None
                    for a, k in taps_by_b[b]:
                        term = h[m + a] * dw_ref[0, k:k + 1, lanes]
                        p = term if p is None else p + term
                    part.append(p)
                if b == 0:
                    acc = part
                else:
                    rolled = [pltpu.roll(p, SUBLANES - b, axis=0) for p in part]
                    keep = row < (SUBLANES - b)
                    acc = [acc[g] + jnp.where(keep, rolled[g], rolled[g + 1]) for g in range(GROUPS)]
            outs.append(jnp.concatenate(acc, axis=0))
        conv = jnp.concatenate(outs, axis=1) + cb_ref[layer:layer + 1, :]
        mu = jnp.mean(conv, axis=-1, keepdims=True)
        cen = conv - mu
        var = jnp.mean(cen * cen, axis=-1, keepdims=True)
        z = cen * lax.rsqrt(var + EPS) * clg_ref[layer:layer + 1, :] + clb_ref[layer:layer + 1, :]
        zc_buf[pl.ds(r, CHUNK), :] = _silu(z).astype(_BF16)

    _unrolled(TS // CHUNK, conv_chunk)
    _unrolled(TS // CHUNK, pool_chunk)

    n_local = POOL_WIDTH + CONV_WIDTH

    top_rows = lax.broadcasted_iota(jnp.int32, (BF16_ROWS, 2 * LANES), 0)
    top_lanes = lax.broadcasted_iota(jnp.int32, (BF16_ROWS, 2 * LANES), 1)
    sink_value_slot = jnp.logical_and(top_rows == 0, top_lanes < LANES)
    sink_key_slot = lax.broadcasted_iota(jnp.int32, (BF16_ROWS, HEAD_DIM), 0) == 0

    def window_keys(kvh, r):
        top = kbuf[kvh, pl.ds(r, BF16_ROWS), :].astype(_F32)
        top = jnp.where(sink_key_slot, 0.0, top).astype(_BF16)
        return jnp.concatenate([top, kbuf[kvh, pl.ds(r + BF16_ROWS, 2 * BLOCK - BF16_ROWS), :]], axis=0)

    def scores(n):
        r = n * BLOCK
        seq_start = first_tile.astype(jnp.int32) if n == 0 else 0
        out = []
        for kvh in range(N_KV_HEADS):
            k2 = window_keys(kvh, r)
            q4 = jnp.concatenate([qbuf[kvh * Q_PER_KV + g, pl.ds(r, BLOCK), :] for g in range(Q_PER_KV)], axis=0)
            s = lax.dot_general(q4, k2, (((1,), (1,)), ((), ())), preferred_element_type=_F32)
            out.append(s + bias_buf[seq_start, kvh])
        return out

    def window_values(buf, kvh, r):
        top = buf[kvh, pl.ds(r, BF16_ROWS), :].astype(_F32)
        top = jnp.where(sink_value_slot, 0.0, top).astype(_BF16)
        return jnp.concatenate([top, buf[kvh, pl.ds(r + BF16_ROWS, 2 * BLOCK - BF16_ROWS), :]], axis=0)

    def softmax_pv(n, s_list):
        r = n * BLOCK
        for kvh in range(N_KV_HEADS):
            probs = []
            for g in range(Q_PER_KV):
                sg = s_list[kvh][g * BLOCK:(g + 1) * BLOCK]
                m = jnp.max(sg, axis=-1, keepdims=True)
                probs.append(jnp.exp2(sg - m).astype(_BF16))
            p_even = jnp.concatenate([probs[0], probs[2]], axis=0)
            p_odd = jnp.concatenate([probs[1], probs[3]], axis=0)
            o = (jnp.dot(p_even, window_values(ve_buf, kvh, r), preferred_element_type=_F32)
                 + jnp.dot(p_odd, window_values(vo_buf, kvh, r), preferred_element_type=_F32))
            res = o[:, 0:LANES] / o[:, LANES:2 * LANES]
            for pair in range(2):
                c0 = (kvh * 2 + pair) * LANES
                gate = ga_buf[pl.ds(r, BLOCK), c0:c0 + LANES]
                mix_ref[pl.ds(r, BLOCK), n_local + c0:n_local + c0 + LANES] = (
                    res[pair * BLOCK:(pair + 1) * BLOCK] * gate).astype(_BF16)

    def out_attn(r0, rows):
        for q in range(D_MODEL // MXU_COLS):
            cols = slice(q * MXU_COLS, (q + 1) * MXU_COLS)
            o_ref[0, pl.ds(r0, rows), cols] = x_ref[0, pl.ds(r0, rows), cols] + jnp.dot(
                mix_ref[pl.ds(r0, rows), n_local:D_MIX], wout_bf[n_local:D_MIX, cols],
                preferred_element_type=_F32)

    s_cur = scores(0)
    for n in range(NB):
        s_next = scores(n + 1) if n + 1 < NB else None
        if n >= OUT_BLOCKS and n % OUT_BLOCKS == 0:
            out_attn((n - OUT_BLOCKS) * BLOCK, OUT_BLOCKS * BLOCK)
        softmax_pv(n, s_cur)
        s_cur = s_next
    out_attn((NB - OUT_BLOCKS) * BLOCK, OUT_BLOCKS * BLOCK)

    y_pool = jnp.dot(zp_buf[...], poolw_ref[0].astype(_BF16), preferred_element_type=_F32)
    y_pool = y_pool * pscale_ref[layer:layer + 1, :]
    mix_ref[:, 0:POOL_WIDTH] = (y_pool * gp_buf[...]).astype(_BF16)
    y_conv = jnp.dot(zc_buf[...], pw_ref[0].astype(_BF16), preferred_element_type=_F32)
    mix_ref[:, POOL_WIDTH:POOL_WIDTH + CONV_WIDTH] = (y_conv * gc_buf[...]).astype(_BF16)

    def final_chunk(c):
        r = c * CHUNK
        xc = o_ref[0, pl.ds(r, CHUNK), :]
        ms = jnp.mean(xc * xc, axis=-1, keepdims=True)
        o_ref[0, pl.ds(r, CHUNK), :] = xc * lax.rsqrt(ms + EPS) * fg_ref[...]

    half_rows = TS
    for r0 in range(0, TS, half_rows):
        rows = pl.ds(r0, half_rows)
        for q in range(D_MODEL // MXU_COLS):
            cols = slice(q * MXU_COLS, (q + 1) * MXU_COLS)
            o_ref[0, rows, cols] = o_ref[0, rows, cols] + jnp.dot(
                mix_ref[rows, 0:n_local], wout_bf[0:n_local, cols], preferred_element_type=_F32)
        if final:
            for c in range(r0 // CHUNK, (r0 + half_rows) // CHUNK):
                final_chunk(c)


def _bias_table():
    i = np.arange(BLOCK)[:, None]
    jj = np.arange(2 * BLOCK)[None, :]
    dist = BLOCK + i - jj
    in_band = (dist >= 0) & (dist < WINDOW)
    slopes = np.asarray([2.0 ** (-8.0 * (h + 1) / N_Q_HEADS) for h in range(N_Q_HEADS)], dtype=np.float32)
    tab = np.empty((2, N_KV_HEADS, Q_PER_KV * BLOCK, 2 * BLOCK), np.float32)
    for start in range(2):
        valid = in_band & ~((jj < BLOCK) & (start == 1))
        for h in range(N_Q_HEADS):
            bias = (-slopes[h] * LOG2E) * dist.astype(np.float64)
            kvh, g = divmod(h, Q_PER_KV)
            tab[start, kvh, g * BLOCK:(g + 1) * BLOCK] = np.where(valid, bias, -1e30).astype(np.float32)
    return tab


def _inv_count_table():
    t = np.arange(HALO_U, dtype=np.float32)[:, None]
    gw = POOL_WIDTH // POOL_GROUPS
    wnd = np.repeat(np.asarray(POOL_WINDOWS, np.float32), gw)[None, :]
    return (1.0 / np.minimum(t + 1.0, wnd)).astype(np.float32)


def _layer_call(batch, seq, layer, final):
    n_tiles = seq // TS
    const = lambda shape: pl.BlockSpec(shape, lambda b, j: (0,) * len(shape), pipeline_mode=pl.Buffered(1))
    per_layer = lambda shape: pl.BlockSpec((1,) + shape, lambda b, j: (layer,) + (0,) * len(shape),
                                           pipeline_mode=pl.Buffered(1))
    in_specs = [
        pl.BlockSpec((1, TS, D_MODEL), lambda b, j: (b, j, 0)),
        const((DEPTH, D_MODEL)),
        per_layer((D_MODEL, D_IN)),
        per_layer((POOL_WIDTH, POOL_WIDTH)),
        const((DEPTH, POOL_WIDTH)),
        const((HALO_U, POOL_WIDTH)),
        per_layer((CONV_KERNEL, CONV_WIDTH)),
        const((DEPTH, CONV_WIDTH)),
        const((DEPTH, CONV_WIDTH)),
        const((DEPTH, CONV_WIDTH)),
        per_layer((CONV_WIDTH, CONV_WIDTH)),
        pl.BlockSpec(memory_space=pltpu.SMEM),
        const((2, N_KV_HEADS, Q_PER_KV * BLOCK, 2 * BLOCK)),
        per_layer((D_MIX, D_MODEL)),
        const((1, D_MODEL)),
    ]
    scratch = [
        pltpu.VMEM((TS, D_MODEL), _BF16),
        pltpu.VMEM((TS + HALO_U, POOL_WIDTH), _F32),
        pltpu.VMEM((TS + HALO_H, CONV_WIDTH), _F32),
        pltpu.VMEM((TS, POOL_WIDTH), _F32),
        pltpu.VMEM((TS, CONV_WIDTH), _F32),
        pltpu.VMEM((TS, ATTN_WIDTH), _F32),
        pltpu.VMEM((TS, POOL_WIDTH), _BF16),
        pltpu.VMEM((TS, CONV_WIDTH), _BF16),
        pltpu.VMEM((N_KV_HEADS, TS + BLOCK, HEAD_DIM), _BF16),
        pltpu.VMEM((N_KV_HEADS, TS + BLOCK, 2 * LANES), _BF16),
        pltpu.VMEM((N_KV_HEADS, TS + BLOCK, 2 * LANES), _BF16),
        pltpu.VMEM((N_Q_HEADS, TS, HEAD_DIM), _BF16),
        pltpu.VMEM((TS, D_MIX), _BF16),
        pltpu.VMEM((2, N_KV_HEADS, Q_PER_KV * BLOCK, 2 * BLOCK), _F32),
        pltpu.VMEM((D_MODEL, D_IN), _BF16),
        pltpu.VMEM((D_MIX, D_MODEL), _BF16),
    ]
    return pl.pallas_call(
        functools.partial(_layer_body, layer=layer, final=final),
        out_shape=jax.ShapeDtypeStruct((batch, seq, D_MODEL), _F32),
        grid=(batch, n_tiles),
        in_specs=in_specs,
        out_specs=pl.BlockSpec((1, TS, D_MODEL), lambda b, j: (b, j, 0)),
        scratch_shapes=scratch,
        compiler_params=pltpu.CompilerParams(
            dimension_semantics=("arbitrary", "arbitrary"),
            vmem_limit_bytes=VMEM_LIMIT_BYTES,
        ),
        name="hybrid_layer_final" if final else "hybrid_layer",
    )


def kernel(x, ln_g, w_in, pool_w, pool_scale, conv_dw, conv_b, conv_ln_g, conv_ln_b, conv_pw, attn_sinks, w_out, final_g):
    batch, seq, d_model = x.shape
    assert d_model == D_MODEL and seq % TS == 0 and TS % BLOCK == 0
    bias = jnp.asarray(_bias_table())
    inv_cnt = jnp.asarray(_inv_count_table())
    poolw_bd = jnp.stack([jax.scipy.linalg.block_diag(*[pool_w[l, g] for g in range(POOL_GROUPS)])
                          for l in range(DEPTH)])
    params = (ln_g, w_in, poolw_bd, pool_scale, inv_cnt, conv_dw, conv_b, conv_ln_g, conv_ln_b, conv_pw,
              attn_sinks, bias, w_out, final_g.reshape(1, -1))
    for l in range(DEPTH):
        x = _layer_call(batch, seq, l, final=(l == DEPTH - 1))(x, *params)
    return x
```

```python
import functools

import numpy as np
import jax
import jax.numpy as jnp
from jax import lax
from jax.experimental import pallas as pl
from jax.experimental.pallas import tpu as pltpu

D_MODEL = 1024
DEPTH = 2
POOL_WIDTH = 256
POOL_GROUPS = 4
POOL_WINDOWS = (2, 4, 8, 16)
CONV_WIDTH = 256
CONV_KERNEL = 31
HEAD_DIM = 64
N_Q_HEADS = 8
N_KV_HEADS = 2
Q_PER_KV = N_Q_HEADS // N_KV_HEADS
ATTN_WIDTH = N_Q_HEADS * HEAD_DIM
KV_WIDTH = N_KV_HEADS * HEAD_DIM
WINDOW = 128
BLOCK = 128
D_MIX = POOL_WIDTH + CONV_WIDTH + ATTN_WIDTH
D_IN = 2 * POOL_WIDTH + 3 * CONV_WIDTH + 2 * ATTN_WIDTH + 2 * KV_WIDTH
EPS = 1e-6

C_UPOOL = 0
C_GPOOL = C_UPOOL + POOL_WIDTH
C_CA = C_GPOOL + POOL_WIDTH
C_CB = C_CA + CONV_WIDTH
C_GCONV = C_CB + CONV_WIDTH
C_Q = C_GCONV + CONV_WIDTH
C_K = C_Q + ATTN_WIDTH
C_V = C_K + KV_WIDTH
C_GATTN = C_V + KV_WIDTH

SUBLANES = 8
LANES = 128
BF16_ROWS = 16
MXU_COLS = 256
TS = 1024
NB = TS // BLOCK
OUT_BLOCKS = 8
GLU_ROW_GROUPS = (128, 128, 256, 512)
CHUNK = 64
GROUPS = CHUNK // SUBLANES
HALO_U = 16
HALO_H = 32
VMEM_LIMIT_BYTES = 63 * 1024 * 1024
LOG2E = 1.4426950408889634

_F32 = jnp.float32
_BF16 = jnp.bfloat16


def _silu(v):
    return v * jax.nn.sigmoid(v)


def _unrolled(n, body):
    for c in range(n):
        body(c)


def _shift_down(groups, d, row):
    rolled = [pltpu.roll(g, d, axis=0) for g in groups]
    take_prev = row < d
    out = [rolled[0]]
    for i in range(1, len(groups)):
        out.append(jnp.where(take_prev, rolled[i - 1], rolled[i]))
    return out


def _layer_body(x_ref, lng_ref, win_ref, poolw_ref, pscale_ref, invcnt_ref, dw_ref, cb_ref, clg_ref, clb_ref,
                pw_ref, sinks_ref, bias_ref, wout_ref, fg_ref, o_ref,
                hn_ref, ubuf, hbuf, gp_buf, gc_buf, ga_buf, zp_buf, zc_buf, kbuf, ve_buf, vo_buf, qbuf, mix_ref,
                bias_buf, win_bf, wout_bf, *, layer, final):
    j = pl.program_id(1)
    first_tile = j == 0

    @pl.when(jnp.logical_and(pl.program_id(0) == 0, first_tile))
    def _():
        for c in range(D_IN // MXU_COLS):
            cols = slice(c * MXU_COLS, (c + 1) * MXU_COLS)
            win_bf[:, cols] = win_ref[0, :, cols].astype(_BF16)
        for c in range(D_MODEL // MXU_COLS):
            cols = slice(c * MXU_COLS, (c + 1) * MXU_COLS)
            wout_bf[:, cols] = wout_ref[0, :, cols].astype(_BF16)

    row = lax.broadcasted_iota(jnp.int32, (SUBLANES, LANES), 0)
    lane = lax.broadcasted_iota(jnp.int32, (SUBLANES, LANES), 1)
    low_half = lane < HEAD_DIM

    @pl.when(first_tile)
    def _():
        sink_slot = lax.broadcasted_iota(jnp.int32, (BLOCK, 2 * BLOCK), 1) == 0
        for start in range(2):
            for h in range(N_Q_HEADS):
                kvh, g = divmod(h, Q_PER_KV)
                rows = pl.ds(g * BLOCK, BLOCK)
                bias_buf[start, kvh, rows, :] = jnp.where(
                    sink_slot, sinks_ref[layer, h] * LOG2E, bias_ref[start, kvh, rows, :])
        ubuf[0:HALO_U, :] = jnp.zeros((HALO_U, POOL_WIDTH), _F32)
        hbuf[0:HALO_H, :] = jnp.zeros((HALO_H, CONV_WIDTH), _F32)
        kbuf[:, 0:BLOCK, :] = jnp.zeros((N_KV_HEADS, BLOCK, HEAD_DIM), _BF16)
        zeros = jnp.zeros((TS + BLOCK, LANES), _BF16)
        lane_t = lax.broadcasted_iota(jnp.int32, (TS + BLOCK, LANES), 1)
        ones_lo = jnp.where(lane_t < HEAD_DIM, 1.0, 0.0).astype(_BF16)
        ones_hi = jnp.where(lane_t < HEAD_DIM, 0.0, 1.0).astype(_BF16)
        for kvh in range(N_KV_HEADS):
            ve_buf[kvh, :, 0:LANES] = zeros
            vo_buf[kvh, :, 0:LANES] = zeros
            ve_buf[kvh, :, LANES:2 * LANES] = ones_lo
            vo_buf[kvh, :, LANES:2 * LANES] = ones_hi

    @pl.when(jnp.logical_not(first_tile))
    def _():
        ubuf[0:HALO_U, :] = ubuf[TS:TS + HALO_U, :]
        hbuf[0:HALO_H, :] = hbuf[TS:TS + HALO_H, :]
        kbuf[:, 0:BLOCK, :] = kbuf[:, TS:TS + BLOCK, :]
        ve_buf[:, 0:BLOCK, 0:LANES] = ve_buf[:, TS:TS + BLOCK, 0:LANES]
        vo_buf[:, 0:BLOCK, 0:LANES] = vo_buf[:, TS:TS + BLOCK, 0:LANES]

    def norm_chunk(c):
        r = c * CHUNK
        xc = x_ref[0, pl.ds(r, CHUNK), :]
        ms = jnp.mean(xc * xc, axis=-1, keepdims=True)
        y = xc * lax.rsqrt(ms + EPS) * lng_ref[layer:layer + 1, :]
        hn_ref[pl.ds(r, CHUNK), :] = y.astype(_BF16)

    _unrolled(TS // CHUNK, norm_chunk)

    def proj(c0, width, r0=0, rows=TS):
        w = win_bf[:, c0:c0 + width]
        return jnp.dot(hn_ref[pl.ds(r0, rows), :], w, preferred_element_type=_F32)

    r0 = 0
    for n_rows in GLU_ROW_GROUPS:
        rows = pl.ds(HALO_H + r0, n_rows)
        hbuf[rows, :] = proj(C_CA, CONV_WIDTH, r0, n_rows)
        hbuf[rows, :] = hbuf[rows, :] * jax.nn.sigmoid(proj(C_CB, CONV_WIDTH, r0, n_rows))
        r0 += n_rows

    kv = proj(C_K, 2 * KV_WIDTH)
    keys = kv[:, 0:KV_WIDTH]
    vals = kv[:, KV_WIDTH:2 * KV_WIDTH]
    vals_swapped = pltpu.roll(vals, HEAD_DIM, axis=1)
    lo = lax.broadcasted_iota(jnp.int32, (TS, LANES), 1) < HEAD_DIM
    for kvh in range(N_KV_HEADS):
        kbuf[kvh, BLOCK:BLOCK + TS, :] = keys[:, kvh * HEAD_DIM:(kvh + 1) * HEAD_DIM].astype(_BF16)
        own_lo, own_hi = (vals, vals_swapped) if kvh == 0 else (vals_swapped, vals)
        ve_buf[kvh, BLOCK:BLOCK + TS, 0:LANES] = jnp.where(lo, own_lo, 0.0).astype(_BF16)
        vo_buf[kvh, BLOCK:BLOCK + TS, 0:LANES] = jnp.where(lo, 0.0, own_hi).astype(_BF16)

    heads_per_group = MXU_COLS // HEAD_DIM
    for half in range(ATTN_WIDTH // MXU_COLS):
        qh = proj(C_Q + half * MXU_COLS, MXU_COLS) * (HEAD_DIM ** -0.5 * LOG2E)
        for i in range(heads_per_group):
            qbuf[half * heads_per_group + i, :, :] = qh[:, i * HEAD_DIM:(i + 1) * HEAD_DIM].astype(_BF16)
    for half in range(ATTN_WIDTH // MXU_COLS):
        ga_buf[:, half * MXU_COLS:(half + 1) * MXU_COLS] = _silu(proj(C_GATTN + half * MXU_COLS, MXU_COLS))

    ubuf[HALO_U:HALO_U + TS, :] = proj(C_UPOOL, POOL_WIDTH)
    gp_buf[...] = _silu(proj(C_GPOOL, POOL_WIDTH))
    gc_buf[...] = _silu(proj(C_GCONV, CONV_WIDTH))

    inv_w = [jnp.where(low_half, 1.0 / POOL_WINDOWS[0], 1.0 / POOL_WINDOWS[1]).astype(_F32),
             jnp.where(low_half, 1.0 / POOL_WINDOWS[2], 1.0 / POOL_WINDOWS[3]).astype(_F32)]

    def pool_chunk(c):
        r = c * CHUNK
        n_in = GROUPS + HALO_U // SUBLANES
        for col in range(2):
            lanes = slice(col * LANES, (col + 1) * LANES)
            u = [ubuf[pl.ds(r + SUBLANES * i, SUBLANES), lanes] for i in range(n_in)]
            s2 = [a + b for a, b in zip(u, _shift_down(u, 1, row))]
            s4 = [a + b for a, b in zip(s2, _shift_down(s2, 2, row))]
            if col == 0:
                wide, narrow = s4, s2
            else:
                s8 = [a + b for a, b in zip(s4, _shift_down(s4, 4, row))]
                s16 = [s8[0]] + [s8[i] + s8[i - 1] for i in range(1, n_in)]
                wide, narrow = s16, s8
            pieces = []
            for g in range(GROUPS):
                i = g + HALO_U // SUBLANES
                inv = inv_w[col]
                if c == 0 and g < HALO_U // SUBLANES:
                    tab = invcnt_ref[SUBLANES * g:SUBLANES * (g + 1), lanes]
                    inv = jnp.where(first_tile, tab, inv)
                pooled = jnp.where(low_half, narrow[i], wide[i]) * inv
                pieces.append(pooled - u[i])
            zp_buf[pl.ds(r, CHUNK), lanes] = jnp.concatenate(pieces, axis=0).astype(_BF16)

    taps_by_b = [[] for _ in range(SUBLANES)]
    for k in range(CONV_KERNEL):
        a, b = divmod(k + HALO_H - (CONV_KERNEL - 1), SUBLANES)
        taps_by_b[b].append((a, k))
    n_h = GROUPS + HALO_H // SUBLANES

    def conv_chunk(c):
        r = c * CHUNK
        outs = []
        for col in range(2):
            lanes = slice(col * LANES, (col + 1) * LANES)
            h = [hbuf[pl.ds(r + SUBLANES * m, SUBLANES), lanes] for m in range(n_h)]
            acc = None
            for b in range(SUBLANES):
                n_p = GROUPS if b == 0 else GROUPS + 1
                part = []
                for m in range(n_p):
                    p = None
                    for a, k in taps_by_b[b]:
                        term = h[m + a] * dw_ref[0, k:k + 1, lanes]
                        p = term if p is None else p + term
                    part.append(p)
                if b == 0:
                    acc = part
                else:
                    rolled = [pltpu.roll(p, SUBLANES - b, axis=0) for p in part]
                    keep = row < (SUBLANES - b)
                    acc = [acc[g] + jnp.where(keep, rolled[g], rolled[g + 1]) for g in range(GROUPS)]
            outs.append(jnp.concatenate(acc, axis=0))
        conv = jnp.concatenate(outs, axis=1) + cb_ref[layer:layer + 1, :]
        mu = jnp.mean(conv, axis=-1, keepdims=True)
        cen = conv - mu
        var = jnp.mean(cen * cen, axis=-1, keepdims=True)
        z = cen * lax.rsqrt(var + EPS) * clg_ref[layer:layer + 1, :] + clb_ref[layer:layer + 1, :]
        zc_buf[pl.ds(r, CHUNK), :] = _silu(z).astype(_BF16)

    _unrolled(TS // CHUNK, conv_chunk)
    _unrolled(TS // CHUNK, pool_chunk)

    n_local = POOL_WIDTH + CONV_WIDTH

    top_rows = lax.broadcasted_iota(jnp.int32, (BF16_ROWS, 2 * LANES), 0)
    top_lanes = lax.broadcasted_iota(jnp.int32, (BF16_ROWS, 2 * LANES), 1)
    sink_value_slot = jnp.logical_and(top_rows == 0, top_lanes < LANES)
    sink_key_slot = lax.broadcasted_iota(jnp.int32, (BF16_ROWS, HEAD_DIM), 0) == 0

    def window_keys(kvh, r):
        top = kbuf[kvh, pl.ds(r, BF16_ROWS), :].astype(_F32)
        top = jnp.where(sink_key_slot, 0.0, top).astype(_BF16)
        return jnp.concatenate([top, kbuf[kvh, pl.ds(r + BF16_ROWS, 2 * BLOCK - BF16_ROWS), :]], axis=0)

    def scores(n):
        r = n * BLOCK
        seq_start = first_tile.astype(jnp.int32) if n == 0 else 0
        out = []
        for kvh in range(N_KV_HEADS):
            k2 = window_keys(kvh, r)
            q4 = jnp.concatenate([qbuf[kvh * Q_PER_KV + g, pl.ds(r, BLOCK), :] for g in range(Q_PER_KV)], axis=0)
            s = lax.dot_general(q4, k2, (((1,), (1,)), ((), ())), preferred_element_type=_F32)
            out.append(s + bias_buf[seq_start, kvh])
        return out

    def window_values(buf, kvh, r):
        top = buf[kvh, pl.ds(r, BF16_ROWS), :].astype(_F32)
        top = jnp.where(sink_value_slot, 0.0, top).astype(_BF16)
        return jnp.concatenate([top, buf[kvh, pl.ds(r + BF16_ROWS, 2 * BLOCK - BF16_ROWS), :]], axis=0)

    def softmax_pv(n, s_list):
        r = n * BLOCK
        for kvh in range(N_KV_HEADS):
            probs = []
            for g in range(Q_PER_KV):
                sg = s_list[kvh][g * BLOCK:(g + 1) * BLOCK]
                m = jnp.max(sg, axis=-1, keepdims=True)
                probs.append(jnp.exp2(sg - m).astype(_BF16))
            p_even = jnp.concatenate([probs[0], probs[2]], axis=0)
            p_odd = jnp.concatenate([probs[1], probs[3]], axis=0)
            o = (jnp.dot(p_even, window_values(ve_buf, kvh, r), preferred_element_type=_F32)
                 + jnp.dot(p_odd, window_values(vo_buf, kvh, r), preferred_element_type=_F32))
            res = o[:, 0:LANES] / o[:, LANES:2 * LANES]
            for pair in range(2):
                c0 = (kvh * 2 + pair) * LANES
                gate = ga_buf[pl.ds(r, BLOCK), c0:c0 + LANES]
                mix_ref[pl.ds(r, BLOCK), n_local + c0:n_local + c0 + LANES] = (
                    res[pair * BLOCK:(pair + 1) * BLOCK] * gate).astype(_BF16)

    def out_attn(r0, rows):
        for q in range(D_MODEL // MXU_COLS):
            cols = slice(q * MXU_COLS, (q + 1) * MXU_COLS)
            o_ref[0, pl.ds(r0, rows), cols] = x_ref[0, pl.ds(r0, rows), cols] + jnp.dot(
                mix_ref[pl.ds(r0, rows), n_local:D_MIX], wout_bf[n_local:D_MIX, cols],
                preferred_element_type=_F32)

    s_cur = scores(0)
    for n in range(NB):
        s_next = scores(n + 1) if n + 1 < NB else None
        if n >= OUT_BLOCKS and n % OUT_BLOCKS == 0:
            out_attn((n - OUT_BLOCKS) * BLOCK, OUT_BLOCKS * BLOCK)
        softmax_pv(n, s_cur)
        s_cur = s_next
    out_attn((NB - OUT_BLOCKS) * BLOCK, OUT_BLOCKS * BLOCK)

    y_pool = jnp.dot(zp_buf[...], poolw_ref[0].astype(_BF16), preferred_element_type=_F32)
    y_pool = y_pool * pscale_ref[layer:layer + 1, :]
    mix_ref[:, 0:POOL_WIDTH] = (y_pool * gp_buf[...]).astype(_BF16)
    y_conv = jnp.dot(zc_buf[...], pw_ref[0].astype(_BF16), preferred_element_type=_F32)
    mix_ref[:, POOL_WIDTH:POOL_WIDTH + CONV_WIDTH] = (y_conv * gc_buf[...]).astype(_BF16)

    def final_chunk(c):
        r = c * CHUNK
        xc = o_ref[0, pl.ds(r, CHUNK), :]
        ms = jnp.mean(xc * xc, axis=-1, keepdims=True)
        o_ref[0, pl.ds(r, CHUNK), :] = xc * lax.rsqrt(ms + EPS) * fg_ref[...]

    half_rows = TS // 2 if final else TS
    for r0 in range(0, TS, half_rows):
        rows = pl.ds(r0, half_rows)
        for q in range(D_MODEL // MXU_COLS):
            cols = slice(q * MXU_COLS, (q + 1) * MXU_COLS)
            o_ref[0, rows, cols] = o_ref[0, rows, cols] + jnp.dot(
                mix_ref[rows, 0:n_local], wout_bf[0:n_local, cols], preferred_element_type=_F32)
        if final:
            for c in range(r0 // CHUNK, (r0 + half_rows) // CHUNK):
                final_chunk(c)


def _bias_table():
    i = np.arange(BLOCK)[:, None]
    jj = np.arange(2 * BLOCK)[None, :]
    dist = BLOCK + i - jj
    in_band = (dist >= 0) & (dist < WINDOW)
    slopes = np.asarray([2.0 ** (-8.0 * (h + 1) / N_Q_HEADS) for h in range(N_Q_HEADS)], dtype=np.float32)
    tab = np.empty((2, N_KV_HEADS, Q_PER_KV * BLOCK, 2 * BLOCK), np.float32)
    for start in range(2):
        valid = in_band & ~((jj < BLOCK) & (start == 1))
        for h in range(N_Q_HEADS):
            bias = (-slopes[h] * LOG2E) * dist.astype(np.float64)
            kvh, g = divmod(h, Q_PER_KV)
            tab[start, kvh, g * BLOCK:(g + 1) * BLOCK] = np.where(valid, bias, -1e30).astype(np.float32)
    return tab


def _inv_count_table():
    t = np.arange(HALO_U, dtype=np.float32)[:, None]
    gw = POOL_WIDTH // POOL_GROUPS
    wnd = np.repeat(np.asarray(POOL_WINDOWS, np.float32), gw)[None, :]
    return (1.0 / np.minimum(t + 1.0, wnd)).astype(np.float32)


def _layer_call(batch, seq, layer, final):
    n_tiles = seq // TS
    const = lambda shape: pl.BlockSpec(shape, lambda b, j: (0,) * len(shape), pipeline_mode=pl.Buffered(1))
    per_layer = lambda shape: pl.BlockSpec((1,) + shape, lambda b, j: (layer,) + (0,) * len(shape),
                                           pipeline_mode=pl.Buffered(1))
    in_specs = [
        pl.BlockSpec((1, TS, D_MODEL), lambda b, j: (b, j, 0)),
        const((DEPTH, D_MODEL)),
        per_layer((D_MODEL, D_IN)),
        per_layer((POOL_WIDTH, POOL_WIDTH)),
        const((DEPTH, POOL_WIDTH)),
        const((HALO_U, POOL_WIDTH)),
        per_layer((CONV_KERNEL, CONV_WIDTH)),
        const((DEPTH, CONV_WIDTH)),
        const((DEPTH, CONV_WIDTH)),
        const((DEPTH, CONV_WIDTH)),
        per_layer((CONV_WIDTH, CONV_WIDTH)),
        pl.BlockSpec(memory_space=pltpu.SMEM),
        const((2, N_KV_HEADS, Q_PER_KV * BLOCK, 2 * BLOCK)),
        per_layer((D_MIX, D_MODEL)),
        const((1, D_MODEL)),
    ]
    scratch = [
        pltpu.VMEM((TS, D_MODEL), _BF16),
        pltpu.VMEM((TS + HALO_U, POOL_WIDTH), _F32),
        pltpu.VMEM((TS + HALO_H, CONV_WIDTH), _F32),
        pltpu.VMEM((TS, POOL_WIDTH), _F32),
        pltpu.VMEM((TS, CONV_WIDTH), _F32),
        pltpu.VMEM((TS, ATTN_WIDTH), _F32),
        pltpu.VMEM((TS, POOL_WIDTH), _BF16),
        pltpu.VMEM((TS, CONV_WIDTH), _BF16),
        pltpu.VMEM((N_KV_HEADS, TS + BLOCK, HEAD_DIM), _BF16),
        pltpu.VMEM((N_KV_HEADS, TS + BLOCK, 2 * LANES), _BF16),
        pltpu.VMEM((N_KV_HEADS, TS + BLOCK, 2 * LANES), _BF16),
        pltpu.VMEM((N_Q_HEADS, TS, HEAD_DIM), _BF16),
        pltpu.VMEM((TS, D_MIX), _BF16),
        pltpu.VMEM((2, N_KV_HEADS, Q_PER_KV * BLOCK, 2 * BLOCK), _F32),
        pltpu.VMEM((D_MODEL, D_IN), _BF16),
        pltpu.VMEM((D_MIX, D_MODEL), _BF16),
    ]
    return pl.pallas_call(
        functools.partial(_layer_body, layer=layer, final=final),
        out_shape=jax.ShapeDtypeStruct((batch, seq, D_MODEL), _F32),
        grid=(batch, n_tiles),
        in_specs=in_specs,
        out_specs=pl.BlockSpec((1, TS, D_MODEL), lambda b, j: (b, j, 0)),
        scratch_shapes=scratch,
        compiler_params=pltpu.CompilerParams(
            dimension_semantics=("arbitrary", "arbitrary"),
            vmem_limit_bytes=VMEM_LIMIT_BYTES,
        ),
        name="hybrid_layer_final" if final else "hybrid_layer",
    )


def kernel(x, ln_g, w_in, pool_w, pool_scale, conv_dw, conv_b, conv_ln_g, conv_ln_b, conv_pw, attn_sinks, w_out, final_g):
    batch, seq, d_model = x.shape
    assert d_model == D_MODEL and seq % TS == 0 and TS % BLOCK == 0
    bias = jnp.asarray(_bias_table())
    inv_cnt = jnp.asarray(_inv_count_table())
    poolw_bd = jnp.stack([jax.scipy.linalg.block_diag(*[pool_w[l, g] for g in range(POOL_GROUPS)])
                          for l in range(DEPTH)])
    params = (ln_g, w_in, poolw_bd, pool_scale, inv_cnt, conv_dw, conv_b, conv_ln_g, conv_ln_b, conv_pw,
              attn_sinks, bias, w_out, final_g.reshape(1, -1))
    for l in range(DEPTH):
        x = _layer_call(batch, seq, l, final=(l == DEPTH - 1))(x, *params)
    return x
```

```python
import functools

import numpy as np
import jax
import jax.numpy as jnp
from jax import lax
from jax.experimental import pallas as pl
from jax.experimental.pallas import tpu as pltpu

D_MODEL = 1024
DEPTH = 2
POOL_WIDTH = 256
POOL_GROUPS = 4
POOL_WINDOWS = (2, 4, 8, 16)
CONV_WIDTH = 256
CONV_KERNEL = 31
HEAD_DIM = 64
N_Q_HEADS = 8
N_KV_HEADS = 2
Q_PER_KV = N_Q_HEADS // N_KV_HEADS
ATTN_WIDTH = N_Q_HEADS * HEAD_DIM
KV_WIDTH = N_KV_HEADS * HEAD_DIM
WINDOW = 128
BLOCK = 128
D_MIX = POOL_WIDTH + CONV_WIDTH + ATTN_WIDTH
D_IN = 2 * POOL_WIDTH + 3 * CONV_WIDTH + 2 * ATTN_WIDTH + 2 * KV_WIDTH
EPS = 1e-6

C_UPOOL = 0
C_GPOOL = C_UPOOL + POOL_WIDTH
C_CA = C_GPOOL + POOL_WIDTH
C_CB = C_CA + CONV_WIDTH
C_GCONV = C_CB + CONV_WIDTH
C_Q = C_GCONV + CONV_WIDTH
C_K = C_Q + ATTN_WIDTH
C_V = C_K + KV_WIDTH
C_GATTN = C_V + KV_WIDTH

SUBLANES = 8
LANES = 128
BF16_ROWS = 16
MXU_COLS = 256
TS = 1024
NB = TS // BLOCK
OUT_BLOCKS = 8
GLU_ROW_GROUPS = (128, 128, 256, 512)
CHUNK = 64
GROUPS = CHUNK // SUBLANES
HALO_U = 16
HALO_H = 32
VMEM_LIMIT_BYTES = 63 * 1024 * 1024
LOG2E = 1.4426950408889634

_F32 = jnp.float32
_BF16 = jnp.bfloat16


def _silu(v):
    return v * jax.nn.sigmoid(v)


def _unrolled(n, body):
    for c in range(n):
        body(c)


def _shift_down(groups, d, row):
    rolled = [pltpu.roll(g, d, axis=0) for g in groups]
    take_prev = row < d
    out = [rolled[0]]
    for i in range(1, len(groups)):
        out.append(jnp.where(take_prev, rolled[i - 1], rolled[i]))
    return out


def _layer_body(x_ref, lng_ref, win_ref, poolw_ref, pscale_ref, invcnt_ref, dw_ref, cb_ref, clg_ref, clb_ref,
                pw_ref, sinks_ref, bias_ref, wout_ref, fg_ref, o_ref,
                hn_ref, ubuf, hbuf, gp_buf, gc_buf, ga_buf, zp_buf, zc_buf, kbuf, ve_buf, vo_buf, qbuf, mix_ref,
                bias_buf, win_bf, wout_bf, *, layer, final):
    j = pl.program_id(1)
    first_tile = j == 0

    @pl.when(jnp.logical_and(pl.program_id(0) == 0, first_tile))
    def _():
        for c in range(D_IN // MXU_COLS):
            cols = slice(c * MXU_COLS, (c + 1) * MXU_COLS)
            win_bf[:, cols] = win_ref[0, :, cols].astype(_BF16)
        for c in range(D_MODEL // MXU_COLS):
            cols = slice(c * MXU_COLS, (c + 1) * MXU_COLS)
            wout_bf[:, cols] = wout_ref[0, :, cols].astype(_BF16)

    row = lax.broadcasted_iota(jnp.int32, (SUBLANES, LANES), 0)
    lane = lax.broadcasted_iota(jnp.int32, (SUBLANES, LANES), 1)
    low_half = lane < HEAD_DIM

    @pl.when(first_tile)
    def _():
        sink_slot = lax.broadcasted_iota(jnp.int32, (BLOCK, 2 * BLOCK), 1) == 0
        for start in range(2):
            for h in range(N_Q_HEADS):
                kvh, g = divmod(h, Q_PER_KV)
                rows = pl.ds(g * BLOCK, BLOCK)
                bias_buf[start, kvh, rows, :] = jnp.where(
                    sink_slot, sinks_ref[layer, h] * LOG2E, bias_ref[start, kvh, rows, :])
        ubuf[0:HALO_U, :] = jnp.zeros((HALO_U, POOL_WIDTH), _F32)
        hbuf[0:HALO_H, :] = jnp.zeros((HALO_H, CONV_WIDTH), _F32)
        kbuf[:, 0:BLOCK, :] = jnp.zeros((N_KV_HEADS, BLOCK, HEAD_DIM), _BF16)
        zeros = jnp.zeros((TS + BLOCK, LANES), _BF16)
        lane_t = lax.broadcasted_iota(jnp.int32, (TS + BLOCK, LANES), 1)
        ones_lo = jnp.where(lane_t < HEAD_DIM, 1.0, 0.0).astype(_BF16)
        ones_hi = jnp.where(lane_t < HEAD_DIM, 0.0, 1.0).astype(_BF16)
        for kvh in range(N_KV_HEADS):
            ve_buf[kvh, :, 0:LANES] = zeros
            vo_buf[kvh, :, 0:LANES] = zeros
            ve_buf[kvh, :, LANES:2 * LANES] = ones_lo
            vo_buf[kvh, :, LANES:2 * LANES] = ones_hi

    @pl.when(jnp.logical_not(first_tile))
    def _():
        ubuf[0:HALO_U, :] = ubuf[TS:TS + HALO_U, :]
        hbuf[0:HALO_H, :] = hbuf[TS:TS + HALO_H, :]
        kbuf[:, 0:BLOCK, :] = kbuf[:, TS:TS + BLOCK, :]
        ve_buf[:, 0:BLOCK, 0:LANES] = ve_buf[:, TS:TS + BLOCK, 0:LANES]
        vo_buf[:, 0:BLOCK, 0:LANES] = vo_buf[:, TS:TS + BLOCK, 0:LANES]

    def norm_chunk(c):
        r = c * CHUNK
        xc = x_ref[0, pl.ds(r, CHUNK), :]
        ms = jnp.mean(xc * xc, axis=-1, keepdims=True)
        y = xc * lax.rsqrt(ms + EPS) * lng_ref[layer:layer + 1, :]
        hn_ref[pl.ds(r, CHUNK), :] = y.astype(_BF16)

    _unrolled(TS // CHUNK, norm_chunk)

    def proj(c0, width, r0=0, rows=TS):
        w = win_bf[:, c0:c0 + width]
        return jnp.dot(hn_ref[pl.ds(r0, rows), :], w, preferred_element_type=_F32)

    r0 = 0
    for n_rows in GLU_ROW_GROUPS:
        rows = pl.ds(HALO_H + r0, n_rows)
        hbuf[rows, :] = proj(C_CA, CONV_WIDTH, r0, n_rows)
        hbuf[rows, :] = hbuf[rows, :] * jax.nn.sigmoid(proj(C_CB, CONV_WIDTH, r0, n_rows))
        r0 += n_rows

    kv = proj(C_K, 2 * KV_WIDTH)
    keys = kv[:, 0:KV_WIDTH]
    vals = kv[:, KV_WIDTH:2 * KV_WIDTH]
    vals_swapped = pltpu.roll(vals, HEAD_DIM, axis=1)
    lo = lax.broadcasted_iota(jnp.int32, (TS, LANES), 1) < HEAD_DIM
    for kvh in range(N_KV_HEADS):
        kbuf[kvh, BLOCK:BLOCK + TS, :] = keys[:, kvh * HEAD_DIM:(kvh + 1) * HEAD_DIM].astype(_BF16)
        own_lo, own_hi = (vals, vals_swapped) if kvh == 0 else (vals_swapped, vals)
        ve_buf[kvh, BLOCK:BLOCK + TS, 0:LANES] = jnp.where(lo, own_lo, 0.0).astype(_BF16)
        vo_buf[kvh, BLOCK:BLOCK + TS, 0:LANES] = jnp.where(lo, 0.0, own_hi).astype(_BF16)

    heads_per_group = MXU_COLS // HEAD_DIM
    for half in range(ATTN_WIDTH // MXU_COLS):
        qh = proj(C_Q + half * MXU_COLS, MXU_COLS) * (HEAD_DIM ** -0.5 * LOG2E)
        for i in range(heads_per_group):
            qbuf[half * heads_per_group + i, :, :] = qh[:, i * HEAD_DIM:(i + 1) * HEAD_DIM].astype(_BF16)
    for half in range(ATTN_WIDTH // MXU_COLS):
        ga_buf[:, half * MXU_COLS:(half + 1) * MXU_COLS] = _silu(proj(C_GATTN + half * MXU_COLS, MXU_COLS))

    ubuf[HALO_U:HALO_U + TS, :] = proj(C_UPOOL, POOL_WIDTH)
    gp_buf[...] = _silu(proj(C_GPOOL, POOL_WIDTH))
    gc_buf[...] = _silu(proj(C_GCONV, CONV_WIDTH))

    inv_w = [jnp.where(low_half, 1.0 / POOL_WINDOWS[0], 1.0 / POOL_WINDOWS[1]).astype(_F32),
             jnp.where(low_half, 1.0 / POOL_WINDOWS[2], 1.0 / POOL_WINDOWS[3]).astype(_F32)]

    def pool_chunk(c):
        r = c * CHUNK
        n_in = GROUPS + HALO_U // SUBLANES
        for col in range(2):
            lanes = slice(col * LANES, (col + 1) * LANES)
            u = [ubuf[pl.ds(r + SUBLANES * i, SUBLANES), lanes] for i in range(n_in)]
            s2 = [a + b for a, b in zip(u, _shift_down(u, 1, row))]
            s4 = [a + b for a, b in zip(s2, _shift_down(s2, 2, row))]
            if col == 0:
                wide, narrow = s4, s2
            else:
                s8 = [a + b for a, b in zip(s4, _shift_down(s4, 4, row))]
                s16 = [s8[0]] + [s8[i] + s8[i - 1] for i in range(1, n_in)]
                wide, narrow = s16, s8
            pieces = []
            for g in range(GROUPS):
                i = g + HALO_U // SUBLANES
                inv = inv_w[col]
                if c == 0 and g < HALO_U // SUBLANES:
                    tab = invcnt_ref[SUBLANES * g:SUBLANES * (g + 1), lanes]
                    inv = jnp.where(first_tile, tab, inv)
                pooled = jnp.where(low_half, narrow[i], wide[i]) * inv
                pieces.append(pooled - u[i])
            zp_buf[pl.ds(r, CHUNK), lanes] = jnp.concatenate(pieces, axis=0).astype(_BF16)

    taps_by_b = [[] for _ in range(SUBLANES)]
    for k in range(CONV_KERNEL):
        a, b = divmod(k + HALO_H - (CONV_KERNEL - 1), SUBLANES)
        taps_by_b[b].append((a, k))
    n_h = GROUPS + HALO_H // SUBLANES

    def conv_chunk(c):
        r = c * CHUNK
        outs = []
        for col in range(2):
            lanes = slice(col * LANES, (col + 1) * LANES)
            h = [hbuf[pl.ds(r + SUBLANES * m, SUBLANES), lanes] for m in range(n_h)]
            acc = None
            for b in range(SUBLANES):
                n_p = GROUPS if b == 0 else GROUPS + 1
                part = []
                for m in range(n_p):
                    p = None
                    for a, k in taps_by_b[b]:
                        term = h[m + a] * dw_ref[0, k:k + 1, lanes]
                        p = term if p is None else p + term
                    part.append(p)
                if b == 0:
                    acc = part
                else:
                    rolled = [pltpu.roll(p, SUBLANES - b, axis=0) for p in part]
                    keep = row < (SUBLANES - b)
                    acc = [acc[g] + jnp.where(keep, rolled[g], rolled[g + 1]) for g in range(GROUPS)]
            outs.append(jnp.concatenate(acc, axis=0))
        conv = jnp.concatenate(outs, axis=1) + cb_ref[layer:layer + 1, :]
        mu = jnp.mean(conv, axis=-1, keepdims=True)
        cen = conv - mu
        var = jnp.mean(cen * cen, axis=-1, keepdims=True)
        z = cen * lax.rsqrt(var + EPS) * clg_ref[layer:layer + 1, :] + clb_ref[layer:layer + 1, :]
        zc_buf[pl.ds(r, CHUNK), :] = _silu(z).astype(_BF16)

    _unrolled(TS // CHUNK, conv_chunk)
    _unrolled(TS // CHUNK, pool_chunk)

    n_local = POOL_WIDTH + CONV_WIDTH

    top_rows = lax.broadcasted_iota(jnp.int32, (BF16_ROWS, 2 * LANES), 0)
    top_lanes = lax.broadcasted_iota(jnp.int32, (BF16_ROWS, 2 * LANES), 1)
    sink_value_slot = jnp.logical_and(top_rows == 0, top_lanes < LANES)
    sink_key_slot = lax.broadcasted_iota(jnp.int32, (BF16_ROWS, HEAD_DIM), 0) == 0

    def window_keys(kvh, r):
        top = kbuf[kvh, pl.ds(r, BF16_ROWS), :].astype(_F32)
        top = jnp.where(sink_key_slot, 0.0, top).astype(_BF16)
        return jnp.concatenate([top, kbuf[kvh, pl.ds(r + BF16_ROWS, 2 * BLOCK - BF16_ROWS), :]], axis=0)

    def scores(n):
        r = n * BLOCK
        seq_start = first_tile.astype(jnp.int32) if n == 0 else 0
        out = []
        for kvh in range(N_KV_HEADS):
            k2 = window_keys(kvh, r)
            per_head = []
            for g in range(Q_PER_KV):
                s = lax.dot_general(qbuf[kvh * Q_PER_KV + g, pl.ds(r, BLOCK), :], k2, (((1,), (1,)), ((), ())),
                                    preferred_element_type=_F32)
                per_head.append(s + bias_buf[seq_start, kvh, g * BLOCK:(g + 1) * BLOCK, :])
            out.append(per_head)
        return out

    def window_values(buf, kvh, r):
        top = buf[kvh, pl.ds(r, BF16_ROWS), :].astype(_F32)
        top = jnp.where(sink_value_slot, 0.0, top).astype(_BF16)
        return jnp.concatenate([top, buf[kvh, pl.ds(r + BF16_ROWS, 2 * BLOCK - BF16_ROWS), :]], axis=0)

    def softmax_pv(n, s_list):
        r = n * BLOCK
        for kvh in range(N_KV_HEADS):
            probs = []
            for g in range(Q_PER_KV):
                sg = s_list[kvh][g]
                m = jnp.max(sg, axis=-1, keepdims=True)
                probs.append(jnp.exp2(sg - m).astype(_BF16))
            p_even = jnp.concatenate([probs[0], probs[2]], axis=0)
            p_odd = jnp.concatenate([probs[1], probs[3]], axis=0)
            o = (jnp.dot(p_even, window_values(ve_buf, kvh, r), preferred_element_type=_F32)
                 + jnp.dot(p_odd, window_values(vo_buf, kvh, r), preferred_element_type=_F32))
            res = o[:, 0:LANES] / o[:, LANES:2 * LANES]
            for pair in range(2):
                c0 = (kvh * 2 + pair) * LANES
                gate = ga_buf[pl.ds(r, BLOCK), c0:c0 + LANES]
                mix_ref[pl.ds(r, BLOCK), n_local + c0:n_local + c0 + LANES] = (
                    res[pair * BLOCK:(pair + 1) * BLOCK] * gate).astype(_BF16)

    def out_attn(r0, rows):
        for q in range(D_MODEL // MXU_COLS):
            cols = slice(q * MXU_COLS, (q + 1) * MXU_COLS)
            o_ref[0, pl.ds(r0, rows), cols] = x_ref[0, pl.ds(r0, rows), cols] + jnp.dot(
                mix_ref[pl.ds(r0, rows), n_local:D_MIX], wout_bf[n_local:D_MIX, cols],
                preferred_element_type=_F32)

    for n in range(NB):
        if n >= OUT_BLOCKS and n % OUT_BLOCKS == 0:
            out_attn((n - OUT_BLOCKS) * BLOCK, OUT_BLOCKS * BLOCK)
        softmax_pv(n, scores(n))
    out_attn((NB - OUT_BLOCKS) * BLOCK, OUT_BLOCKS * BLOCK)

    y_pool = jnp.dot(zp_buf[...], poolw_ref[0].astype(_BF16), preferred_element_type=_F32)
    y_pool = y_pool * pscale_ref[layer:layer + 1, :]
    mix_ref[:, 0:POOL_WIDTH] = (y_pool * gp_buf[...]).astype(_BF16)
    y_conv = jnp.dot(zc_buf[...], pw_ref[0].astype(_BF16), preferred_element_type=_F32)
    mix_ref[:, POOL_WIDTH:POOL_WIDTH + CONV_WIDTH] = (y_conv * gc_buf[...]).astype(_BF16)

    def final_chunk(c):
        r = c * CHUNK
        xc = o_ref[0, pl.ds(r, CHUNK), :]
        ms = jnp.mean(xc * xc, axis=-1, keepdims=True)
        o_ref[0, pl.ds(r, CHUNK), :] = xc * lax.rsqrt(ms + EPS) * fg_ref[...]

    half_rows = TS
    for r0 in range(0, TS, half_rows):
        rows = pl.ds(r0, half_rows)
        for q in range(D_MODEL // MXU_COLS):
            cols = slice(q * MXU_COLS, (q + 1) * MXU_COLS)
            o_ref[0, rows, cols] = o_ref[0, rows, cols] + jnp.dot(
                mix_ref[rows, 0:n_local], wout_bf[0:n_local, cols], preferred_element_type=_F32)
        if final:
            for c in range(r0 // CHUNK, (r0 + half_rows) // CHUNK):
                final_chunk(c)


def _bias_table():
    i = np.arange(BLOCK)[:, None]
    jj = np.arange(2 * BLOCK)[None, :]
    dist = BLOCK + i - jj
    in_band = (dist >= 0) & (dist < WINDOW)
    slopes = np.asarray([2.0 ** (-8.0 * (h + 1) / N_Q_HEADS) for h in range(N_Q_HEADS)], dtype=np.float32)
    tab = np.empty((2, N_KV_HEADS, Q_PER_KV * BLOCK, 2 * BLOCK), np.float32)
    for start in range(2):
        valid = in_band & ~((jj < BLOCK) & (start == 1))
        for h in range(N_Q_HEADS):
            bias = (-slopes[h] * LOG2E) * dist.astype(np.float64)
            kvh, g = divmod(h, Q_PER_KV)
            tab[start, kvh, g * BLOCK:(g + 1) * BLOCK] = np.where(valid, bias, -1e30).astype(np.float32)
    return tab


def _inv_count_table():
    t = np.arange(HALO_U, dtype=np.float32)[:, None]
    gw = POOL_WIDTH // POOL_GROUPS
    wnd = np.repeat(np.asarray(POOL_WINDOWS, np.float32), gw)[None, :]
    return (1.0 / np.minimum(t + 1.0, wnd)).astype(np.float32)


def _layer_call(batch, seq, layer, final):
    n_tiles = seq // TS
    const = lambda shape: pl.BlockSpec(shape, lambda b, j: (0,) * len(shape), pipeline_mode=pl.Buffered(1))
    per_layer = lambda shape: pl.BlockSpec((1,) + shape, lambda b, j: (layer,) + (0,) * len(shape),
                                           pipeline_mode=pl.Buffered(1))
    in_specs = [
        pl.BlockSpec((1, TS, D_MODEL), lambda b, j: (b, j, 0)),
        const((DEPTH, D_MODEL)),
        per_layer((D_MODEL, D_IN)),
        per_layer((POOL_WIDTH, POOL_WIDTH)),
        const((DEPTH, POOL_WIDTH)),
        const((HALO_U, POOL_WIDTH)),
        per_layer((CONV_KERNEL, CONV_WIDTH)),
        const((DEPTH, CONV_WIDTH)),
        const((DEPTH, CONV_WIDTH)),
        const((DEPTH, CONV_WIDTH)),
        per_layer((CONV_WIDTH, CONV_WIDTH)),
        pl.BlockSpec(memory_space=pltpu.SMEM),
        const((2, N_KV_HEADS, Q_PER_KV * BLOCK, 2 * BLOCK)),
        per_layer((D_MIX, D_MODEL)),
        const((1, D_MODEL)),
    ]
    scratch = [
        pltpu.VMEM((TS, D_MODEL), _BF16),
        pltpu.VMEM((TS + HALO_U, POOL_WIDTH), _F32),
        pltpu.VMEM((TS + HALO_H, CONV_WIDTH), _F32),
        pltpu.VMEM((TS, POOL_WIDTH), _F32),
        pltpu.VMEM((TS, CONV_WIDTH), _F32),
        pltpu.VMEM((TS, ATTN_WIDTH), _F32),
        pltpu.VMEM((TS, POOL_WIDTH), _BF16),
        pltpu.VMEM((TS, CONV_WIDTH), _BF16),
        pltpu.VMEM((N_KV_HEADS, TS + BLOCK, HEAD_DIM), _BF16),
        pltpu.VMEM((N_KV_HEADS, TS + BLOCK, 2 * LANES), _BF16),
        pltpu.VMEM((N_KV_HEADS, TS + BLOCK, 2 * LANES), _BF16),
        pltpu.VMEM((N_Q_HEADS, TS, HEAD_DIM), _BF16),
        pltpu.VMEM((TS, D_MIX), _BF16),
        pltpu.VMEM((2, N_KV_HEADS, Q_PER_KV * BLOCK, 2 * BLOCK), _F32),
        pltpu.VMEM((D_MODEL, D_IN), _BF16),
        pltpu.VMEM((D_MIX, D_MODEL), _BF16),
    ]
    return pl.pallas_call(
        functools.partial(_layer_body, layer=layer, final=final),
        out_shape=jax.ShapeDtypeStruct((batch, seq, D_MODEL), _F32),
        grid=(batch, n_tiles),
        in_specs=in_specs,
        out_specs=pl.BlockSpec((1, TS, D_MODEL), lambda b, j: (b, j, 0)),
        scratch_shapes=scratch,
        compiler_params=pltpu.CompilerParams(
            dimension_semantics=("arbitrary", "arbitrary"),
            vmem_limit_bytes=VMEM_LIMIT_BYTES,
        ),
        name="hybrid_layer_final" if final else "hybrid_layer",
    )


def kernel(x, ln_g, w_in, pool_w, pool_scale, conv_dw, conv_b, conv_ln_g, conv_ln_b, conv_pw, attn_sinks, w_out, final_g):
    batch, seq, d_model = x.shape
    assert d_model == D_MODEL and seq % TS == 0 and TS % BLOCK == 0
    bias = jnp.asarray(_bias_table())
    inv_cnt = jnp.asarray(_inv_count_table())
    poolw_bd = jnp.stack([jax.scipy.linalg.block_diag(*[pool_w[l, g] for g in range(POOL_GROUPS)])
                          for l in range(DEPTH)])
    params = (ln_g, w_in, poolw_bd, pool_scale, inv_cnt, conv_dw, conv_b, conv_ln_g, conv_ln_b, conv_pw,
              attn_sinks, bias, w_out, final_g.reshape(1, -1))
    for l in range(DEPTH):
        x = _layer_call(batch, seq, l, final=(l == DEPTH - 1))(x, *params)
    return x
```

```python
import functools

import numpy as np
import jax
import jax.numpy as jnp
from jax import lax
from jax.experimental import pallas as pl
from jax.experimental.pallas import tpu as pltpu

D_MODEL = 1024
DEPTH = 2
POOL_WIDTH = 256
POOL_GROUPS = 4
POOL_WINDOWS = (2, 4, 8, 16)
CONV_WIDTH = 256
CONV_KERNEL = 31
HEAD_DIM = 64
N_Q_HEADS = 8
N_KV_HEADS = 2
Q_PER_KV = N_Q_HEADS // N_KV_HEADS
ATTN_WIDTH = N_Q_HEADS * HEAD_DIM
KV_WIDTH = N_KV_HEADS * HEAD_DIM
WINDOW = 128
BLOCK = 128
D_MIX = POOL_WIDTH + CONV_WIDTH + ATTN_WIDTH
D_IN = 2 * POOL_WIDTH + 3 * CONV_WIDTH + 2 * ATTN_WIDTH + 2 * KV_WIDTH
EPS = 1e-6

C_UPOOL = 0
C_GPOOL = C_UPOOL + POOL_WIDTH
C_CA = C_GPOOL + POOL_WIDTH
C_CB = C_CA + CONV_WIDTH
C_GCONV = C_CB + CONV_WIDTH
C_Q = C_GCONV + CONV_WIDTH
C_K = C_Q + ATTN_WIDTH
C_V = C_K + KV_WIDTH
C_GATTN = C_V + KV_WIDTH

SUBLANES = 8
LANES = 128
BF16_ROWS = 16
MXU_COLS = 256
TS = 1024
NB = TS // BLOCK
OUT_BLOCKS = 8
GLU_ROW_GROUPS = (128, 128, 256, 512)
SCORE_LOOKAHEAD = 2
CHUNK = 64
GROUPS = CHUNK // SUBLANES
HALO_U = 16
HALO_H = 32
VMEM_LIMIT_BYTES = 63 * 1024 * 1024
LOG2E = 1.4426950408889634

_F32 = jnp.float32
_BF16 = jnp.bfloat16


def _silu(v):
    return v * jax.nn.sigmoid(v)


def _unrolled(n, body):
    for c in range(n):
        body(c)


def _shift_down(groups, d, row):
    rolled = [pltpu.roll(g, d, axis=0) for g in groups]
    take_prev = row < d
    out = [rolled[0]]
    for i in range(1, len(groups)):
        out.append(jnp.where(take_prev, rolled[i - 1], rolled[i]))
    return out


def _layer_body(x_ref, lng_ref, win_ref, poolw_ref, pscale_ref, invcnt_ref, dw_ref, cb_ref, clg_ref, clb_ref,
                pw_ref, sinks_ref, bias_ref, wout_ref, fg_ref, o_ref,
                hn_ref, ubuf, hbuf, gp_buf, gc_buf, ga_buf, zp_buf, zc_buf, kbuf, ve_buf, vo_buf, qbuf, mix_ref,
                bias_buf, win_bf, wout_bf, *, layer, final):
    j = pl.program_id(1)
    first_tile = j == 0

    @pl.when(jnp.logical_and(pl.program_id(0) == 0, first_tile))
    def _():
        for c in range(D_IN // MXU_COLS):
            cols = slice(c * MXU_COLS, (c + 1) * MXU_COLS)
            win_bf[:, cols] = win_ref[0, :, cols].astype(_BF16)
        for c in range(D_MODEL // MXU_COLS):
            cols = slice(c * MXU_COLS, (c + 1) * MXU_COLS)
            wout_bf[:, cols] = wout_ref[0, :, cols].astype(_BF16)

    row = lax.broadcasted_iota(jnp.int32, (SUBLANES, LANES), 0)
    lane = lax.broadcasted_iota(jnp.int32, (SUBLANES, LANES), 1)
    low_half = lane < HEAD_DIM

    @pl.when(first_tile)
    def _():
        sink_slot = lax.broadcasted_iota(jnp.int32, (BLOCK, 2 * BLOCK), 1) == 0
        for start in range(2):
            for h in range(N_Q_HEADS):
                kvh, g = divmod(h, Q_PER_KV)
                rows = pl.ds(g * BLOCK, BLOCK)
                bias_buf[start, kvh, rows, :] = jnp.where(
                    sink_slot, sinks_ref[layer, h] * LOG2E, bias_ref[start, kvh, rows, :])
        ubuf[0:HALO_U, :] = jnp.zeros((HALO_U, POOL_WIDTH), _F32)
        hbuf[0:HALO_H, :] = jnp.zeros((HALO_H, CONV_WIDTH), _F32)
        kbuf[:, 0:BLOCK, :] = jnp.zeros((N_KV_HEADS, BLOCK, HEAD_DIM), _BF16)
        zeros = jnp.zeros((TS + BLOCK, LANES), _BF16)
        lane_t = lax.broadcasted_iota(jnp.int32, (TS + BLOCK, LANES), 1)
        ones_lo = jnp.where(lane_t < HEAD_DIM, 1.0, 0.0).astype(_BF16)
        ones_hi = jnp.where(lane_t < HEAD_DIM, 0.0, 1.0).astype(_BF16)
        for kvh in range(N_KV_HEADS):
            ve_buf[kvh, :, 0:LANES] = zeros
            vo_buf[kvh, :, 0:LANES] = zeros
            ve_buf[kvh, :, LANES:2 * LANES] = ones_lo
            vo_buf[kvh, :, LANES:2 * LANES] = ones_hi

    @pl.when(jnp.logical_not(first_tile))
    def _():
        ubuf[0:HALO_U, :] = ubuf[TS:TS + HALO_U, :]
        hbuf[0:HALO_H, :] = hbuf[TS:TS + HALO_H, :]
        kbuf[:, 0:BLOCK, :] = kbuf[:, TS:TS + BLOCK, :]
        ve_buf[:, 0:BLOCK, 0:LANES] = ve_buf[:, TS:TS + BLOCK, 0:LANES]
        vo_buf[:, 0:BLOCK, 0:LANES] = vo_buf[:, TS:TS + BLOCK, 0:LANES]

    def norm_chunk(c):
        r = c * CHUNK
        xc = x_ref[0, pl.ds(r, CHUNK), :]
        ms = jnp.mean(xc * xc, axis=-1, keepdims=True)
        y = xc * lax.rsqrt(ms + EPS) * lng_ref[layer:layer + 1, :]
        hn_ref[pl.ds(r, CHUNK), :] = y.astype(_BF16)

    _unrolled(TS // CHUNK, norm_chunk)

    def proj(c0, width, r0=0, rows=TS):
        w = win_bf[:, c0:c0 + width]
        return jnp.dot(hn_ref[pl.ds(r0, rows), :], w, preferred_element_type=_F32)

    r0 = 0
    for n_rows in GLU_ROW_GROUPS:
        rows = pl.ds(HALO_H + r0, n_rows)
        hbuf[rows, :] = proj(C_CA, CONV_WIDTH, r0, n_rows)
        hbuf[rows, :] = hbuf[rows, :] * jax.nn.sigmoid(proj(C_CB, CONV_WIDTH, r0, n_rows))
        r0 += n_rows

    kv = proj(C_K, 2 * KV_WIDTH)
    keys = kv[:, 0:KV_WIDTH]
    vals = kv[:, KV_WIDTH:2 * KV_WIDTH]
    vals_swapped = pltpu.roll(vals, HEAD_DIM, axis=1)
    lo = lax.broadcasted_iota(jnp.int32, (TS, LANES), 1) < HEAD_DIM
    for kvh in range(N_KV_HEADS):
        kbuf[kvh, BLOCK:BLOCK + TS, :] = keys[:, kvh * HEAD_DIM:(kvh + 1) * HEAD_DIM].astype(_BF16)
        own_lo, own_hi = (vals, vals_swapped) if kvh == 0 else (vals_swapped, vals)
        ve_buf[kvh, BLOCK:BLOCK + TS, 0:LANES] = jnp.where(lo, own_lo, 0.0).astype(_BF16)
        vo_buf[kvh, BLOCK:BLOCK + TS, 0:LANES] = jnp.where(lo, 0.0, own_hi).astype(_BF16)

    heads_per_group = MXU_COLS // HEAD_DIM
    for half in range(ATTN_WIDTH // MXU_COLS):
        qh = proj(C_Q + half * MXU_COLS, MXU_COLS) * (HEAD_DIM ** -0.5 * LOG2E)
        for i in range(heads_per_group):
            qbuf[half * heads_per_group + i, :, :] = qh[:, i * HEAD_DIM:(i + 1) * HEAD_DIM].astype(_BF16)
    for half in range(ATTN_WIDTH // MXU_COLS):
        ga_buf[:, half * MXU_COLS:(half + 1) * MXU_COLS] = _silu(proj(C_GATTN + half * MXU_COLS, MXU_COLS))

    ubuf[HALO_U:HALO_U + TS, :] = proj(C_UPOOL, POOL_WIDTH)
    gp_buf[...] = _silu(proj(C_GPOOL, POOL_WIDTH))
    gc_buf[...] = _silu(proj(C_GCONV, CONV_WIDTH))

    inv_w = [jnp.where(low_half, 1.0 / POOL_WINDOWS[0], 1.0 / POOL_WINDOWS[1]).astype(_F32),
             jnp.where(low_half, 1.0 / POOL_WINDOWS[2], 1.0 / POOL_WINDOWS[3]).astype(_F32)]

    def pool_chunk(c):
        r = c * CHUNK
        n_in = GROUPS + HALO_U // SUBLANES
        for col in range(2):
            lanes = slice(col * LANES, (col + 1) * LANES)
            u = [ubuf[pl.ds(r + SUBLANES * i, SUBLANES), lanes] for i in range(n_in)]
            s2 = [a + b for a, b in zip(u, _shift_down(u, 1, row))]
            s4 = [a + b for a, b in zip(s2, _shift_down(s2, 2, row))]
            if col == 0:
                wide, narrow = s4, s2
            else:
                s8 = [a + b for a, b in zip(s4, _shift_down(s4, 4, row))]
                s16 = [s8[0]] + [s8[i] + s8[i - 1] for i in range(1, n_in)]
                wide, narrow = s16, s8
            pieces = []
            for g in range(GROUPS):
                i = g + HALO_U // SUBLANES
                inv = inv_w[col]
                if c == 0 and g < HALO_U // SUBLANES:
                    tab = invcnt_ref[SUBLANES * g:SUBLANES * (g + 1), lanes]
                    inv = jnp.where(first_tile, tab, inv)
                pooled = jnp.where(low_half, narrow[i], wide[i]) * inv
                pieces.append(pooled - u[i])
            zp_buf[pl.ds(r, CHUNK), lanes] = jnp.concatenate(pieces, axis=0).astype(_BF16)

    taps_by_b = [[] for _ in range(SUBLANES)]
    for k in range(CONV_KERNEL):
        a, b = divmod(k + HALO_H - (CONV_KERNEL - 1), SUBLANES)
        taps_by_b[b].append((a, k))
    n_h = GROUPS + HALO_H // SUBLANES

    def conv_chunk(c):
        r = c * CHUNK
        outs = []
        for col in range(2):
            lanes = slice(col * LANES, (col + 1) * LANES)
            h = [hbuf[pl.ds(r + SUBLANES * m, SUBLANES), lanes] for m in range(n_h)]
            acc = None
            for b in range(SUBLANES):
                n_p = GROUPS if b == 0 else GROUPS + 1
                part = []
                for m in range(n_p):
                    p = None
                    for a, k in taps_by_b[b]:
                        term = h[m + a] * dw_ref[0, k:k + 1, lanes]
                        p = term if p is None else p + term
                    part.append(p)
                if b == 0:
                    acc = part
                else:
                    rolled = [pltpu.roll(p, SUBLANES - b, axis=0) for p in part]
                    keep = row < (SUBLANES - b)
                    acc = [acc[g] + jnp.where(keep, rolled[g], rolled[g + 1]) for g in range(GROUPS)]
            outs.append(jnp.concatenate(acc, axis=0))
        conv = jnp.concatenate(outs, axis=1) + cb_ref[layer:layer + 1, :]
        mu = jnp.mean(conv, axis=-1, keepdims=True)
        cen = conv - mu
        var = jnp.mean(cen * cen, axis=-1, keepdims=True)
        z = cen * lax.rsqrt(var + EPS) * clg_ref[layer:layer + 1, :] + clb_ref[layer:layer + 1, :]
        zc_buf[pl.ds(r, CHUNK), :] = _silu(z).astype(_BF16)

    _unrolled(TS // CHUNK, conv_chunk)
    _unrolled(TS // CHUNK, pool_chunk)

    n_local = POOL_WIDTH + CONV_WIDTH

    top_rows = lax.broadcasted_iota(jnp.int32, (BF16_ROWS, 2 * LANES), 0)
    top_lanes = lax.broadcasted_iota(jnp.int32, (BF16_ROWS, 2 * LANES), 1)
    sink_value_slot = jnp.logical_and(top_rows == 0, top_lanes < LANES)
    sink_key_slot = lax.broadcasted_iota(jnp.int32, (BF16_ROWS, HEAD_DIM), 0) == 0

    def window_keys(kvh, r):
        top = kbuf[kvh, pl.ds(r, BF16_ROWS), :].astype(_F32)
        top = jnp.where(sink_key_slot, 0.0, top).astype(_BF16)
        return jnp.concatenate([top, kbuf[kvh, pl.ds(r + BF16_ROWS, 2 * BLOCK - BF16_ROWS), :]], axis=0)

    def scores(n):
        r = n * BLOCK
        seq_start = first_tile.astype(jnp.int32) if n == 0 else 0
        out = []
        for kvh in range(N_KV_HEADS):
            k2 = window_keys(kvh, r)
            q4 = jnp.concatenate([qbuf[kvh * Q_PER_KV + g, pl.ds(r, BLOCK), :] for g in range(Q_PER_KV)], axis=0)
            s = lax.dot_general(q4, k2, (((1,), (1,)), ((), ())), preferred_element_type=_F32)
            out.append(s + bias_buf[seq_start, kvh])
        return out

    def window_values(buf, kvh, r):
        top = buf[kvh, pl.ds(r, BF16_ROWS), :].astype(_F32)
        top = jnp.where(sink_value_slot, 0.0, top).astype(_BF16)
        return jnp.concatenate([top, buf[kvh, pl.ds(r + BF16_ROWS, 2 * BLOCK - BF16_ROWS), :]], axis=0)

    def softmax_pv(n, s_list):
        r = n * BLOCK
        for kvh in range(N_KV_HEADS):
            probs = []
            for g in range(Q_PER_KV):
                sg = s_list[kvh][g * BLOCK:(g + 1) * BLOCK]
                m = jnp.max(sg, axis=-1, keepdims=True)
                probs.append(jnp.exp2(sg - m).astype(_BF16))
            p_even = jnp.concatenate([probs[0], probs[2]], axis=0)
            p_odd = jnp.concatenate([probs[1], probs[3]], axis=0)
            o = (jnp.dot(p_even, window_values(ve_buf, kvh, r), preferred_element_type=_F32)
                 + jnp.dot(p_odd, window_values(vo_buf, kvh, r), preferred_element_type=_F32))
            res = o[:, 0:LANES] / o[:, LANES:2 * LANES]
            for pair in range(2):
                c0 = (kvh * 2 + pair) * LANES
                gate = ga_buf[pl.ds(r, BLOCK), c0:c0 + LANES]
                mix_ref[pl.ds(r, BLOCK), n_local + c0:n_local + c0 + LANES] = (
                    res[pair * BLOCK:(pair + 1) * BLOCK] * gate).astype(_BF16)

    def out_attn(r0, rows):
        for q in range(D_MODEL // MXU_COLS):
            cols = slice(q * MXU_COLS, (q + 1) * MXU_COLS)
            o_ref[0, pl.ds(r0, rows), cols] = x_ref[0, pl.ds(r0, rows), cols] + jnp.dot(
                mix_ref[pl.ds(r0, rows), n_local:D_MIX], wout_bf[n_local:D_MIX, cols],
                preferred_element_type=_F32)

    pending = [scores(n) for n in range(SCORE_LOOKAHEAD)]
    for n in range(NB):
        if n + SCORE_LOOKAHEAD < NB:
            pending.append(scores(n + SCORE_LOOKAHEAD))
        if n >= OUT_BLOCKS and n % OUT_BLOCKS == 0:
            out_attn((n - OUT_BLOCKS) * BLOCK, OUT_BLOCKS * BLOCK)
        softmax_pv(n, pending.pop(0))
    out_attn((NB - OUT_BLOCKS) * BLOCK, OUT_BLOCKS * BLOCK)

    y_pool = jnp.dot(zp_buf[...], poolw_ref[0].astype(_BF16), preferred_element_type=_F32)
    y_pool = y_pool * pscale_ref[layer:layer + 1, :]
    mix_ref[:, 0:POOL_WIDTH] = (y_pool * gp_buf[...]).astype(_BF16)
    y_conv = jnp.dot(zc_buf[...], pw_ref[0].astype(_BF16), preferred_element_type=_F32)
    mix_ref[:, POOL_WIDTH:POOL_WIDTH + CONV_WIDTH] = (y_conv * gc_buf[...]).astype(_BF16)

    def final_chunk(c):
        r = c * CHUNK
        xc = o_ref[0, pl.ds(r, CHUNK), :]
        ms = jnp.mean(xc * xc, axis=-1, keepdims=True)
        o_ref[0, pl.ds(r, CHUNK), :] = xc * lax.rsqrt(ms + EPS) * fg_ref[...]

    half_rows = TS
    for r0 in range(0, TS, half_rows):
        rows = pl.ds(r0, half_rows)
        for q in range(D_MODEL // MXU_COLS):
            cols = slice(q * MXU_COLS, (q + 1) * MXU_COLS)
            o_ref[0, rows, cols] = o_ref[0, rows, cols] + jnp.dot(
                mix_ref[rows, 0:n_local], wout_bf[0:n_local, cols], preferred_element_type=_F32)
        if final:
            for c in range(r0 // CHUNK, (r0 + half_rows) // CHUNK):
                final_chunk(c)


def _bias_table():
    i = np.arange(BLOCK)[:, None]
    jj = np.arange(2 * BLOCK)[None, :]
    dist = BLOCK + i - jj
    in_band = (dist >= 0) & (dist < WINDOW)
    slopes = np.asarray([2.0 ** (-8.0 * (h + 1) / N_Q_HEADS) for h in range(N_Q_HEADS)], dtype=np.float32)
    tab = np.empty((2, N_KV_HEADS, Q_PER_KV * BLOCK, 2 * BLOCK), np.float32)
    for start in range(2):
        valid = in_band & ~((jj < BLOCK) & (start == 1))
        for h in range(N_Q_HEADS):
            bias = (-slopes[h] * LOG2E) * dist.astype(np.float64)
            kvh, g = divmod(h, Q_PER_KV)
            tab[start, kvh, g * BLOCK:(g + 1) * BLOCK] = np.where(valid, bias, -1e30).astype(np.float32)
    return tab


def _inv_count_table():
    t = np.arange(HALO_U, dtype=np.float32)[:, None]
    gw = POOL_WIDTH // POOL_GROUPS
    wnd = np.repeat(np.asarray(POOL_WINDOWS, np.float32), gw)[None, :]
    return (1.0 / np.minimum(t + 1.0, wnd)).astype(np.float32)


def _layer_call(batch, seq, layer, final):
    n_tiles = seq // TS
    const = lambda shape: pl.BlockSpec(shape, lambda b, j: (0,) * len(shape), pipeline_mode=pl.Buffered(1))
    per_layer = lambda shape: pl.BlockSpec((1,) + shape, lambda b, j: (layer,) + (0,) * len(shape),
                                           pipeline_mode=pl.Buffered(1))
    in_specs = [
        pl.BlockSpec((1, TS, D_MODEL), lambda b, j: (b, j, 0)),
        const((DEPTH, D_MODEL)),
        per_layer((D_MODEL, D_IN)),
        per_layer((POOL_WIDTH, POOL_WIDTH)),
        const((DEPTH, POOL_WIDTH)),
        const((HALO_U, POOL_WIDTH)),
        per_layer((CONV_KERNEL, CONV_WIDTH)),
        const((DEPTH, CONV_WIDTH)),
        const((DEPTH, CONV_WIDTH)),
        const((DEPTH, CONV_WIDTH)),
        per_layer((CONV_WIDTH, CONV_WIDTH)),
        pl.BlockSpec(memory_space=pltpu.SMEM),
        const((2, N_KV_HEADS, Q_PER_KV * BLOCK, 2 * BLOCK)),
        per_layer((D_MIX, D_MODEL)),
        const((1, D_MODEL)),
    ]
    scratch = [
        pltpu.VMEM((TS, D_MODEL), _BF16),
        pltpu.VMEM((TS + HALO_U, POOL_WIDTH), _F32),
        pltpu.VMEM((TS + HALO_H, CONV_WIDTH), _F32),
        pltpu.VMEM((TS, POOL_WIDTH), _F32),
        pltpu.VMEM((TS, CONV_WIDTH), _F32),
        pltpu.VMEM((TS, ATTN_WIDTH), _F32),
        pltpu.VMEM((TS, POOL_WIDTH), _BF16),
        pltpu.VMEM((TS, CONV_WIDTH), _BF16),
        pltpu.VMEM((N_KV_HEADS, TS + BLOCK, HEAD_DIM), _BF16),
        pltpu.VMEM((N_KV_HEADS, TS + BLOCK, 2 * LANES), _BF16),
        pltpu.VMEM((N_KV_HEADS, TS + BLOCK, 2 * LANES), _BF16),
        pltpu.VMEM((N_Q_HEADS, TS, HEAD_DIM), _BF16),
        pltpu.VMEM((TS, D_MIX), _BF16),
        pltpu.VMEM((2, N_KV_HEADS, Q_PER_KV * BLOCK, 2 * BLOCK), _F32),
        pltpu.VMEM((D_MODEL, D_IN), _BF16),
        pltpu.VMEM((D_MIX, D_MODEL), _BF16),
    ]
    return pl.pallas_call(
        functools.partial(_layer_body, layer=layer, final=final),
        out_shape=jax.ShapeDtypeStruct((batch, seq, D_MODEL), _F32),
        grid=(batch, n_tiles),
        in_specs=in_specs,
        out_specs=pl.BlockSpec((1, TS, D_MODEL), lambda b, j: (b, j, 0)),
        scratch_shapes=scratch,
        compiler_params=pltpu.CompilerParams(
            dimension_semantics=("arbitrary", "arbitrary"),
            vmem_limit_bytes=VMEM_LIMIT_BYTES,
        ),
        name="hybrid_layer_final" if final else "hybrid_layer",
    )


def kernel(x, ln_g, w_in, pool_w, pool_scale, conv_dw, conv_b, conv_ln_g, conv_ln_b, conv_pw, attn_sinks, w_out, final_g):
    batch, seq, d_model = x.shape
    assert d_model == D_MODEL and seq % TS == 0 and TS % BLOCK == 0
    bias = jnp.asarray(_bias_table())
    inv_cnt = jnp.asarray(_inv_count_table())
    poolw_bd = jnp.stack([jax.scipy.linalg.block_diag(*[pool_w[l, g] for g in range(POOL_GROUPS)])
                          for l in range(DEPTH)])
    params = (ln_g, w_in, poolw_bd, pool_scale, inv_cnt, conv_dw, conv_b, conv_ln_g, conv_ln_b, conv_pw,
              attn_sinks, bias, w_out, final_g.reshape(1, -1))
    for l in range(DEPTH):
        x = _layer_call(batch, seq, l, final=(l == DEPTH - 1))(x, *params)
    return x
```

```python
import functools

import numpy as np
import jax
import jax.numpy as jnp
from jax import lax
from jax.experimental import pallas as pl
from jax.experimental.pallas import tpu as pltpu

D_MODEL = 1024
DEPTH = 2
POOL_WIDTH = 256
POOL_GROUPS = 4
POOL_WINDOWS = (2, 4, 8, 16)
CONV_WIDTH = 256
CONV_KERNEL = 31
HEAD_DIM = 64
N_Q_HEADS = 8
N_KV_HEADS = 2
Q_PER_KV = N_Q_HEADS // N_KV_HEADS
ATTN_WIDTH = N_Q_HEADS * HEAD_DIM
KV_WIDTH = N_KV_HEADS * HEAD_DIM
WINDOW = 128
BLOCK = 128
D_MIX = POOL_WIDTH + CONV_WIDTH + ATTN_WIDTH
D_IN = 2 * POOL_WIDTH + 3 * CONV_WIDTH + 2 * ATTN_WIDTH + 2 * KV_WIDTH
EPS = 1e-6

C_UPOOL = 0
C_GPOOL = C_UPOOL + POOL_WIDTH
C_CA = C_GPOOL + POOL_WIDTH
C_CB = C_CA + CONV_WIDTH
C_GCONV = C_CB + CONV_WIDTH
C_Q = C_GCONV + CONV_WIDTH
C_K = C_Q + ATTN_WIDTH
C_V = C_K + KV_WIDTH
C_GATTN = C_V + KV_WIDTH

SUBLANES = 8
LANES = 128
BF16_ROWS = 16
MXU_COLS = 256
TS = 1024
NB = TS // BLOCK
OUT_BLOCKS = 8
GLU_ROW_GROUPS = (128, 128, 256, 512)
CHUNK = 64
GROUPS = CHUNK // SUBLANES
HALO_U = 16
HALO_H = 32
VMEM_LIMIT_BYTES = 63 * 1024 * 1024
LOG2E = 1.4426950408889634

_F32 = jnp.float32
_BF16 = jnp.bfloat16


def _silu(v):
    return v * jax.nn.sigmoid(v)


def _unrolled(n, body):
    for c in range(n):
        body(c)


def _shift_down(groups, d, row):
    rolled = [pltpu.roll(g, d, axis=0) for g in groups]
    take_prev = row < d
    out = [rolled[0]]
    for i in range(1, len(groups)):
        out.append(jnp.where(take_prev, rolled[i - 1], rolled[i]))
    return out


def _layer_body(x_ref, lng_ref, win_ref, poolw_ref, pscale_ref, invcnt_ref, dw_ref, cb_ref, clg_ref, clb_ref,
                pw_ref, sinks_ref, bias_ref, wout_ref, fg_ref, o_ref,
                hn_ref, ubuf, hbuf, gp_buf, gc_buf, ga_buf, zp_buf, zc_buf, kbuf, ve_buf, vo_buf, qbuf, mix_ref,
                bias_buf, win_bf, wout_bf, *, layer, final):
    j = pl.program_id(1)
    first_tile = j == 0

    @pl.when(jnp.logical_and(pl.program_id(0) == 0, first_tile))
    def _():
        for c in range(D_IN // MXU_COLS):
            cols = slice(c * MXU_COLS, (c + 1) * MXU_COLS)
            win_bf[:, cols] = win_ref[0, :, cols].astype(_BF16)
        for c in range(D_MODEL // MXU_COLS):
            cols = slice(c * MXU_COLS, (c + 1) * MXU_COLS)
            wout_bf[:, cols] = wout_ref[0, :, cols].astype(_BF16)

    row = lax.broadcasted_iota(jnp.int32, (SUBLANES, LANES), 0)
    lane = lax.broadcasted_iota(jnp.int32, (SUBLANES, LANES), 1)
    low_half = lane < HEAD_DIM

    @pl.when(first_tile)
    def _():
        sink_slot = lax.broadcasted_iota(jnp.int32, (BLOCK, 2 * BLOCK), 1) == 0
        for start in range(2):
            for h in range(N_Q_HEADS):
                kvh, g = divmod(h, Q_PER_KV)
                rows = pl.ds(g * BLOCK, BLOCK)
                bias_buf[start, kvh, rows, :] = jnp.where(
                    sink_slot, sinks_ref[layer, h] * LOG2E, bias_ref[start, kvh, rows, :])
        ubuf[0:HALO_U, :] = jnp.zeros((HALO_U, POOL_WIDTH), _F32)
        hbuf[0:HALO_H, :] = jnp.zeros((HALO_H, CONV_WIDTH), _F32)
        kbuf[:, 0:BLOCK, :] = jnp.zeros((N_KV_HEADS, BLOCK, HEAD_DIM), _BF16)
        zeros = jnp.zeros((TS + BLOCK, LANES), _BF16)
        lane_t = lax.broadcasted_iota(jnp.int32, (TS + BLOCK, LANES), 1)
        ones_lo = jnp.where(lane_t < HEAD_DIM, 1.0, 0.0).astype(_BF16)
        ones_hi = jnp.where(lane_t < HEAD_DIM, 0.0, 1.0).astype(_BF16)
        for kvh in range(N_KV_HEADS):
            ve_buf[kvh, :, 0:LANES] = zeros
            vo_buf[kvh, :, 0:LANES] = zeros
            ve_buf[kvh, :, LANES:2 * LANES] = ones_lo
            vo_buf[kvh, :, LANES:2 * LANES] = ones_hi

    @pl.when(jnp.logical_not(first_tile))
    def _():
        ubuf[0:HALO_U, :] = ubuf[TS:TS + HALO_U, :]
        hbuf[0:HALO_H, :] = hbuf[TS:TS + HALO_H, :]
        kbuf[:, 0:BLOCK, :] = kbuf[:, TS:TS + BLOCK, :]
        ve_buf[:, 0:BLOCK, 0:LANES] = ve_buf[:, TS:TS + BLOCK, 0:LANES]
        vo_buf[:, 0:BLOCK, 0:LANES] = vo_buf[:, TS:TS + BLOCK, 0:LANES]

    def norm_chunk(c):
        r = c * CHUNK
        xc = x_ref[0, pl.ds(r, CHUNK), :]
        ms = jnp.mean(xc * xc, axis=-1, keepdims=True)
        y = xc * lax.rsqrt(ms + EPS) * lng_ref[layer:layer + 1, :]
        hn_ref[pl.ds(r, CHUNK), :] = y.astype(_BF16)

    _unrolled(TS // CHUNK, norm_chunk)

    def proj(c0, width, r0=0, rows=TS):
        w = win_bf[:, c0:c0 + width]
        return jnp.dot(hn_ref[pl.ds(r0, rows), :], w, preferred_element_type=_F32)

    r0 = 0
    for n_rows in GLU_ROW_GROUPS:
        rows = pl.ds(HALO_H + r0, n_rows)
        hbuf[rows, :] = proj(C_CA, CONV_WIDTH, r0, n_rows)
        hbuf[rows, :] = hbuf[rows, :] * jax.nn.sigmoid(proj(C_CB, CONV_WIDTH, r0, n_rows))
        r0 += n_rows

    kv = proj(C_K, 2 * KV_WIDTH)
    keys = kv[:, 0:KV_WIDTH]
    vals = kv[:, KV_WIDTH:2 * KV_WIDTH]
    vals_swapped = pltpu.roll(vals, HEAD_DIM, axis=1)
    lo = lax.broadcasted_iota(jnp.int32, (TS, LANES), 1) < HEAD_DIM
    for kvh in range(N_KV_HEADS):
        kbuf[kvh, BLOCK:BLOCK + TS, :] = keys[:, kvh * HEAD_DIM:(kvh + 1) * HEAD_DIM].astype(_BF16)
        own_lo, own_hi = (vals, vals_swapped) if kvh == 0 else (vals_swapped, vals)
        ve_buf[kvh, BLOCK:BLOCK + TS, 0:LANES] = jnp.where(lo, own_lo, 0.0).astype(_BF16)
        vo_buf[kvh, BLOCK:BLOCK + TS, 0:LANES] = jnp.where(lo, 0.0, own_hi).astype(_BF16)

    heads_per_group = MXU_COLS // HEAD_DIM
    for half in range(ATTN_WIDTH // MXU_COLS):
        qh = proj(C_Q + half * MXU_COLS, MXU_COLS) * (HEAD_DIM ** -0.5 * LOG2E)
        for i in range(heads_per_group):
            qbuf[half * heads_per_group + i, :, :] = qh[:, i * HEAD_DIM:(i + 1) * HEAD_DIM].astype(_BF16)
    for half in range(ATTN_WIDTH // MXU_COLS):
        ga_buf[:, half * MXU_COLS:(half + 1) * MXU_COLS] = _silu(proj(C_GATTN + half * MXU_COLS, MXU_COLS))

    ubuf[HALO_U:HALO_U + TS, :] = proj(C_UPOOL, POOL_WIDTH)
    gp_buf[...] = _silu(proj(C_GPOOL, POOL_WIDTH))
    gc_buf[...] = _silu(proj(C_GCONV, CONV_WIDTH))

    inv_w = [jnp.where(low_half, 1.0 / POOL_WINDOWS[0], 1.0 / POOL_WINDOWS[1]).astype(_F32),
             jnp.where(low_half, 1.0 / POOL_WINDOWS[2], 1.0 / POOL_WINDOWS[3]).astype(_F32)]

    def pool_chunk(c):
        r = c * CHUNK
        n_in = GROUPS + HALO_U // SUBLANES
        for col in range(2):
            lanes = slice(col * LANES, (col + 1) * LANES)
            u = [ubuf[pl.ds(r + SUBLANES * i, SUBLANES), lanes] for i in range(n_in)]
            s2 = [a + b for a, b in zip(u, _shift_down(u, 1, row))]
            s4 = [a + b for a, b in zip(s2, _shift_down(s2, 2, row))]
            if col == 0:
                wide, narrow = s4, s2
            else:
                s8 = [a + b for a, b in zip(s4, _shift_down(s4, 4, row))]
                s16 = [s8[0]] + [s8[i] + s8[i - 1] for i in range(1, n_in)]
                wide, narrow = s16, s8
            pieces = []
            for g in range(GROUPS):
                i = g + HALO_U // SUBLANES
                inv = inv_w[col]
                if c == 0 and g < HALO_U // SUBLANES:
                    tab = invcnt_ref[SUBLANES * g:SUBLANES * (g + 1), lanes]
                    inv = jnp.where(first_tile, tab, inv)
                pooled = jnp.where(low_half, narrow[i], wide[i]) * inv
                pieces.append(pooled - u[i])
            zp_buf[pl.ds(r, CHUNK), lanes] = jnp.concatenate(pieces, axis=0).astype(_BF16)

    taps_by_b = [[] for _ in range(SUBLANES)]
    for k in range(CONV_KERNEL):
        a, b = divmod(k + HALO_H - (CONV_KERNEL - 1), SUBLANES)
        taps_by_b[b].append((a, k))
    n_h = GROUPS + HALO_H // SUBLANES

    def conv_chunk(c):
        r = c * CHUNK
        outs = []
        for col in range(2):
            lanes = slice(col * LANES, (col + 1) * LANES)
            h = [hbuf[pl.ds(r + SUBLANES * m, SUBLANES), lanes] for m in range(n_h)]
            acc = None
            for b in range(SUBLANES):
                n_p = GROUPS if b == 0 else GROUPS + 1
                part = []
                for m in range(n_p):
                    p = None
                    for a, k in taps_by_b[b]:
                        term = h[m + a] * dw_ref[0, k:k + 1, lanes]
                        p = term if p is None else p + term
                    part.append(p)
                if b == 0:
                    acc = part
                else:
                    rolled = [pltpu.roll(p, SUBLANES - b, axis=0) for p in part]
                    keep = row < (SUBLANES - b)
                    acc = [acc[g] + jnp.where(keep, rolled[g], rolled[g + 1]) for g in range(GROUPS)]
            outs.append(jnp.concatenate(acc, axis=0))
        conv = jnp.concatenate(outs, axis=1) + cb_ref[layer:layer + 1, :]
        mu = jnp.mean(conv, axis=-1, keepdims=True)
        cen = conv - mu
        var = jnp.mean(cen * cen, axis=-1, keepdims=True)
        z = cen * lax.rsqrt(var + EPS) * clg_ref[layer:layer + 1, :] + clb_ref[layer:layer + 1, :]
        zc_buf[pl.ds(r, CHUNK), :] = _silu(z).astype(_BF16)

    _unrolled(TS // CHUNK, conv_chunk)
    _unrolled(TS // CHUNK, pool_chunk)

    n_local = POOL_WIDTH + CONV_WIDTH

    top_rows = lax.broadcasted_iota(jnp.int32, (BF16_ROWS, 2 * LANES), 0)
    top_lanes = lax.broadcasted_iota(jnp.int32, (BF16_ROWS, 2 * LANES), 1)
    sink_value_slot = jnp.logical_and(top_rows == 0, top_lanes < LANES)
    sink_key_slot = lax.broadcasted_iota(jnp.int32, (BF16_ROWS, HEAD_DIM), 0) == 0

    def window_keys(kvh, r):
        top = kbuf[kvh, pl.ds(r, BF16_ROWS), :].astype(_F32)
        top = jnp.where(sink_key_slot, 0.0, top).astype(_BF16)
        return jnp.concatenate([top, kbuf[kvh, pl.ds(r + BF16_ROWS, 2 * BLOCK - BF16_ROWS), :]], axis=0)

    def scores(n, kvh):
        r = n * BLOCK
        seq_start = first_tile.astype(jnp.int32) if n == 0 else 0
        k2 = window_keys(kvh, r)
        q4 = jnp.concatenate([qbuf[kvh * Q_PER_KV + g, pl.ds(r, BLOCK), :] for g in range(Q_PER_KV)], axis=0)
        s = lax.dot_general(q4, k2, (((1,), (1,)), ((), ())), preferred_element_type=_F32)
        return s + bias_buf[seq_start, kvh]

    def window_values(buf, kvh, r):
        top = buf[kvh, pl.ds(r, BF16_ROWS), :].astype(_F32)
        top = jnp.where(sink_value_slot, 0.0, top).astype(_BF16)
        return jnp.concatenate([top, buf[kvh, pl.ds(r + BF16_ROWS, 2 * BLOCK - BF16_ROWS), :]], axis=0)

    def softmax_pv(n, kvh, s):
        r = n * BLOCK
        probs = []
        for g in range(Q_PER_KV):
            sg = s[g * BLOCK:(g + 1) * BLOCK]
            m = jnp.max(sg, axis=-1, keepdims=True)
            probs.append(jnp.exp2(sg - m).astype(_BF16))
        p_even = jnp.concatenate([probs[0], probs[2]], axis=0)
        p_odd = jnp.concatenate([probs[1], probs[3]], axis=0)
        o = (jnp.dot(p_even, window_values(ve_buf, kvh, r), preferred_element_type=_F32)
             + jnp.dot(p_odd, window_values(vo_buf, kvh, r), preferred_element_type=_F32))
        res = o[:, 0:LANES] / o[:, LANES:2 * LANES]
        for pair in range(2):
            c0 = (kvh * 2 + pair) * LANES
            gate = ga_buf[pl.ds(r, BLOCK), c0:c0 + LANES]
            mix_ref[pl.ds(r, BLOCK), n_local + c0:n_local + c0 + LANES] = (
                res[pair * BLOCK:(pair + 1) * BLOCK] * gate).astype(_BF16)

    def out_attn(r0, rows):
        for q in range(D_MODEL // MXU_COLS):
            cols = slice(q * MXU_COLS, (q + 1) * MXU_COLS)
            o_ref[0, pl.ds(r0, rows), cols] = x_ref[0, pl.ds(r0, rows), cols] + jnp.dot(
                mix_ref[pl.ds(r0, rows), n_local:D_MIX], wout_bf[n_local:D_MIX, cols],
                preferred_element_type=_F32)

    s_cur = [scores(0, kvh) for kvh in range(N_KV_HEADS)]
    for n in range(NB):
        if n >= OUT_BLOCKS and n % OUT_BLOCKS == 0:
            out_attn((n - OUT_BLOCKS) * BLOCK, OUT_BLOCKS * BLOCK)
        s_next = []
        for kvh in range(N_KV_HEADS):
            if n + 1 < NB:
                s_next.append(scores(n + 1, kvh))
            softmax_pv(n, kvh, s_cur[kvh])
        s_cur = s_next
    out_attn((NB - OUT_BLOCKS) * BLOCK, OUT_BLOCKS * BLOCK)

    y_pool = jnp.dot(zp_buf[...], poolw_ref[0].astype(_BF16), preferred_element_type=_F32)
    y_pool = y_pool * pscale_ref[layer:layer + 1, :]
    mix_ref[:, 0:POOL_WIDTH] = (y_pool * gp_buf[...]).astype(_BF16)
    y_conv = jnp.dot(zc_buf[...], pw_ref[0].astype(_BF16), preferred_element_type=_F32)
    mix_ref[:, POOL_WIDTH:POOL_WIDTH + CONV_WIDTH] = (y_conv * gc_buf[...]).astype(_BF16)

    def final_chunk(c):
        r = c * CHUNK
        xc = o_ref[0, pl.ds(r, CHUNK), :]
        ms = jnp.mean(xc * xc, axis=-1, keepdims=True)
        o_ref[0, pl.ds(r, CHUNK), :] = xc * lax.rsqrt(ms + EPS) * fg_ref[...]

    half_rows = TS
    for r0 in range(0, TS, half_rows):
        rows = pl.ds(r0, half_rows)
        for q in range(D_MODEL // MXU_COLS):
            cols = slice(q * MXU_COLS, (q + 1) * MXU_COLS)
            o_ref[0, rows, cols] = o_ref[0, rows, cols] + jnp.dot(
                mix_ref[rows, 0:n_local], wout_bf[0:n_local, cols], preferred_element_type=_F32)
        if final:
            for c in range(r0 // CHUNK, (r0 + half_rows) // CHUNK):
                final_chunk(c)


def _bias_table():
    i = np.arange(BLOCK)[:, None]
    jj = np.arange(2 * BLOCK)[None, :]
    dist = BLOCK + i - jj
    in_band = (dist >= 0) & (dist < WINDOW)
    slopes = np.asarray([2.0 ** (-8.0 * (h + 1) / N_Q_HEADS) for h in range(N_Q_HEADS)], dtype=np.float32)
    tab = np.empty((2, N_KV_HEADS, Q_PER_KV * BLOCK, 2 * BLOCK), np.float32)
    for start in range(2):
        valid = in_band & ~((jj < BLOCK) & (start == 1))
        for h in range(N_Q_HEADS):
            bias = (-slopes[h] * LOG2E) * dist.astype(np.float64)
            kvh, g = divmod(h, Q_PER_KV)
            tab[start, kvh, g * BLOCK:(g + 1) * BLOCK] = np.where(valid, bias, -1e30).astype(np.float32)
    return tab


def _inv_count_table():
    t = np.arange(HALO_U, dtype=np.float32)[:, None]
    gw = POOL_WIDTH // POOL_GROUPS
    wnd = np.repeat(np.asarray(POOL_WINDOWS, np.float32), gw)[None, :]
    return (1.0 / np.minimum(t + 1.0, wnd)).astype(np.float32)


def _layer_call(batch, seq, layer, final):
    n_tiles = seq // TS
    const = lambda shape: pl.BlockSpec(shape, lambda b, j: (0,) * len(shape), pipeline_mode=pl.Buffered(1))
    per_layer = lambda shape: pl.BlockSpec((1,) + shape, lambda b, j: (layer,) + (0,) * len(shape),
                                           pipeline_mode=pl.Buffered(1))
    in_specs = [
        pl.BlockSpec((1, TS, D_MODEL), lambda b, j: (b, j, 0)),
        const((DEPTH, D_MODEL)),
        per_layer((D_MODEL, D_IN)),
        per_layer((POOL_WIDTH, POOL_WIDTH)),
        const((DEPTH, POOL_WIDTH)),
        const((HALO_U, POOL_WIDTH)),
        per_layer((CONV_KERNEL, CONV_WIDTH)),
        const((DEPTH, CONV_WIDTH)),
        const((DEPTH, CONV_WIDTH)),
        const((DEPTH, CONV_WIDTH)),
        per_layer((CONV_WIDTH, CONV_WIDTH)),
        pl.BlockSpec(memory_space=pltpu.SMEM),
        const((2, N_KV_HEADS, Q_PER_KV * BLOCK, 2 * BLOCK)),
        per_layer((D_MIX, D_MODEL)),
        const((1, D_MODEL)),
    ]
    scratch = [
        pltpu.VMEM((TS, D_MODEL), _BF16),
        pltpu.VMEM((TS + HALO_U, POOL_WIDTH), _F32),
        pltpu.VMEM((TS + HALO_H, CONV_WIDTH), _F32),
        pltpu.VMEM((TS, POOL_WIDTH), _F32),
        pltpu.VMEM((TS, CONV_WIDTH), _F32),
        pltpu.VMEM((TS, ATTN_WIDTH), _F32),
        pltpu.VMEM((TS, POOL_WIDTH), _BF16),
        pltpu.VMEM((TS, CONV_WIDTH), _BF16),
        pltpu.VMEM((N_KV_HEADS, TS + BLOCK, HEAD_DIM), _BF16),
        pltpu.VMEM((N_KV_HEADS, TS + BLOCK, 2 * LANES), _BF16),
        pltpu.VMEM((N_KV_HEADS, TS + BLOCK, 2 * LANES), _BF16),
        pltpu.VMEM((N_Q_HEADS, TS, HEAD_DIM), _BF16),
        pltpu.VMEM((TS, D_MIX), _BF16),
        pltpu.VMEM((2, N_KV_HEADS, Q_PER_KV * BLOCK, 2 * BLOCK), _F32),
        pltpu.VMEM((D_MODEL, D_IN), _BF16),
        pltpu.VMEM((D_MIX, D_MODEL), _BF16),
    ]
    return pl.pallas_call(
        functools.partial(_layer_body, layer=layer, final=final),
        out_shape=jax.ShapeDtypeStruct((batch, seq, D_MODEL), _F32),
        grid=(batch, n_tiles),
        in_specs=in_specs,
        out_specs=pl.BlockSpec((1, TS, D_MODEL), lambda b, j: (b, j, 0)),
        scratch_shapes=scratch,
        compiler_params=pltpu.CompilerParams(
            dimension_semantics=("arbitrary", "arbitrary"),
            vmem_limit_bytes=VMEM_LIMIT_BYTES,
        ),
        name="hybrid_layer_final" if final else "hybrid_layer",
    )


def kernel(x, ln_g, w_in, pool_w, pool_scale, conv_dw, conv_b, conv_ln_g, conv_ln_b, conv_pw, attn_sinks, w_out, final_g):
    batch, seq, d_model = x.shape
    assert d_model == D_MODEL and seq % TS == 0 and TS % BLOCK == 0
    bias = jnp.asarray(_bias_table())
    inv_cnt = jnp.asarray(_inv_count_table())
    poolw_bd = jnp.stack([jax.scipy.linalg.block_diag(*[pool_w[l, g] for g in range(POOL_GROUPS)])
                          for l in range(DEPTH)])
    params = (ln_g, w_in, poolw_bd, pool_scale, inv_cnt, conv_dw, conv_b, conv_ln_g, conv_ln_b, conv_pw,
              attn_sinks, bias, w_out, final_g.reshape(1, -1))
    for l in range(DEPTH):
        x = _layer_call(batch, seq, l, final=(l == DEPTH - 1))(x, *params)
    return x
```

```python
import functools

import numpy as np
import jax
import jax.numpy as jnp
from jax import lax
from jax.experimental import pallas as pl
from jax.experimental.pallas import tpu as pltpu

D_MODEL = 1024
DEPTH = 2
POOL_WIDTH = 256
POOL_GROUPS = 4
POOL_WINDOWS = (2, 4, 8, 16)
CONV_WIDTH = 256
CONV_KERNEL = 31
HEAD_DIM = 64
N_Q_HEADS = 8
N_KV_HEADS = 2
Q_PER_KV = N_Q_HEADS // N_KV_HEADS
ATTN_WIDTH = N_Q_HEADS * HEAD_DIM
KV_WIDTH = N_KV_HEADS * HEAD_DIM
WINDOW = 128
BLOCK = 128
D_MIX = POOL_WIDTH + CONV_WIDTH + ATTN_WIDTH
D_IN = 2 * POOL_WIDTH + 3 * CONV_WIDTH + 2 * ATTN_WIDTH + 2 * KV_WIDTH
EPS = 1e-6

C_UPOOL = 0
C_GPOOL = C_UPOOL + POOL_WIDTH
C_CA = C_GPOOL + POOL_WIDTH
C_CB = C_CA + CONV_WIDTH
C_GCONV = C_CB + CONV_WIDTH
C_Q = C_GCONV + CONV_WIDTH
C_K = C_Q + ATTN_WIDTH
C_V = C_K + KV_WIDTH
C_GATTN = C_V + KV_WIDTH

SUBLANES = 8
LANES = 128
BF16_ROWS = 16
MXU_COLS = 256
TS = 1024
NB = TS // BLOCK
OUT_BLOCKS = 8
GLU_ROW_GROUPS = (64, 64, 128, 256, 512)
CHUNK = 64
GROUPS = CHUNK // SUBLANES
HALO_U = 16
HALO_H = 32
VMEM_LIMIT_BYTES = 63 * 1024 * 1024
LOG2E = 1.4426950408889634

_F32 = jnp.float32
_BF16 = jnp.bfloat16


def _silu(v):
    return v * jax.nn.sigmoid(v)


def _unrolled(n, body):
    for c in range(n):
        body(c)


def _shift_down(groups, d, row):
    rolled = [pltpu.roll(g, d, axis=0) for g in groups]
    take_prev = row < d
    out = [rolled[0]]
    for i in range(1, len(groups)):
        out.append(jnp.where(take_prev, rolled[i - 1], rolled[i]))
    return out


def _layer_body(x_ref, lng_ref, win_ref, poolw_ref, pscale_ref, invcnt_ref, dw_ref, cb_ref, clg_ref, clb_ref,
                pw_ref, sinks_ref, bias_ref, wout_ref, fg_ref, o_ref,
                hn_ref, ubuf, hbuf, gp_buf, gc_buf, ga_buf, zp_buf, zc_buf, kbuf, ve_buf, vo_buf, qbuf, mix_ref,
                bias_buf, win_bf, wout_bf, *, layer, final):
    j = pl.program_id(1)
    first_tile = j == 0

    @pl.when(jnp.logical_and(pl.program_id(0) == 0, first_tile))
    def _():
        for c in range(D_IN // MXU_COLS):
            cols = slice(c * MXU_COLS, (c + 1) * MXU_COLS)
            win_bf[:, cols] = win_ref[0, :, cols].astype(_BF16)
        for c in range(D_MODEL // MXU_COLS):
            cols = slice(c * MXU_COLS, (c + 1) * MXU_COLS)
            wout_bf[:, cols] = wout_ref[0, :, cols].astype(_BF16)

    row = lax.broadcasted_iota(jnp.int32, (SUBLANES, LANES), 0)
    lane = lax.broadcasted_iota(jnp.int32, (SUBLANES, LANES), 1)
    low_half = lane < HEAD_DIM

    @pl.when(first_tile)
    def _():
        sink_slot = lax.broadcasted_iota(jnp.int32, (BLOCK, 2 * BLOCK), 1) == 0
        for start in range(2):
            for h in range(N_Q_HEADS):
                kvh, g = divmod(h, Q_PER_KV)
                rows = pl.ds(g * BLOCK, BLOCK)
                bias_buf[start, kvh, rows, :] = jnp.where(
                    sink_slot, sinks_ref[layer, h] * LOG2E, bias_ref[start, kvh, rows, :])
        ubuf[0:HALO_U, :] = jnp.zeros((HALO_U, POOL_WIDTH), _F32)
        hbuf[0:HALO_H, :] = jnp.zeros((HALO_H, CONV_WIDTH), _F32)
        kbuf[:, 0:BLOCK, :] = jnp.zeros((N_KV_HEADS, BLOCK, HEAD_DIM), _BF16)
        zeros = jnp.zeros((TS + BLOCK, LANES), _BF16)
        lane_t = lax.broadcasted_iota(jnp.int32, (TS + BLOCK, LANES), 1)
        ones_lo = jnp.where(lane_t < HEAD_DIM, 1.0, 0.0).astype(_BF16)
        ones_hi = jnp.where(lane_t < HEAD_DIM, 0.0, 1.0).astype(_BF16)
        for kvh in range(N_KV_HEADS):
            ve_buf[kvh, :, 0:LANES] = zeros
            vo_buf[kvh, :, 0:LANES] = zeros
            ve_buf[kvh, :, LANES:2 * LANES] = ones_lo
            vo_buf[kvh, :, LANES:2 * LANES] = ones_hi

    @pl.when(jnp.logical_not(first_tile))
    def _():
        ubuf[0:HALO_U, :] = ubuf[TS:TS + HALO_U, :]
        hbuf[0:HALO_H, :] = hbuf[TS:TS + HALO_H, :]
        kbuf[:, 0:BLOCK, :] = kbuf[:, TS:TS + BLOCK, :]
        ve_buf[:, 0:BLOCK, 0:LANES] = ve_buf[:, TS:TS + BLOCK, 0:LANES]
        vo_buf[:, 0:BLOCK, 0:LANES] = vo_buf[:, TS:TS + BLOCK, 0:LANES]

    def norm_chunk(c):
        r = c * CHUNK
        xc = x_ref[0, pl.ds(r, CHUNK), :]
        ms = jnp.mean(xc * xc, axis=-1, keepdims=True)
        y = xc * lax.rsqrt(ms + EPS) * lng_ref[layer:layer + 1, :]
        hn_ref[pl.ds(r, CHUNK), :] = y.astype(_BF16)

    _unrolled(TS // CHUNK, norm_chunk)

    def proj(c0, width, r0=0, rows=TS):
        w = win_bf[:, c0:c0 + width]
        return jnp.dot(hn_ref[pl.ds(r0, rows), :], w, preferred_element_type=_F32)

    r0 = 0
    for n_rows in GLU_ROW_GROUPS:
        rows = pl.ds(HALO_H + r0, n_rows)
        hbuf[rows, :] = proj(C_CA, CONV_WIDTH, r0, n_rows)
        hbuf[rows, :] = hbuf[rows, :] * jax.nn.sigmoid(proj(C_CB, CONV_WIDTH, r0, n_rows))
        r0 += n_rows

    kv = proj(C_K, 2 * KV_WIDTH)
    keys = kv[:, 0:KV_WIDTH]
    vals = kv[:, KV_WIDTH:2 * KV_WIDTH]
    vals_swapped = pltpu.roll(vals, HEAD_DIM, axis=1)
    lo = lax.broadcasted_iota(jnp.int32, (TS, LANES), 1) < HEAD_DIM
    for kvh in range(N_KV_HEADS):
        kbuf[kvh, BLOCK:BLOCK + TS, :] = keys[:, kvh * HEAD_DIM:(kvh + 1) * HEAD_DIM].astype(_BF16)
        own_lo, own_hi = (vals, vals_swapped) if kvh == 0 else (vals_swapped, vals)
        ve_buf[kvh, BLOCK:BLOCK + TS, 0:LANES] = jnp.where(lo, own_lo, 0.0).astype(_BF16)
        vo_buf[kvh, BLOCK:BLOCK + TS, 0:LANES] = jnp.where(lo, 0.0, own_hi).astype(_BF16)

    heads_per_group = MXU_COLS // HEAD_DIM
    for half in range(ATTN_WIDTH // MXU_COLS):
        qh = proj(C_Q + half * MXU_COLS, MXU_COLS) * (HEAD_DIM ** -0.5 * LOG2E)
        for i in range(heads_per_group):
            qbuf[half * heads_per_group + i, :, :] = qh[:, i * HEAD_DIM:(i + 1) * HEAD_DIM].astype(_BF16)
    for half in range(ATTN_WIDTH // MXU_COLS):
        ga_buf[:, half * MXU_COLS:(half + 1) * MXU_COLS] = _silu(proj(C_GATTN + half * MXU_COLS, MXU_COLS))

    ubuf[HALO_U:HALO_U + TS, :] = proj(C_UPOOL, POOL_WIDTH)
    gp_buf[...] = _silu(proj(C_GPOOL, POOL_WIDTH))
    gc_buf[...] = _silu(proj(C_GCONV, CONV_WIDTH))

    inv_w = [jnp.where(low_half, 1.0 / POOL_WINDOWS[0], 1.0 / POOL_WINDOWS[1]).astype(_F32),
             jnp.where(low_half, 1.0 / POOL_WINDOWS[2], 1.0 / POOL_WINDOWS[3]).astype(_F32)]

    def pool_chunk(c):
        r = c * CHUNK
        n_in = GROUPS + HALO_U // SUBLANES
        for col in range(2):
            lanes = slice(col * LANES, (col + 1) * LANES)
            u = [ubuf[pl.ds(r + SUBLANES * i, SUBLANES), lanes] for i in range(n_in)]
            s2 = [a + b for a, b in zip(u, _shift_down(u, 1, row))]
            s4 = [a + b for a, b in zip(s2, _shift_down(s2, 2, row))]
            if col == 0:
                wide, narrow = s4, s2
            else:
                s8 = [a + b for a, b in zip(s4, _shift_down(s4, 4, row))]
                s16 = [s8[0]] + [s8[i] + s8[i - 1] for i in range(1, n_in)]
                wide, narrow = s16, s8
            pieces = []
            for g in range(GROUPS):
                i = g + HALO_U // SUBLANES
                inv = inv_w[col]
                if c == 0 and g < HALO_U // SUBLANES:
                    tab = invcnt_ref[SUBLANES * g:SUBLANES * (g + 1), lanes]
                    inv = jnp.where(first_tile, tab, inv)
                pooled = jnp.where(low_half, narrow[i], wide[i]) * inv
                pieces.append(pooled - u[i])
            zp_buf[pl.ds(r, CHUNK), lanes] = jnp.concatenate(pieces, axis=0).astype(_BF16)

    taps_by_b = [[] for _ in range(SUBLANES)]
    for k in range(CONV_KERNEL):
        a, b = divmod(k + HALO_H - (CONV_KERNEL - 1), SUBLANES)
        taps_by_b[b].append((a, k))
    n_h = GROUPS + HALO_H // SUBLANES

    def conv_chunk(c):
        r = c * CHUNK
        outs = []
        for col in range(2):
            lanes = slice(col * LANES, (col + 1) * LANES)
            h = [hbuf[pl.ds(r + SUBLANES * m, SUBLANES), lanes] for m in range(n_h)]
            acc = None
            for b in range(SUBLANES):
                n_p = GROUPS if b == 0 else GROUPS + 1
                part = []
                for m in range(n_p):
                    p = None
                    for a, k in taps_by_b[b]:
                        term = h[m + a] * dw_ref[0, k:k + 1, lanes]
                        p = term if p is None else p + term
                    part.append(p)
                if b == 0:
                    acc = part
                else:
                    rolled = [pltpu.roll(p, SUBLANES - b, axis=0) for p in part]
                    keep = row < (SUBLANES - b)
                    acc = [acc[g] + jnp.where(keep, rolled[g], rolled[g + 1]) for g in range(GROUPS)]
            outs.append(jnp.concatenate(acc, axis=0))
        conv = jnp.concatenate(outs, axis=1) + cb_ref[layer:layer + 1, :]
        mu = jnp.mean(conv, axis=-1, keepdims=True)
        cen = conv - mu
        var = jnp.mean(cen * cen, axis=-1, keepdims=True)
        z = cen * lax.rsqrt(var + EPS) * clg_ref[layer:layer + 1, :] + clb_ref[layer:layer + 1, :]
        zc_buf[pl.ds(r, CHUNK), :] = _silu(z).astype(_BF16)

    _unrolled(TS // CHUNK, conv_chunk)
    _unrolled(TS // CHUNK, pool_chunk)

    n_local = POOL_WIDTH + CONV_WIDTH

    top_rows = lax.broadcasted_iota(jnp.int32, (BF16_ROWS, 2 * LANES), 0)
    top_lanes = lax.broadcasted_iota(jnp.int32, (BF16_ROWS, 2 * LANES), 1)
    sink_value_slot = jnp.logical_and(top_rows == 0, top_lanes < LANES)
    sink_key_slot = lax.broadcasted_iota(jnp.int32, (BF16_ROWS, HEAD_DIM), 0) == 0

    def window_keys(kvh, r):
        top = kbuf[kvh, pl.ds(r, BF16_ROWS), :].astype(_F32)
        top = jnp.where(sink_key_slot, 0.0, top).astype(_BF16)
        return jnp.concatenate([top, kbuf[kvh, pl.ds(r + BF16_ROWS, 2 * BLOCK - BF16_ROWS), :]], axis=0)

    def scores(n):
        r = n * BLOCK
        seq_start = first_tile.astype(jnp.int32) if n == 0 else 0
        out = []
        for kvh in range(N_KV_HEADS):
            k2 = window_keys(kvh, r)
            q4 = jnp.concatenate([qbuf[kvh * Q_PER_KV + g, pl.ds(r, BLOCK), :] for g in range(Q_PER_KV)], axis=0)
            s = lax.dot_general(q4, k2, (((1,), (1,)), ((), ())), preferred_element_type=_F32)
            out.append(s + bias_buf[seq_start, kvh])
        return out

    def window_values(buf, kvh, r):
        top = buf[kvh, pl.ds(r, BF16_ROWS), :].astype(_F32)
        top = jnp.where(sink_value_slot, 0.0, top).astype(_BF16)
        return jnp.concatenate([top, buf[kvh, pl.ds(r + BF16_ROWS, 2 * BLOCK - BF16_ROWS), :]], axis=0)

    def softmax_pv(n, s_list):
        r = n * BLOCK
        for kvh in range(N_KV_HEADS):
            probs = []
            for g in range(Q_PER_KV):
                sg = s_list[kvh][g * BLOCK:(g + 1) * BLOCK]
                m = jnp.max(sg, axis=-1, keepdims=True)
                probs.append(jnp.exp2(sg - m).astype(_BF16))
            p_even = jnp.concatenate([probs[0], probs[2]], axis=0)
            p_odd = jnp.concatenate([probs[1], probs[3]], axis=0)
            o = (jnp.dot(p_even, window_values(ve_buf, kvh, r), preferred_element_type=_F32)
                 + jnp.dot(p_odd, window_values(vo_buf, kvh, r), preferred_element_type=_F32))
            res = o[:, 0:LANES] / o[:, LANES:2 * LANES]
            for pair in range(2):
                c0 = (kvh * 2 + pair) * LANES
                gate = ga_buf[pl.ds(r, BLOCK), c0:c0 + LANES]
                mix_ref[pl.ds(r, BLOCK), n_local + c0:n_local + c0 + LANES] = (
                    res[pair * BLOCK:(pair + 1) * BLOCK] * gate).astype(_BF16)

    def out_attn(r0, rows):
        for q in range(D_MODEL // MXU_COLS):
            cols = slice(q * MXU_COLS, (q + 1) * MXU_COLS)
            o_ref[0, pl.ds(r0, rows), cols] = x_ref[0, pl.ds(r0, rows), cols] + jnp.dot(
                mix_ref[pl.ds(r0, rows), n_local:D_MIX], wout_bf[n_local:D_MIX, cols],
                preferred_element_type=_F32)

    s_cur = scores(0)
    for n in range(NB):
        s_next = scores(n + 1) if n + 1 < NB else None
        if n >= OUT_BLOCKS and n % OUT_BLOCKS == 0:
            out_attn((n - OUT_BLOCKS) * BLOCK, OUT_BLOCKS * BLOCK)
        softmax_pv(n, s_cur)
        s_cur = s_next
    out_attn((NB - OUT_BLOCKS) * BLOCK, OUT_BLOCKS * BLOCK)

    y_pool = jnp.dot(zp_buf[...], poolw_ref[0].astype(_BF16), preferred_element_type=_F32)
    y_pool = y_pool * pscale_ref[layer:layer + 1, :]
    mix_ref[:, 0:POOL_WIDTH] = (y_pool * gp_buf[...]).astype(_BF16)
    y_conv = jnp.dot(zc_buf[...], pw_ref[0].astype(_BF16), preferred_element_type=_F32)
    mix_ref[:, POOL_WIDTH:POOL_WIDTH + CONV_WIDTH] = (y_conv * gc_buf[...]).astype(_BF16)

    def final_chunk(c):
        r = c * CHUNK
        xc = o_ref[0, pl.ds(r, CHUNK), :]
        ms = jnp.mean(xc * xc, axis=-1, keepdims=True)
        o_ref[0, pl.ds(r, CHUNK), :] = xc * lax.rsqrt(ms + EPS) * fg_ref[...]

    half_rows = TS
    for r0 in range(0, TS, half_rows):
        rows = pl.ds(r0, half_rows)
        for q in range(D_MODEL // MXU_COLS):
            cols = slice(q * MXU_COLS, (q + 1) * MXU_COLS)
            o_ref[0, rows, cols] = o_ref[0, rows, cols] + jnp.dot(
                mix_ref[rows, 0:n_local], wout_bf[0:n_local, cols], preferred_element_type=_F32)
        if final:
            for c in range(r0 // CHUNK, (r0 + half_rows) // CHUNK):
                final_chunk(c)


def _bias_table():
    i = np.arange(BLOCK)[:, None]
    jj = np.arange(2 * BLOCK)[None, :]
    dist = BLOCK + i - jj
    in_band = (dist >= 0) & (dist < WINDOW)
    slopes = np.asarray([2.0 ** (-8.0 * (h + 1) / N_Q_HEADS) for h in range(N_Q_HEADS)], dtype=np.float32)
    tab = np.empty((2, N_KV_HEADS, Q_PER_KV * BLOCK, 2 * BLOCK), np.float32)
    for start in range(2):
        valid = in_band & ~((jj < BLOCK) & (start == 1))
        for h in range(N_Q_HEADS):
            bias = (-slopes[h] * LOG2E) * dist.astype(np.float64)
            kvh, g = divmod(h, Q_PER_KV)
            tab[start, kvh, g * BLOCK:(g + 1) * BLOCK] = np.where(valid, bias, -1e30).astype(np.float32)
    return tab


def _inv_count_table():
    t = np.arange(HALO_U, dtype=np.float32)[:, None]
    gw = POOL_WIDTH // POOL_GROUPS
    wnd = np.repeat(np.asarray(POOL_WINDOWS, np.float32), gw)[None, :]
    return (1.0 / np.minimum(t + 1.0, wnd)).astype(np.float32)


def _layer_call(batch, seq, layer, final):
    n_tiles = seq // TS
    const = lambda shape: pl.BlockSpec(shape, lambda b, j: (0,) * len(shape), pipeline_mode=pl.Buffered(1))
    per_layer = lambda shape: pl.BlockSpec((1,) + shape, lambda b, j: (layer,) + (0,) * len(shape),
                                           pipeline_mode=pl.Buffered(1))
    in_specs = [
        pl.BlockSpec((1, TS, D_MODEL), lambda b, j: (b, j, 0)),
        const((DEPTH, D_MODEL)),
        per_layer((D_MODEL, D_IN)),
        per_layer((POOL_WIDTH, POOL_WIDTH)),
        const((DEPTH, POOL_WIDTH)),
        const((HALO_U, POOL_WIDTH)),
        per_layer((CONV_KERNEL, CONV_WIDTH)),
        const((DEPTH, CONV_WIDTH)),
        const((DEPTH, CONV_WIDTH)),
        const((DEPTH, CONV_WIDTH)),
        per_layer((CONV_WIDTH, CONV_WIDTH)),
        pl.BlockSpec(memory_space=pltpu.SMEM),
        const((2, N_KV_HEADS, Q_PER_KV * BLOCK, 2 * BLOCK)),
        per_layer((D_MIX, D_MODEL)),
        const((1, D_MODEL)),
    ]
    scratch = [
        pltpu.VMEM((TS, D_MODEL), _BF16),
        pltpu.VMEM((TS + HALO_U, POOL_WIDTH), _F32),
        pltpu.VMEM((TS + HALO_H, CONV_WIDTH), _F32),
        pltpu.VMEM((TS, POOL_WIDTH), _F32),
        pltpu.VMEM((TS, CONV_WIDTH), _F32),
        pltpu.VMEM((TS, ATTN_WIDTH), _F32),
        pltpu.VMEM((TS, POOL_WIDTH), _BF16),
        pltpu.VMEM((TS, CONV_WIDTH), _BF16),
        pltpu.VMEM((N_KV_HEADS, TS + BLOCK, HEAD_DIM), _BF16),
        pltpu.VMEM((N_KV_HEADS, TS + BLOCK, 2 * LANES), _BF16),
        pltpu.VMEM((N_KV_HEADS, TS + BLOCK, 2 * LANES), _BF16),
        pltpu.VMEM((N_Q_HEADS, TS, HEAD_DIM), _BF16),
        pltpu.VMEM((TS, D_MIX), _BF16),
        pltpu.VMEM((2, N_KV_HEADS, Q_PER_KV * BLOCK, 2 * BLOCK), _F32),
        pltpu.VMEM((D_MODEL, D_IN), _BF16),
        pltpu.VMEM((D_MIX, D_MODEL), _BF16),
    ]
    return pl.pallas_call(
        functools.partial(_layer_body, layer=layer, final=final),
        out_shape=jax.ShapeDtypeStruct((batch, seq, D_MODEL), _F32),
        grid=(batch, n_tiles),
        in_specs=in_specs,
        out_specs=pl.BlockSpec((1, TS, D_MODEL), lambda b, j: (b, j, 0)),
        scratch_shapes=scratch,
        compiler_params=pltpu.CompilerParams(
            dimension_semantics=("arbitrary", "arbitrary"),
            vmem_limit_bytes=VMEM_LIMIT_BYTES,
        ),
        name="hybrid_layer_final" if final else "hybrid_layer",
    )


def kernel(x, ln_g, w_in, pool_w, pool_scale, conv_dw, conv_b, conv_ln_g, conv_ln_b, conv_pw, attn_sinks, w_out, final_g):
    batch, seq, d_model = x.shape
    assert d_model == D_MODEL and seq % TS == 0 and TS % BLOCK == 0
    bias = jnp.asarray(_bias_table())
    inv_cnt = jnp.asarray(_inv_count_table())
    poolw_bd = jnp.stack([jax.scipy.linalg.block_diag(*[pool_w[l, g] for g in range(POOL_GROUPS)])
                          for l in range(DEPTH)])
    params = (ln_g, w_in, poolw_bd, pool_scale, inv_cnt, conv_dw, conv_b, conv_ln_g, conv_ln_b, conv_pw,
              attn_sinks, bias, w_out, final_g.reshape(1, -1))
    for l in range(DEPTH):
        x = _layer_call(batch, seq, l, final=(l == DEPTH - 1))(x, *params)
    return x
```

```python
import functools

import numpy as np
import jax
import jax.numpy as jnp
from jax import lax
from jax.experimental import pallas as pl
from jax.experimental.pallas import tpu as pltpu

D_MODEL = 1024
DEPTH = 2
POOL_WIDTH = 256
POOL_GROUPS = 4
POOL_WINDOWS = (2, 4, 8, 16)
CONV_WIDTH = 256
CONV_KERNEL = 31
HEAD_DIM = 64
N_Q_HEADS = 8
N_KV_HEADS = 2
Q_PER_KV = N_Q_HEADS // N_KV_HEADS
ATTN_WIDTH = N_Q_HEADS * HEAD_DIM
KV_WIDTH = N_KV_HEADS * HEAD_DIM
WINDOW = 128
BLOCK = 128
D_MIX = POOL_WIDTH + CONV_WIDTH + ATTN_WIDTH
D_IN = 2 * POOL_WIDTH + 3 * CONV_WIDTH + 2 * ATTN_WIDTH + 2 * KV_WIDTH
EPS = 1e-6

C_UPOOL = 0
C_GPOOL = C_UPOOL + POOL_WIDTH
C_CA = C_GPOOL + POOL_WIDTH
C_CB = C_CA + CONV_WIDTH
C_GCONV = C_CB + CONV_WIDTH
C_Q = C_GCONV + CONV_WIDTH
C_K = C_Q + ATTN_WIDTH
C_V = C_K + KV_WIDTH
C_GATTN = C_V + KV_WIDTH

SUBLANES = 8
LANES = 128
BF16_ROWS = 16
MXU_COLS = 256
TS = 1024
NB = TS // BLOCK
OUT_BLOCKS = 8
GLU_ROW_GROUPS = (128, 384, 512)
CHUNK = 64
GROUPS = CHUNK // SUBLANES
HALO_U = 16
HALO_H = 32
VMEM_LIMIT_BYTES = 63 * 1024 * 1024
LOG2E = 1.4426950408889634

_F32 = jnp.float32
_BF16 = jnp.bfloat16


def _silu(v):
    return v * jax.nn.sigmoid(v)


def _unrolled(n, body):
    for c in range(n):
        body(c)


def _shift_down(groups, d, row):
    rolled = [pltpu.roll(g, d, axis=0) for g in groups]
    take_prev = row < d
    out = [rolled[0]]
    for i in range(1, len(groups)):
        out.append(jnp.where(take_prev, rolled[i - 1], rolled[i]))
    return out


def _layer_body(x_ref, lng_ref, win_ref, poolw_ref, pscale_ref, invcnt_ref, dw_ref, cb_ref, clg_ref, clb_ref,
                pw_ref, sinks_ref, bias_ref, wout_ref, fg_ref, o_ref,
                hn_ref, ubuf, hbuf, gp_buf, gc_buf, ga_buf, zp_buf, zc_buf, kbuf, ve_buf, vo_buf, qbuf, mix_ref,
                bias_buf, win_bf, wout_bf, *, layer, final):
    j = pl.program_id(1)
    first_tile = j == 0

    @pl.when(jnp.logical_and(pl.program_id(0) == 0, first_tile))
    def _():
        for c in range(D_IN // MXU_COLS):
            cols = slice(c * MXU_COLS, (c + 1) * MXU_COLS)
            win_bf[:, cols] = win_ref[0, :, cols].astype(_BF16)
        for c in range(D_MODEL // MXU_COLS):
            cols = slice(c * MXU_COLS, (c + 1) * MXU_COLS)
            wout_bf[:, cols] = wout_ref[0, :, cols].astype(_BF16)

    row = lax.broadcasted_iota(jnp.int32, (SUBLANES, LANES), 0)
    lane = lax.broadcasted_iota(jnp.int32, (SUBLANES, LANES), 1)
    low_half = lane < HEAD_DIM

    @pl.when(first_tile)
    def _():
        sink_slot = lax.broadcasted_iota(jnp.int32, (BLOCK, 2 * BLOCK), 1) == 0
        for start in range(2):
            for h in range(N_Q_HEADS):
                kvh, g = divmod(h, Q_PER_KV)
                rows = pl.ds(g * BLOCK, BLOCK)
                bias_buf[start, kvh, rows, :] = jnp.where(
                    sink_slot, sinks_ref[layer, h] * LOG2E, bias_ref[start, kvh, rows, :])
        ubuf[0:HALO_U, :] = jnp.zeros((HALO_U, POOL_WIDTH), _F32)
        hbuf[0:HALO_H, :] = jnp.zeros((HALO_H, CONV_WIDTH), _F32)
        kbuf[:, 0:BLOCK, :] = jnp.zeros((N_KV_HEADS, BLOCK, HEAD_DIM), _BF16)
        zeros = jnp.zeros((TS + BLOCK, LANES), _BF16)
        lane_t = lax.broadcasted_iota(jnp.int32, (TS + BLOCK, LANES), 1)
        ones_lo = jnp.where(lane_t < HEAD_DIM, 1.0, 0.0).astype(_BF16)
        ones_hi = jnp.where(lane_t < HEAD_DIM, 0.0, 1.0).astype(_BF16)
        for kvh in range(N_KV_HEADS):
            ve_buf[kvh, :, 0:LANES] = zeros
            vo_buf[kvh, :, 0:LANES] = zeros
            ve_buf[kvh, :, LANES:2 * LANES] = ones_lo
            vo_buf[kvh, :, LANES:2 * LANES] = ones_hi

    @pl.when(jnp.logical_not(first_tile))
    def _():
        ubuf[0:HALO_U, :] = ubuf[TS:TS + HALO_U, :]
        hbuf[0:HALO_H, :] = hbuf[TS:TS + HALO_H, :]
        kbuf[:, 0:BLOCK, :] = kbuf[:, TS:TS + BLOCK, :]
        ve_buf[:, 0:BLOCK, 0:LANES] = ve_buf[:, TS:TS + BLOCK, 0:LANES]
        vo_buf[:, 0:BLOCK, 0:LANES] = vo_buf[:, TS:TS + BLOCK, 0:LANES]

    def norm_chunk(c):
        r = c * CHUNK
        xc = x_ref[0, pl.ds(r, CHUNK), :]
        ms = jnp.mean(xc * xc, axis=-1, keepdims=True)
        y = xc * lax.rsqrt(ms + EPS) * lng_ref[layer:layer + 1, :]
        hn_ref[pl.ds(r, CHUNK), :] = y.astype(_BF16)

    _unrolled(TS // CHUNK, norm_chunk)

    def proj(c0, width, r0=0, rows=TS):
        w = win_bf[:, c0:c0 + width]
        return jnp.dot(hn_ref[pl.ds(r0, rows), :], w, preferred_element_type=_F32)

    r0 = 0
    for n_rows in GLU_ROW_GROUPS:
        rows = pl.ds(HALO_H + r0, n_rows)
        hbuf[rows, :] = proj(C_CA, CONV_WIDTH, r0, n_rows)
        hbuf[rows, :] = hbuf[rows, :] * jax.nn.sigmoid(proj(C_CB, CONV_WIDTH, r0, n_rows))
        r0 += n_rows

    kv = proj(C_K, 2 * KV_WIDTH)
    keys = kv[:, 0:KV_WIDTH]
    vals = kv[:, KV_WIDTH:2 * KV_WIDTH]
    vals_swapped = pltpu.roll(vals, HEAD_DIM, axis=1)
    lo = lax.broadcasted_iota(jnp.int32, (TS, LANES), 1) < HEAD_DIM
    for kvh in range(N_KV_HEADS):
        kbuf[kvh, BLOCK:BLOCK + TS, :] = keys[:, kvh * HEAD_DIM:(kvh + 1) * HEAD_DIM].astype(_BF16)
        own_lo, own_hi = (vals, vals_swapped) if kvh == 0 else (vals_swapped, vals)
        ve_buf[kvh, BLOCK:BLOCK + TS, 0:LANES] = jnp.where(lo, own_lo, 0.0).astype(_BF16)
        vo_buf[kvh, BLOCK:BLOCK + TS, 0:LANES] = jnp.where(lo, 0.0, own_hi).astype(_BF16)

    heads_per_group = MXU_COLS // HEAD_DIM
    for half in range(ATTN_WIDTH // MXU_COLS):
        qh = proj(C_Q + half * MXU_COLS, MXU_COLS) * (HEAD_DIM ** -0.5 * LOG2E)
        for i in range(heads_per_group):
            qbuf[half * heads_per_group + i, :, :] = qh[:, i * HEAD_DIM:(i + 1) * HEAD_DIM].astype(_BF16)
    for half in range(ATTN_WIDTH // MXU_COLS):
        ga_buf[:, half * MXU_COLS:(half + 1) * MXU_COLS] = _silu(proj(C_GATTN + half * MXU_COLS, MXU_COLS))

    ubuf[HALO_U:HALO_U + TS, :] = proj(C_UPOOL, POOL_WIDTH)
    gp_buf[...] = _silu(proj(C_GPOOL, POOL_WIDTH))
    gc_buf[...] = _silu(proj(C_GCONV, CONV_WIDTH))

    inv_w = [jnp.where(low_half, 1.0 / POOL_WINDOWS[0], 1.0 / POOL_WINDOWS[1]).astype(_F32),
             jnp.where(low_half, 1.0 / POOL_WINDOWS[2], 1.0 / POOL_WINDOWS[3]).astype(_F32)]

    def pool_chunk(c):
        r = c * CHUNK
        n_in = GROUPS + HALO_U // SUBLANES
        for col in range(2):
            lanes = slice(col * LANES, (col + 1) * LANES)
            u = [ubuf[pl.ds(r + SUBLANES * i, SUBLANES), lanes] for i in range(n_in)]
            s2 = [a + b for a, b in zip(u, _shift_down(u, 1, row))]
            s4 = [a + b for a, b in zip(s2, _shift_down(s2, 2, row))]
            if col == 0:
                wide, narrow = s4, s2
            else:
                s8 = [a + b for a, b in zip(s4, _shift_down(s4, 4, row))]
                s16 = [s8[0]] + [s8[i] + s8[i - 1] for i in range(1, n_in)]
                wide, narrow = s16, s8
            pieces = []
            for g in range(GROUPS):
                i = g + HALO_U // SUBLANES
                inv = inv_w[col]
                if c == 0 and g < HALO_U // SUBLANES:
                    tab = invcnt_ref[SUBLANES * g:SUBLANES * (g + 1), lanes]
                    inv = jnp.where(first_tile, tab, inv)
                pooled = jnp.where(low_half, narrow[i], wide[i]) * inv
                pieces.append(pooled - u[i])
            zp_buf[pl.ds(r, CHUNK), lanes] = jnp.concatenate(pieces, axis=0).astype(_BF16)

    taps_by_b = [[] for _ in range(SUBLANES)]
    for k in range(CONV_KERNEL):
        a, b = divmod(k + HALO_H - (CONV_KERNEL - 1), SUBLANES)
        taps_by_b[b].append((a, k))
    n_h = GROUPS + HALO_H // SUBLANES

    def conv_chunk(c):
        r = c * CHUNK
        outs = []
        for col in range(2):
            lanes = slice(col * LANES, (col + 1) * LANES)
            h = [hbuf[pl.ds(r + SUBLANES * m, SUBLANES), lanes] for m in range(n_h)]
            acc = None
            for b in range(SUBLANES):
                n_p = GROUPS if b == 0 else GROUPS + 1
                part = []
                for m in range(n_p):
                    p = None
                    for a, k in taps_by_b[b]:
                        term = h[m + a] * dw_ref[0, k:k + 1, lanes]
                        p = term if p is None else p + term
                    part.append(p)
                if b == 0:
                    acc = part
                else:
                    rolled = [pltpu.roll(p, SUBLANES - b, axis=0) for p in part]
                    keep = row < (SUBLANES - b)
                    acc = [acc[g] + jnp.where(keep, rolled[g], rolled[g + 1]) for g in range(GROUPS)]
            outs.append(jnp.concatenate(acc, axis=0))
        conv = jnp.concatenate(outs, axis=1) + cb_ref[layer:layer + 1, :]
        mu = jnp.mean(conv, axis=-1, keepdims=True)
        cen = conv - mu
        var = jnp.mean(cen * cen, axis=-1, keepdims=True)
        z = cen * lax.rsqrt(var + EPS) * clg_ref[layer:layer + 1, :] + clb_ref[layer:layer + 1, :]
        zc_buf[pl.ds(r, CHUNK), :] = _silu(z).astype(_BF16)

    _unrolled(TS // CHUNK, conv_chunk)
    _unrolled(TS // CHUNK, pool_chunk)

    n_local = POOL_WIDTH + CONV_WIDTH

    top_rows = lax.broadcasted_iota(jnp.int32, (BF16_ROWS, 2 * LANES), 0)
    top_lanes = lax.broadcasted_iota(jnp.int32, (BF16_ROWS, 2 * LANES), 1)
    sink_value_slot = jnp.logical_and(top_rows == 0, top_lanes < LANES)
    sink_key_slot = lax.broadcasted_iota(jnp.int32, (BF16_ROWS, HEAD_DIM), 0) == 0

    def window_keys(kvh, r):
        top = kbuf[kvh, pl.ds(r, BF16_ROWS), :].astype(_F32)
        top = jnp.where(sink_key_slot, 0.0, top).astype(_BF16)
        return jnp.concatenate([top, kbuf[kvh, pl.ds(r + BF16_ROWS, 2 * BLOCK - BF16_ROWS), :]], axis=0)

    def scores(n):
        r = n * BLOCK
        seq_start = first_tile.astype(jnp.int32) if n == 0 else 0
        out = []
        for kvh in range(N_KV_HEADS):
            k2 = window_keys(kvh, r)
            q4 = jnp.concatenate([qbuf[kvh * Q_PER_KV + g, pl.ds(r, BLOCK), :] for g in range(Q_PER_KV)], axis=0)
            s = lax.dot_general(q4, k2, (((1,), (1,)), ((), ())), preferred_element_type=_F32)
            out.append(s + bias_buf[seq_start, kvh])
        return out

    def window_values(buf, kvh, r):
        top = buf[kvh, pl.ds(r, BF16_ROWS), :].astype(_F32)
        top = jnp.where(sink_value_slot, 0.0, top).astype(_BF16)
        return jnp.concatenate([top, buf[kvh, pl.ds(r + BF16_ROWS, 2 * BLOCK - BF16_ROWS), :]], axis=0)

    def softmax_pv(n, s_list):
        r = n * BLOCK
        for kvh in range(N_KV_HEADS):
            probs = []
            for g in range(Q_PER_KV):
                sg = s_list[kvh][g * BLOCK:(g + 1) * BLOCK]
                m = jnp.max(sg, axis=-1, keepdims=True)
                probs.append(jnp.exp2(sg - m).astype(_BF16))
            p_even = jnp.concatenate([probs[0], probs[2]], axis=0)
            p_odd = jnp.concatenate([probs[1], probs[3]], axis=0)
            o = (jnp.dot(p_even, window_values(ve_buf, kvh, r), preferred_element_type=_F32)
                 + jnp.dot(p_odd, window_values(vo_buf, kvh, r), preferred_element_type=_F32))
            res = o[:, 0:LANES] / o[:, LANES:2 * LANES]
            for pair in range(2):
                c0 = (kvh * 2 + pair) * LANES
                gate = ga_buf[pl.ds(r, BLOCK), c0:c0 + LANES]
                mix_ref[pl.ds(r, BLOCK), n_local + c0:n_local + c0 + LANES] = (
                    res[pair * BLOCK:(pair + 1) * BLOCK] * gate).astype(_BF16)

    def out_attn(r0, rows):
        for q in range(D_MODEL // MXU_COLS):
            cols = slice(q * MXU_COLS, (q + 1) * MXU_COLS)
            o_ref[0, pl.ds(r0, rows), cols] = x_ref[0, pl.ds(r0, rows), cols] + jnp.dot(
                mix_ref[pl.ds(r0, rows), n_local:D_MIX], wout_bf[n_local:D_MIX, cols],
                preferred_element_type=_F32)

    s_cur = scores(0)
    for n in range(NB):
        s_next = scores(n + 1) if n + 1 < NB else None
        if n >= OUT_BLOCKS and n % OUT_BLOCKS == 0:
            out_attn((n - OUT_BLOCKS) * BLOCK, OUT_BLOCKS * BLOCK)
        softmax_pv(n, s_cur)
        s_cur = s_next
    out_attn((NB - OUT_BLOCKS) * BLOCK, OUT_BLOCKS * BLOCK)

    y_pool = jnp.dot(zp_buf[...], poolw_ref[0].astype(_BF16), preferred_element_type=_F32)
    y_pool = y_pool * pscale_ref[layer:layer + 1, :]
    mix_ref[:, 0:POOL_WIDTH] = (y_pool * gp_buf[...]).astype(_BF16)
    y_conv = jnp.dot(zc_buf[...], pw_ref[0].astype(_BF16), preferred_element_type=_F32)
    mix_ref[:, POOL_WIDTH:POOL_WIDTH + CONV_WIDTH] = (y_conv * gc_buf[...]).astype(_BF16)

    def final_chunk(c):
        r = c * CHUNK
        xc = o_ref[0, pl.ds(r, CHUNK), :]
        ms = jnp.mean(xc * xc, axis=-1, keepdims=True)
        o_ref[0, pl.ds(r, CHUNK), :] = xc * lax.rsqrt(ms + EPS) * fg_ref[...]

    half_rows = TS
    for r0 in range(0, TS, half_rows):
        rows = pl.ds(r0, half_rows)
        for q in range(D_MODEL // MXU_COLS):
            cols = slice(q * MXU_COLS, (q + 1) * MXU_COLS)
            o_ref[0, rows, cols] = o_ref[0, rows, cols] + jnp.dot(
                mix_ref[rows, 0:n_local], wout_bf[0:n_local, cols], preferred_element_type=_F32)
        if final:
            for c in range(r0 // CHUNK, (r0 + half_rows) // CHUNK):
                final_chunk(c)


def _bias_table():
    i = np.arange(BLOCK)[:, None]
    jj = np.arange(2 * BLOCK)[None, :]
    dist = BLOCK + i - jj
    in_band = (dist >= 0) & (dist < WINDOW)
    slopes = np.asarray([2.0 ** (-8.0 * (h + 1) / N_Q_HEADS) for h in range(N_Q_HEADS)], dtype=np.float32)
    tab = np.empty((2, N_KV_HEADS, Q_PER_KV * BLOCK, 2 * BLOCK), np.float32)
    for start in range(2):
        valid = in_band & ~((jj < BLOCK) & (start == 1))
        for h in range(N_Q_HEADS):
            bias = (-slopes[h] * LOG2E) * dist.astype(np.float64)
            kvh, g = divmod(h, Q_PER_KV)
            tab[start, kvh, g * BLOCK:(g + 1) * BLOCK] = np.where(valid, bias, -1e30).astype(np.float32)
    return tab


def _inv_count_table():
    t = np.arange(HALO_U, dtype=np.float32)[:, None]
    gw = POOL_WIDTH // POOL_GROUPS
    wnd = np.repeat(np.asarray(POOL_WINDOWS, np.float32), gw)[None, :]
    return (1.0 / np.minimum(t + 1.0, wnd)).astype(np.float32)


def _layer_call(batch, seq, layer, final):
    n_tiles = seq // TS
    const = lambda shape: pl.BlockSpec(shape, lambda b, j: (0,) * len(shape), pipeline_mode=pl.Buffered(1))
    per_layer = lambda shape: pl.BlockSpec((1,) + shape, lambda b, j: (layer,) + (0,) * len(shape),
                                           pipeline_mode=pl.Buffered(1))
    in_specs = [
        pl.BlockSpec((1, TS, D_MODEL), lambda b, j: (b, j, 0)),
        const((DEPTH, D_MODEL)),
        per_layer((D_MODEL, D_IN)),
        per_layer((POOL_WIDTH, POOL_WIDTH)),
        const((DEPTH, POOL_WIDTH)),
        const((HALO_U, POOL_WIDTH)),
        per_layer((CONV_KERNEL, CONV_WIDTH)),
        const((DEPTH, CONV_WIDTH)),
        const((DEPTH, CONV_WIDTH)),
        const((DEPTH, CONV_WIDTH)),
        per_layer((CONV_WIDTH, CONV_WIDTH)),
        pl.BlockSpec(memory_space=pltpu.SMEM),
        const((2, N_KV_HEADS, Q_PER_KV * BLOCK, 2 * BLOCK)),
        per_layer((D_MIX, D_MODEL)),
        const((1, D_MODEL)),
    ]
    scratch = [
        pltpu.VMEM((TS, D_MODEL), _BF16),
        pltpu.VMEM((TS + HALO_U, POOL_WIDTH), _F32),
        pltpu.VMEM((TS + HALO_H, CONV_WIDTH), _F32),
        pltpu.VMEM((TS, POOL_WIDTH), _F32),
        pltpu.VMEM((TS, CONV_WIDTH), _F32),
        pltpu.VMEM((TS, ATTN_WIDTH), _F32),
        pltpu.VMEM((TS, POOL_WIDTH), _BF16),
        pltpu.VMEM((TS, CONV_WIDTH), _BF16),
        pltpu.VMEM((N_KV_HEADS, TS + BLOCK, HEAD_DIM), _BF16),
        pltpu.VMEM((N_KV_HEADS, TS + BLOCK, 2 * LANES), _BF16),
        pltpu.VMEM((N_KV_HEADS, TS + BLOCK, 2 * LANES), _BF16),
        pltpu.VMEM((N_Q_HEADS, TS, HEAD_DIM), _BF16),
        pltpu.VMEM((TS, D_MIX), _BF16),
        pltpu.VMEM((2, N_KV_HEADS, Q_PER_KV * BLOCK, 2 * BLOCK), _F32),
        pltpu.VMEM((D_MODEL, D_IN), _BF16),
        pltpu.VMEM((D_MIX, D_MODEL), _BF16),
    ]
    return pl.pallas_call(
        functools.partial(_layer_body, layer=layer, final=final),
        out_shape=jax.ShapeDtypeStruct((batch, seq, D_MODEL), _F32),
        grid=(batch, n_tiles),
        in_specs=in_specs,
        out_specs=pl.BlockSpec((1, TS, D_MODEL), lambda b, j: (b, j, 0)),
        scratch_shapes=scratch,
        compiler_params=pltpu.CompilerParams(
            dimension_semantics=("arbitrary", "arbitrary"),
            vmem_limit_bytes=VMEM_LIMIT_BYTES,
        ),
        name="hybrid_layer_final" if final else "hybrid_layer",
    )


def kernel(x, ln_g, w_in, pool_w, pool_scale, conv_dw, conv_b, conv_ln_g, conv_ln_b, conv_pw, attn_sinks, w_out, final_g):
    batch, seq, d_model = x.shape
    assert d_model == D_MODEL and seq % TS == 0 and TS % BLOCK == 0
    bias = jnp.asarray(_bias_table())
    inv_cnt = jnp.asarray(_inv_count_table())
    poolw_bd = jnp.stack([jax.scipy.linalg.block_diag(*[pool_w[l, g] for g in range(POOL_GROUPS)])
                          for l in range(DEPTH)])
    params = (ln_g, w_in, poolw_bd, pool_scale, inv_cnt, conv_dw, conv_b, conv_ln_g, conv_ln_b, conv_pw,
              attn_sinks, bias, w_out, final_g.reshape(1, -1))
    for l in range(DEPTH):
        x = _layer_call(batch, seq, l, final=(l == DEPTH - 1))(x, *params)
    return x
```

```python
import functools

import numpy as np
import jax
import jax.numpy as jnp
from jax import lax
from jax.experimental import pallas as pl
from jax.experimental.pallas import tpu as pltpu

D_MODEL = 1024
DEPTH = 2
POOL_WIDTH = 256
POOL_GROUPS = 4
POOL_WINDOWS = (2, 4, 8, 16)
CONV_WIDTH = 256
CONV_KERNEL = 31
HEAD_DIM = 64
N_Q_HEADS = 8
N_KV_HEADS = 2
Q_PER_KV = N_Q_HEADS // N_KV_HEADS
ATTN_WIDTH = N_Q_HEADS * HEAD_DIM
KV_WIDTH = N_KV_HEADS * HEAD_DIM
WINDOW = 128
BLOCK = 128
D_MIX = POOL_WIDTH + CONV_WIDTH + ATTN_WIDTH
D_IN = 2 * POOL_WIDTH + 3 * CONV_WIDTH + 2 * ATTN_WIDTH + 2 * KV_WIDTH
EPS = 1e-6

C_UPOOL = 0
C_GPOOL = C_UPOOL + POOL_WIDTH
C_CA = C_GPOOL + POOL_WIDTH
C_CB = C_CA + CONV_WIDTH
C_GCONV = C_CB + CONV_WIDTH
C_Q = C_GCONV + CONV_WIDTH
C_K = C_Q + ATTN_WIDTH
C_V = C_K + KV_WIDTH
C_GATTN = C_V + KV_WIDTH

SUBLANES = 8
LANES = 128
BF16_ROWS = 16
MXU_COLS = 256
TS = 1024
NB = TS // BLOCK
OUT_BLOCKS = 4
GLU_ROW_GROUPS = (128, 128, 256, 512)
CHUNK = 64
GROUPS = CHUNK // SUBLANES
HALO_U = 16
HALO_H = 32
VMEM_LIMIT_BYTES = 63 * 1024 * 1024
LOG2E = 1.4426950408889634

_F32 = jnp.float32
_BF16 = jnp.bfloat16


def _silu(v):
    return v * jax.nn.sigmoid(v)


def _unrolled(n, body):
    for c in range(n):
        body(c)


def _shift_down(groups, d, row):
    rolled = [pltpu.roll(g, d, axis=0) for g in groups]
    take_prev = row < d
    out = [rolled[0]]
    for i in range(1, len(groups)):
        out.append(jnp.where(take_prev, rolled[i - 1], rolled[i]))
    return out


def _layer_body(x_ref, lng_ref, win_ref, poolw_ref, pscale_ref, invcnt_ref, dw_ref, cb_ref, clg_ref, clb_ref,
                pw_ref, sinks_ref, bias_ref, wout_ref, fg_ref, o_ref,
                hn_ref, ubuf, hbuf, gp_buf, gc_buf, ga_buf, zp_buf, zc_buf, kbuf, ve_buf, vo_buf, qbuf, mix_ref,
                bias_buf, win_bf, wout_bf, *, layer, final):
    j = pl.program_id(1)
    first_tile = j == 0

    @pl.when(jnp.logical_and(pl.program_id(0) == 0, first_tile))
    def _():
        for c in range(D_IN // MXU_COLS):
            cols = slice(c * MXU_COLS, (c + 1) * MXU_COLS)
            win_bf[:, cols] = win_ref[0, :, cols].astype(_BF16)
        for c in range(D_MODEL // MXU_COLS):
            cols = slice(c * MXU_COLS, (c + 1) * MXU_COLS)
            wout_bf[:, cols] = wout_ref[0, :, cols].astype(_BF16)

    row = lax.broadcasted_iota(jnp.int32, (SUBLANES, LANES), 0)
    lane = lax.broadcasted_iota(jnp.int32, (SUBLANES, LANES), 1)
    low_half = lane < HEAD_DIM

    @pl.when(first_tile)
    def _():
        sink_slot = lax.broadcasted_iota(jnp.int32, (BLOCK, 2 * BLOCK), 1) == 0
        for start in range(2):
            for h in range(N_Q_HEADS):
                kvh, g = divmod(h, Q_PER_KV)
                rows = pl.ds(g * BLOCK, BLOCK)
                bias_buf[start, kvh, rows, :] = jnp.where(
                    sink_slot, sinks_ref[layer, h] * LOG2E, bias_ref[start, kvh, rows, :])
        ubuf[0:HALO_U, :] = jnp.zeros((HALO_U, POOL_WIDTH), _F32)
        hbuf[0:HALO_H, :] = jnp.zeros((HALO_H, CONV_WIDTH), _F32)
        kbuf[:, 0:BLOCK, :] = jnp.zeros((N_KV_HEADS, BLOCK, HEAD_DIM), _BF16)
        zeros = jnp.zeros((TS + BLOCK, LANES), _BF16)
        lane_t = lax.broadcasted_iota(jnp.int32, (TS + BLOCK, LANES), 1)
        ones_lo = jnp.where(lane_t < HEAD_DIM, 1.0, 0.0).astype(_BF16)
        ones_hi = jnp.where(lane_t < HEAD_DIM, 0.0, 1.0).astype(_BF16)
        for kvh in range(N_KV_HEADS):
            ve_buf[kvh, :, 0:LANES] = zeros
            vo_buf[kvh, :, 0:LANES] = zeros
            ve_buf[kvh, :, LANES:2 * LANES] = ones_lo
            vo_buf[kvh, :, LANES:2 * LANES] = ones_hi

    @pl.when(jnp.logical_not(first_tile))
    def _():
        ubuf[0:HALO_U, :] = ubuf[TS:TS + HALO_U, :]
        hbuf[0:HALO_H, :] = hbuf[TS:TS + HALO_H, :]
        kbuf[:, 0:BLOCK, :] = kbuf[:, TS:TS + BLOCK, :]
        ve_buf[:, 0:BLOCK, 0:LANES] = ve_buf[:, TS:TS + BLOCK, 0:LANES]
        vo_buf[:, 0:BLOCK, 0:LANES] = vo_buf[:, TS:TS + BLOCK, 0:LANES]

    def norm_chunk(c):
        r = c * CHUNK
        xc = x_ref[0, pl.ds(r, CHUNK), :]
        ms = jnp.mean(xc * xc, axis=-1, keepdims=True)
        y = xc * lax.rsqrt(ms + EPS) * lng_ref[layer:layer + 1, :]
        hn_ref[pl.ds(r, CHUNK), :] = y.astype(_BF16)

    _unrolled(TS // CHUNK, norm_chunk)

    def proj(c0, width, r0=0, rows=TS):
        w = win_bf[:, c0:c0 + width]
        return jnp.dot(hn_ref[pl.ds(r0, rows), :], w, preferred_element_type=_F32)

    r0 = 0
    for n_rows in GLU_ROW_GROUPS:
        rows = pl.ds(HALO_H + r0, n_rows)
        hbuf[rows, :] = proj(C_CA, CONV_WIDTH, r0, n_rows)
        hbuf[rows, :] = hbuf[rows, :] * jax.nn.sigmoid(proj(C_CB, CONV_WIDTH, r0, n_rows))
        r0 += n_rows

    kv = proj(C_K, 2 * KV_WIDTH)
    keys = kv[:, 0:KV_WIDTH]
    vals = kv[:, KV_WIDTH:2 * KV_WIDTH]
    vals_swapped = pltpu.roll(vals, HEAD_DIM, axis=1)
    lo = lax.broadcasted_iota(jnp.int32, (TS, LANES), 1) < HEAD_DIM
    for kvh in range(N_KV_HEADS):
        kbuf[kvh, BLOCK:BLOCK + TS, :] = keys[:, kvh * HEAD_DIM:(kvh + 1) * HEAD_DIM].astype(_BF16)
        own_lo, own_hi = (vals, vals_swapped) if kvh == 0 else (vals_swapped, vals)
        ve_buf[kvh, BLOCK:BLOCK + TS, 0:LANES] = jnp.where(lo, own_lo, 0.0).astype(_BF16)
        vo_buf[kvh, BLOCK:BLOCK + TS, 0:LANES] = jnp.where(lo, 0.0, own_hi).astype(_BF16)

    heads_per_group = MXU_COLS // HEAD_DIM
    for half in range(ATTN_WIDTH // MXU_COLS):
        qh = proj(C_Q + half * MXU_COLS, MXU_COLS) * (HEAD_DIM ** -0.5 * LOG2E)
        for i in range(heads_per_group):
            qbuf[half * heads_per_group + i, :, :] = qh[:, i * HEAD_DIM:(i + 1) * HEAD_DIM].astype(_BF16)
    for half in range(ATTN_WIDTH // MXU_COLS):
        ga_buf[:, half * MXU_COLS:(half + 1) * MXU_COLS] = _silu(proj(C_GATTN + half * MXU_COLS, MXU_COLS))

    ubuf[HALO_U:HALO_U + TS, :] = proj(C_UPOOL, POOL_WIDTH)
    gp_buf[...] = _silu(proj(C_GPOOL, POOL_WIDTH))
    gc_buf[...] = _silu(proj(C_GCONV, CONV_WIDTH))

    inv_w = [jnp.where(low_half, 1.0 / POOL_WINDOWS[0], 1.0 / POOL_WINDOWS[1]).astype(_F32),
             jnp.where(low_half, 1.0 / POOL_WINDOWS[2], 1.0 / POOL_WINDOWS[3]).astype(_F32)]

    def pool_chunk(c):
        r = c * CHUNK
        n_in = GROUPS + HALO_U // SUBLANES
        for col in range(2):
            lanes = slice(col * LANES, (col + 1) * LANES)
            u = [ubuf[pl.ds(r + SUBLANES * i, SUBLANES), lanes] for i in range(n_in)]
            s2 = [a + b for a, b in zip(u, _shift_down(u, 1, row))]
            s4 = [a + b for a, b in zip(s2, _shift_down(s2, 2, row))]
            if col == 0:
                wide, narrow = s4, s2
            else:
                s8 = [a + b for a, b in zip(s4, _shift_down(s4, 4, row))]
                s16 = [s8[0]] + [s8[i] + s8[i - 1] for i in range(1, n_in)]
                wide, narrow = s16, s8
            pieces = []
            for g in range(GROUPS):
                i = g + HALO_U // SUBLANES
                inv = inv_w[col]
                if c == 0 and g < HALO_U // SUBLANES:
                    tab = invcnt_ref[SUBLANES * g:SUBLANES * (g + 1), lanes]
                    inv = jnp.where(first_tile, tab, inv)
                pooled = jnp.where(low_half, narrow[i], wide[i]) * inv
                pieces.append(pooled - u[i])
            zp_buf[pl.ds(r, CHUNK), lanes] = jnp.concatenate(pieces, axis=0).astype(_BF16)

    taps_by_b = [[] for _ in range(SUBLANES)]
    for k in range(CONV_KERNEL):
        a, b = divmod(k + HALO_H - (CONV_KERNEL - 1), SUBLANES)
        taps_by_b[b].append((a, k))
    n_h = GROUPS + HALO_H // SUBLANES

    def conv_chunk(c):
        r = c * CHUNK
        outs = []
        for col in range(2):
            lanes = slice(col * LANES, (col + 1) * LANES)
            h = [hbuf[pl.ds(r + SUBLANES * m, SUBLANES), lanes] for m in range(n_h)]
            acc = None
            for b in range(SUBLANES):
                n_p = GROUPS if b == 0 else GROUPS + 1
                part = []
                for m in range(n_p):
                    p = None
                    for a, k in taps_by_b[b]:
                        term = h[m + a] * dw_ref[0, k:k + 1, lanes]
                        p = term if p is None else p + term
                    part.append(p)
                if b == 0:
                    acc = part
                else:
                    rolled = [pltpu.roll(p, SUBLANES - b, axis=0) for p in part]
                    keep = row < (SUBLANES - b)
                    acc = [acc[g] + jnp.where(keep, rolled[g], rolled[g + 1]) for g in range(GROUPS)]
            outs.append(jnp.concatenate(acc, axis=0))
        conv = jnp.concatenate(outs, axis=1) + cb_ref[layer:layer + 1, :]
        mu = jnp.mean(conv, axis=-1, keepdims=True)
        cen = conv - mu
        var = jnp.mean(cen * cen, axis=-1, keepdims=True)
        z = cen * lax.rsqrt(var + EPS) * clg_ref[layer:layer + 1, :] + clb_ref[layer:layer + 1, :]
        zc_buf[pl.ds(r, CHUNK), :] = _silu(z).astype(_BF16)

    _unrolled(TS // CHUNK, conv_chunk)
    _unrolled(TS // CHUNK, pool_chunk)

    n_local = POOL_WIDTH + CONV_WIDTH

    top_rows = lax.broadcasted_iota(jnp.int32, (BF16_ROWS, 2 * LANES), 0)
    top_lanes = lax.broadcasted_iota(jnp.int32, (BF16_ROWS, 2 * LANES), 1)
    sink_value_slot = jnp.logical_and(top_rows == 0, top_lanes < LANES)
    sink_key_slot = lax.broadcasted_iota(jnp.int32, (BF16_ROWS, HEAD_DIM), 0) == 0

    def window_keys(kvh, r):
        top = kbuf[kvh, pl.ds(r, BF16_ROWS), :].astype(_F32)
        top = jnp.where(sink_key_slot, 0.0, top).astype(_BF16)
        return jnp.concatenate([top, kbuf[kvh, pl.ds(r + BF16_ROWS, 2 * BLOCK - BF16_ROWS), :]], axis=0)

    def scores(n):
        r = n * BLOCK
        seq_start = first_tile.astype(jnp.int32) if n == 0 else 0
        out = []
        for kvh in range(N_KV_HEADS):
            k2 = window_keys(kvh, r)
            q4 = jnp.concatenate([qbuf[kvh * Q_PER_KV + g, pl.ds(r, BLOCK), :] for g in range(Q_PER_KV)], axis=0)
            s = lax.dot_general(q4, k2, (((1,), (1,)), ((), ())), preferred_element_type=_F32)
            out.append(s + bias_buf[seq_start, kvh])
        return out

    def window_values(buf, kvh, r):
        top = buf[kvh, pl.ds(r, BF16_ROWS), :].astype(_F32)
        top = jnp.where(sink_value_slot, 0.0, top).astype(_BF16)
        return jnp.concatenate([top, buf[kvh, pl.ds(r + BF16_ROWS, 2 * BLOCK - BF16_ROWS), :]], axis=0)

    def softmax_pv(n, s_list):
        r = n * BLOCK
        for kvh in range(N_KV_HEADS):
            probs = []
            for g in range(Q_PER_KV):
                sg = s_list[kvh][g * BLOCK:(g + 1) * BLOCK]
                m = jnp.max(sg, axis=-1, keepdims=True)
                probs.append(jnp.exp2(sg - m).astype(_BF16))
            p_even = jnp.concatenate([probs[0], probs[2]], axis=0)
            p_odd = jnp.concatenate([probs[1], probs[3]], axis=0)
            o = (jnp.dot(p_even, window_values(ve_buf, kvh, r), preferred_element_type=_F32)
                 + jnp.dot(p_odd, window_values(vo_buf, kvh, r), preferred_element_type=_F32))
            res = o[:, 0:LANES] / o[:, LANES:2 * LANES]
            for pair in range(2):
                c0 = (kvh * 2 + pair) * LANES
                gate = ga_buf[pl.ds(r, BLOCK), c0:c0 + LANES]
                mix_ref[pl.ds(r, BLOCK), n_local + c0:n_local + c0 + LANES] = (
                    res[pair * BLOCK:(pair + 1) * BLOCK] * gate).astype(_BF16)

    def out_attn(r0, rows):
        for q in range(D_MODEL // MXU_COLS):
            cols = slice(q * MXU_COLS, (q + 1) * MXU_COLS)
            o_ref[0, pl.ds(r0, rows), cols] = x_ref[0, pl.ds(r0, rows), cols] + jnp.dot(
                mix_ref[pl.ds(r0, rows), n_local:D_MIX], wout_bf[n_local:D_MIX, cols],
                preferred_element_type=_F32)

    s_cur = scores(0)
    for n in range(NB):
        s_next = scores(n + 1) if n + 1 < NB else None
        if n >= OUT_BLOCKS and n % OUT_BLOCKS == 0:
            out_attn((n - OUT_BLOCKS) * BLOCK, OUT_BLOCKS * BLOCK)
        softmax_pv(n, s_cur)
        s_cur = s_next
    out_attn((NB - OUT_BLOCKS) * BLOCK, OUT_BLOCKS * BLOCK)

    y_pool = jnp.dot(zp_buf[...], poolw_ref[0].astype(_BF16), preferred_element_type=_F32)
    y_pool = y_pool * pscale_ref[layer:layer + 1, :]
    mix_ref[:, 0:POOL_WIDTH] = (y_pool * gp_buf[...]).astype(_BF16)
    y_conv = jnp.dot(zc_buf[...], pw_ref[0].astype(_BF16), preferred_element_type=_F32)
    mix_ref[:, POOL_WIDTH:POOL_WIDTH + CONV_WIDTH] = (y_conv * gc_buf[...]).astype(_BF16)

    def final_chunk(c):
        r = c * CHUNK
        xc = o_ref[0, pl.ds(r, CHUNK), :]
        ms = jnp.mean(xc * xc, axis=-1, keepdims=True)
        o_ref[0, pl.ds(r, CHUNK), :] = xc * lax.rsqrt(ms + EPS) * fg_ref[...]

    half_rows = TS
    for r0 in range(0, TS, half_rows):
        rows = pl.ds(r0, half_rows)
        for q in range(D_MODEL // MXU_COLS):
            cols = slice(q * MXU_COLS, (q + 1) * MXU_COLS)
            o_ref[0, rows, cols] = o_ref[0, rows, cols] + jnp.dot(
                mix_ref[rows, 0:n_local], wout_bf[0:n_local, cols], preferred_element_type=_F32)
        if final:
            for c in range(r0 // CHUNK, (r0 + half_rows) // CHUNK):
                final_chunk(c)


def _bias_table():
    i = np.arange(BLOCK)[:, None]
    jj = np.arange(2 * BLOCK)[None, :]
    dist = BLOCK + i - jj
    in_band = (dist >= 0) & (dist < WINDOW)
    slopes = np.asarray([2.0 ** (-8.0 * (h + 1) / N_Q_HEADS) for h in range(N_Q_HEADS)], dtype=np.float32)
    tab = np.empty((2, N_KV_HEADS, Q_PER_KV * BLOCK, 2 * BLOCK), np.float32)
    for start in range(2):
        valid = in_band & ~((jj < BLOCK) & (start == 1))
        for h in range(N_Q_HEADS):
            bias = (-slopes[h] * LOG2E) * dist.astype(np.float64)
            kvh, g = divmod(h, Q_PER_KV)
            tab[start, kvh, g * BLOCK:(g + 1) * BLOCK] = np.where(valid, bias, -1e30).astype(np.float32)
    return tab


def _inv_count_table():
    t = np.arange(HALO_U, dtype=np.float32)[:, None]
    gw = POOL_WIDTH // POOL_GROUPS
    wnd = np.repeat(np.asarray(POOL_WINDOWS, np.float32), gw)[None, :]
    return (1.0 / np.minimum(t + 1.0, wnd)).astype(np.float32)


def _layer_call(batch, seq, layer, final):
    n_tiles = seq // TS
    const = lambda shape: pl.BlockSpec(shape, lambda b, j: (0,) * len(shape), pipeline_mode=pl.Buffered(1))
    per_layer = lambda shape: pl.BlockSpec((1,) + shape, lambda b, j: (layer,) + (0,) * len(shape),
                                           pipeline_mode=pl.Buffered(1))
    in_specs = [
        pl.BlockSpec((1, TS, D_MODEL), lambda b, j: (b, j, 0)),
        const((DEPTH, D_MODEL)),
        per_layer((D_MODEL, D_IN)),
        per_layer((POOL_WIDTH, POOL_WIDTH)),
        const((DEPTH, POOL_WIDTH)),
        const((HALO_U, POOL_WIDTH)),
        per_layer((CONV_KERNEL, CONV_WIDTH)),
        const((DEPTH, CONV_WIDTH)),
        const((DEPTH, CONV_WIDTH)),
        const((DEPTH, CONV_WIDTH)),
        per_layer((CONV_WIDTH, CONV_WIDTH)),
        pl.BlockSpec(memory_space=pltpu.SMEM),
        const((2, N_KV_HEADS, Q_PER_KV * BLOCK, 2 * BLOCK)),
        per_layer((D_MIX, D_MODEL)),
        const((1, D_MODEL)),
    ]
    scratch = [
        pltpu.VMEM((TS, D_MODEL), _BF16),
        pltpu.VMEM((TS + HALO_U, POOL_WIDTH), _F32),
        pltpu.VMEM((TS + HALO_H, CONV_WIDTH), _F32),
        pltpu.VMEM((TS, POOL_WIDTH), _F32),
        pltpu.VMEM((TS, CONV_WIDTH), _F32),
        pltpu.VMEM((TS, ATTN_WIDTH), _F32),
        pltpu.VMEM((TS, POOL_WIDTH), _BF16),
        pltpu.VMEM((TS, CONV_WIDTH), _BF16),
        pltpu.VMEM((N_KV_HEADS, TS + BLOCK, HEAD_DIM), _BF16),
        pltpu.VMEM((N_KV_HEADS, TS + BLOCK, 2 * LANES), _BF16),
        pltpu.VMEM((N_KV_HEADS, TS + BLOCK, 2 * LANES), _BF16),
        pltpu.VMEM((N_Q_HEADS, TS, HEAD_DIM), _BF16),
        pltpu.VMEM((TS, D_MIX), _BF16),
        pltpu.VMEM((2, N_KV_HEADS, Q_PER_KV * BLOCK, 2 * BLOCK), _F32),
        pltpu.VMEM((D_MODEL, D_IN), _BF16),
        pltpu.VMEM((D_MIX, D_MODEL), _BF16),
    ]
    return pl.pallas_call(
        functools.partial(_layer_body, layer=layer, final=final),
        out_shape=jax.ShapeDtypeStruct((batch, seq, D_MODEL), _F32),
        grid=(batch, n_tiles),
        in_specs=in_specs,
        out_specs=pl.BlockSpec((1, TS, D_MODEL), lambda b, j: (b, j, 0)),
        scratch_shapes=scratch,
        compiler_params=pltpu.CompilerParams(
            dimension_semantics=("arbitrary", "arbitrary"),
            vmem_limit_bytes=VMEM_LIMIT_BYTES,
        ),
        name="hybrid_layer_final" if final else "hybrid_layer",
    )


def kernel(x, ln_g, w_in, pool_w, pool_scale, conv_dw, conv_b, conv_ln_g, conv_ln_b, conv_pw, attn_sinks, w_out, final_g):
    batch, seq, d_model = x.shape
    assert d_model == D_MODEL and seq % TS == 0 and TS % BLOCK == 0
    bias = jnp.asarray(_bias_table())
    inv_cnt = jnp.asarray(_inv_count_table())
    poolw_bd = jnp.stack([jax.scipy.linalg.block_diag(*[pool_w[l, g] for g in range(POOL_GROUPS)])
                          for l in range(DEPTH)])
    params = (ln_g, w_in, poolw_bd, pool_scale, inv_cnt, conv_dw, conv_b, conv_ln_g, conv_ln_b, conv_pw,
              attn_sinks, bias, w_out, final_g.reshape(1, -1))
    for l in range(DEPTH):
        x = _layer_call(batch, seq, l, final=(l == DEPTH - 1))(x, *params)
    return x
```
